```python
import jax, jax.numpy as jnp
from jax import lax
import numpy as np

D_MODEL = 1024
BATCH = 8
SEQ = 2048
DEPTH = 2
DEC_BATCH = 128
DEC_SEQ = 1
PAST_LEN = 16384
PAGE_SIZE = 128

N_EVEN = (DEPTH + 1) // 2
N_ODD = DEPTH // 2
D_POOL = D_MODEL // 4
POOL_WINDOWS = (2, 4, 8, 16)
N_POOL_GROUPS = len(POOL_WINDOWS)
POOL_GROUP_DIM = D_POOL // N_POOL_GROUPS
POOL_BUF = max(POOL_WINDOWS) - 1
D_HGRN = D_MODEL - D_POOL
HGRN_HEAD_DIM = 128
N_HGRN_HEADS = D_HGRN // HGRN_HEAD_DIM
HGRN_CHUNK = 32
D_EVEN_IN = D_POOL + 4 * D_HGRN
D_CONV = D_MODEL
CONV_WIDTH = 3
CONV_BUF = CONV_WIDTH - 1
D_FF = 4 * D_MODEL
EPS = 1e-6

kernel_name = "pool_hgrn2_shortconv_hybrid_step"


def rmsnorm(x, g):
    xf = x.astype(jnp.float32)
    r = xf * lax.rsqrt(jnp.mean(xf * xf, axis=-1, keepdims=True) + EPS)
    return (r * g.astype(jnp.float32)).astype(x.dtype)


def pool_mix(u, prev, pos0, w_pool, scale):
    b_, t_, _ = u.shape
    ext = jnp.concatenate([prev.astype(u.dtype), u], axis=1)
    ext32 = ext.astype(jnp.float32)
    cs = jnp.pad(jnp.cumsum(ext32, axis=1), ((0, 0), (1, 0), (0, 0)))
    pos = pos0 + jnp.arange(t_, dtype=jnp.int32)
    means = []
    for g, w in enumerate(POOL_WINDOWS):
        sl = slice(g * POOL_GROUP_DIM, (g + 1) * POOL_GROUP_DIM)
        s = cs[:, POOL_BUF + 1:POOL_BUF + 1 + t_, sl] - cs[:, POOL_BUF + 1 - w:POOL_BUF + 1 - w + t_, sl]
        cnt = jnp.minimum(w, pos + 1).astype(jnp.float32)
        means.append(s / cnt[None, :, None])
    mean = jnp.stack(means, axis=2)
    diff = mean - u.astype(jnp.float32).reshape(b_, t_, N_POOL_GROUPS, POOL_GROUP_DIM)
    out = jnp.einsum('btgc,gcd->btgd', diff.astype(u.dtype), w_pool).reshape(b_, t_, D_POOL) * scale
    return out, ext[:, -POOL_BUF:]


def hgrn2_chunked(q, k, v, logf, s0):
    b_, t_, h_, _ = q.shape
    dv = v.shape[-1]
    cs = min(HGRN_CHUNK, t_)
    n = -(-t_ // cs)
    pad = n * cs - t_

    def blocks(a):
        a = jnp.pad(a, ((0, 0), (0, pad), (0, 0), (0, 0)))
        return jnp.moveaxis(a.reshape(b_, n, cs, h_, a.shape[-1]), 1, 0)

    mask = jnp.tril(jnp.ones((cs, cs), dtype=bool))

    def step(s, blk):
        qc, kc, vc, gc = blk
        bcum = jnp.cumsum(gc, axis=1)
        q_dec = qc * jnp.exp(bcum)
        k_dec = kc * jnp.exp(-bcum)
        inter = jnp.einsum('bthk,bhkv->bthv', q_dec, s)
        scores = jnp.where(mask, jnp.einsum('bthk,bshk->bhts', q_dec, k_dec), 0.0)
        intra = jnp.einsum('bhts,bshv->bthv', scores, vc)
        b_last = bcum[:, -1]
        k_end = kc * jnp.exp(b_last[:, None] - bcum)
        s_new = s * jnp.exp(b_last)[..., None] + jnp.einsum('bshk,bshv->bhkv', k_end, vc)
        return s_new, inter + intra

    s_fin, o = lax.scan(step, s0, (blocks(q), blocks(k), blocks(v), blocks(logf)))
    o = jnp.moveaxis(o, 0, 1).reshape(b_, n * cs, h_, dv)[:, :t_]
    return o, s_fin


def even_layer(h, pool_prev, s_prev, pos0, lb, w_in, w_pool, pool_scale, hgrn_gain, w_out):
    b_, t_, _ = h.shape
    proj = h @ w_in
    u = proj[..., :D_POOL]
    q, fz, iz, gz = jnp.split(proj[..., D_POOL:], 4, axis=-1)
    pool_out, pool_new = pool_mix(u, pool_prev, pos0, w_pool, pool_scale)
    lb32 = lb.astype(jnp.float32)
    fz32 = fz.astype(jnp.float32)
    logf = jnp.log(lb32 + (1.0 - lb32) * jax.nn.sigmoid(fz32))
    k = (1.0 - lb32) * jax.nn.sigmoid(-fz32)
    v = jax.nn.silu(iz.astype(jnp.float32))

    def heads(a):
        return a.reshape(b_, t_, N_HGRN_HEADS, HGRN_HEAD_DIM)

    o, s_new = hgrn2_chunked(heads(q.astype(jnp.float32)), heads(k), heads(v), heads(logf),
                             s_prev.astype(jnp.float32))
    o = o * lax.rsqrt(jnp.mean(o * o, axis=-1, keepdims=True) + EPS)
    o = o.reshape(b_, t_, D_HGRN) * hgrn_gain.astype(jnp.float32) * jax.nn.sigmoid(gz.astype(jnp.float32))
    mixed = jnp.concatenate([pool_out, o.astype(h.dtype)], axis=-1) @ w_out
    return mixed, pool_new, s_new.astype(s_prev.dtype)


def odd_layer(h, conv_prev, w_in, conv_w, w_out):
    t_ = h.shape[1]
    bg, cg, hv = jnp.split(h @ w_in, 3, axis=-1)
    z = cg * hv
    ext = jnp.concatenate([conv_prev.astype(z.dtype), z], axis=1)
    conv = conv_w[0] * ext[:, 0:t_]
    for j in range(1, CONV_WIDTH):
        conv = conv + conv_w[j] * ext[:, j:j + t_]
    return (bg * conv) @ w_out, ext[:, -CONV_BUF:]


def trunk(x, pool_st, hgrn_st, conv_st, pos0, norm_mix, norm_mlp, norm_final, even_w_in, pool_w,
          pool_scale, hgrn_lb_logits, hgrn_gain, even_w_out, odd_w_in, conv_w, odd_w_out, ff_w1, ff_w2):
    lb_all = jnp.cumsum(jax.nn.softmax(hgrn_lb_logits.astype(jnp.float32), axis=0), axis=0)
    new_pool, new_hgrn, new_conv = [], [], []
    for l in range(DEPTH):
        h = rmsnorm(x, norm_mix[l])
        if l % 2 == 0:
            e = l // 2
            mixed, p_new, s_new = even_layer(h, pool_st[e], hgrn_st[e], pos0, lb_all[l], even_w_in[e],
                                             pool_w[e], pool_scale[e], hgrn_gain[e], even_w_out[e])
            new_pool.append(p_new)
            new_hgrn.append(s_new)
        else:
            o = l // 2
            mixed, c_new = odd_layer(h, conv_st[o], odd_w_in[o], conv_w[o], odd_w_out[o])
            new_conv.append(c_new)
        x = x + mixed
        h = rmsnorm(x, norm_mlp[l])
        x = x + jnp.square(jax.nn.relu(h @ ff_w1[l])) @ ff_w2[l]
    return rmsnorm(x, norm_final), jnp.stack(new_pool), jnp.stack(new_hgrn), jnp.stack(new_conv)


def setup_inputs(seed: int = 0) -> dict:
    key = jax.random.key(seed)
    ks = jax.random.split(key, 20)
    f32 = jnp.float32

    def nrm(k, shape, scale):
        return jax.random.normal(k, shape, f32) * scale

    return {
        "x_prompt": nrm(ks[0], (BATCH, SEQ, D_MODEL), 1.0),
        "x_sample": nrm(ks[1], (DEC_BATCH, DEC_SEQ, D_MODEL), 1.0),
        "state_pool": nrm(ks[2], (N_EVEN, DEC_BATCH, POOL_BUF, D_POOL), 1.0),
        "state_hgrn": nrm(ks[3], (N_EVEN, DEC_BATCH, N_HGRN_HEADS, HGRN_HEAD_DIM, HGRN_HEAD_DIM), 0.5),
        "state_conv": nrm(ks[4], (N_ODD, DEC_BATCH, CONV_BUF, D_CONV), 0.5),
        "norm_mix": 1.0 + nrm(ks[5], (DEPTH, D_MODEL), 0.02),
        "norm_mlp": 1.0 + nrm(ks[6], (DEPTH, D_MODEL), 0.02),
        "norm_final": 1.0 + nrm(ks[7], (D_MODEL,), 0.02),
        "even_w_in": nrm(ks[8], (N_EVEN, D_MODEL, D_EVEN_IN), D_MODEL ** -0.5),
        "pool_w": nrm(ks[9], (N_EVEN, N_POOL_GROUPS, POOL_GROUP_DIM, POOL_GROUP_DIM), POOL_GROUP_DIM ** -0.5),
        "pool_scale": 0.5 + nrm(ks[10], (N_EVEN, D_POOL), 0.05),
        "hgrn_lb_logits": nrm(ks[11], (DEPTH + 1, D_HGRN), 0.1),
        "hgrn_gain": 1.0 + nrm(ks[12], (N_EVEN, D_HGRN), 0.02),
        "even_w_out": nrm(ks[13], (N_EVEN, D_MODEL, D_MODEL), D_MODEL ** -0.5),
        "odd_w_in": nrm(ks[14], (N_ODD, D_MODEL, 3 * D_CONV), D_MODEL ** -0.5),
        "conv_w": nrm(ks[15], (N_ODD, CONV_WIDTH, D_CONV), CONV_WIDTH ** -0.5),
        "odd_w_out": nrm(ks[16], (N_ODD, D_CONV, D_MODEL), D_CONV ** -0.5),
        "ff_w1": nrm(ks[17], (DEPTH, D_MODEL, D_FF), D_MODEL ** -0.5),
        "ff_w2": nrm(ks[18], (DEPTH, D_FF, D_MODEL), D_FF ** -0.5),
    }


def reference(x_prompt, x_sample, state_pool, state_hgrn, state_conv, norm_mix, norm_mlp, norm_final,
              even_w_in, pool_w, pool_scale, hgrn_lb_logits, hgrn_gain, even_w_out, odd_w_in, conv_w,
              odd_w_out, ff_w1, ff_w2):
    bp = x_prompt.shape[0]
    zero_pool = jnp.zeros((N_EVEN, bp, POOL_BUF, D_POOL), x_prompt.dtype)
    zero_hgrn = jnp.zeros((N_EVEN, bp, N_HGRN_HEADS, HGRN_HEAD_DIM, HGRN_HEAD_DIM), state_hgrn.dtype)
    zero_conv = jnp.zeros((N_ODD, bp, CONV_BUF, D_CONV), x_prompt.dtype)
    y_prompt, new_pool_prompt, new_hgrn_prompt, new_conv_prompt = trunk(
        x_prompt, zero_pool, zero_hgrn, zero_conv, 0, norm_mix, norm_mlp, norm_final, even_w_in, pool_w,
        pool_scale, hgrn_lb_logits, hgrn_gain, even_w_out, odd_w_in, conv_w, odd_w_out, ff_w1, ff_w2)
    y_sample, new_pool_sample, new_hgrn_sample, new_conv_sample = trunk(
        x_sample, state_pool, state_hgrn, state_conv, PAST_LEN, norm_mix, norm_mlp, norm_final, even_w_in,
        pool_w, pool_scale, hgrn_lb_logits, hgrn_gain, even_w_out, odd_w_in, conv_w, odd_w_out, ff_w1, ff_w2)
    return (y_prompt, y_sample, new_pool_prompt, new_hgrn_prompt, new_conv_prompt,
            new_pool_sample, new_hgrn_sample, new_conv_sample)
```

```python
import functools

import jax
import jax.numpy as jnp
from jax import lax
from jax.experimental import pallas as pl
from jax.experimental.pallas import tpu as pltpu

F32 = jnp.float32
BF16 = jnp.bfloat16

EPS = 1e-6
PAST_LEN = 16384
POOL_WINDOWS = (2, 4, 8, 16)
POOL_GROUP_DIM = 64
POOL_BUF = max(POOL_WINDOWS) - 1
POOL_HIST = 16
HGRN_CHUNK = 32
HEAD = 128
CONV_WIDTH = 3
CONV_BUF = CONV_WIDTH - 1
CONV_HIST = 8

VMEM_LIMIT_V7X = 56 * 1024 * 1024


def _resident(shape):
    nd = len(shape)
    return pl.BlockSpec(shape, lambda *_: (0,) * nd, pipeline_mode=pl.Buffered(1))


def _rms(x, g):
    return x * lax.rsqrt(jnp.mean(x * x, axis=-1, keepdims=True) + EPS) * g


def _sigmoid(x):
    return 1.0 / (1.0 + jnp.exp(-x))


def _lower_bound(lbl, layer):
    e = jnp.exp(lbl - jnp.max(lbl, axis=0, keepdims=True))
    p = e / jnp.sum(e, axis=0, keepdims=True)
    return jnp.sum(p[0:layer + 1], axis=0, keepdims=True)


def _pool_select(snaps, u_shape):
    grp = lax.broadcasted_iota(jnp.int32, u_shape, 1) // POOL_GROUP_DIM
    s = snaps[POOL_WINDOWS[-1]]
    for g in range(len(POOL_WINDOWS) - 2, -1, -1):
        s = jnp.where(grp == g, snaps[POOL_WINDOWS[g]], s)
    win = jnp.left_shift(2, grp)
    return s, win


def _norm_proj_kernel(x_ref, g_ref, w_ref, o_ref):
    h = _rms(x_ref[...], g_ref[...]).astype(BF16)
    o_ref[...] = jnp.dot(h, w_ref[...], preferred_element_type=F32)


def norm_proj(x, g, w, *, bm):
    m, d = x.shape
    n = w.shape[1]
    return pl.pallas_call(
        _norm_proj_kernel,
        grid=(m // bm,),
        in_specs=[pl.BlockSpec((bm, d), lambda i: (i, 0)), _resident((1, d)), _resident((d, n))],
        out_specs=pl.BlockSpec((bm, n), lambda i: (i, 0)),
        out_shape=jax.ShapeDtypeStruct((m, n), F32),
        compiler_params=pltpu.CompilerParams(dimension_semantics=("arbitrary",),
                                             vmem_limit_bytes=VMEM_LIMIT_V7X),
        name="norm_proj",
    )(x, g, w)


def _out_mlp_kernel(m_ref, x_ref, wout_ref, g_ref, w1_ref, w2_ref, gf_ref, o_ref, *, final_norm, ff_chunk):
    y = x_ref[...] + jnp.dot(m_ref[...], wout_ref[...], preferred_element_type=F32)
    h = _rms(y, g_ref[...]).astype(BF16)
    acc = y
    d_ff = w1_ref.shape[1]
    for c in range(d_ff // ff_chunk):
        a = jnp.dot(h, w1_ref[:, c * ff_chunk:(c + 1) * ff_chunk], preferred_element_type=F32)
        a = jnp.square(jnp.maximum(a, 0.0)).astype(BF16)
        acc = acc + jnp.dot(a, w2_ref[c * ff_chunk:(c + 1) * ff_chunk, :], preferred_element_type=F32)
    if final_norm:
        acc = _rms(acc, gf_ref[...])
    o_ref[...] = acc


def out_mlp(mixed, x, w_out, g_mlp, w1, w2, g_final, *, final_norm, bm):
    m, d = x.shape
    d_ff = w1.shape[1]
    kern = functools.partial(_out_mlp_kernel, final_norm=final_norm, ff_chunk=1024)
    return pl.pallas_call(
        kern,
        grid=(m // bm,),
        in_specs=[pl.BlockSpec((bm, d), lambda i: (i, 0)), pl.BlockSpec((bm, d), lambda i: (i, 0)),
                  _resident((d, d)), _resident((1, d)), _resident((d, d_ff)), _resident((d_ff, d)),
                  _resident((1, d))],
        out_specs=pl.BlockSpec((bm, d), lambda i: (i, 0)),
        out_shape=jax.ShapeDtypeStruct((m, d), F32),
        compiler_params=pltpu.CompilerParams(dimension_semantics=("arbitrary",),
                                             vmem_limit_bytes=VMEM_LIMIT_V7X),
        name="out_mlp",
    )(mixed, x, w_out, g_mlp, w1, w2, g_final)


def _even_prompt_kernel(x_ref, g_ref, win_ref, wpool_ref, pscale_ref, lbl_ref, gain_ref,
                        mixed_ref, pstate_ref, hstate_ref, proj_ref, ext_ref, st_ref, *, layer, tb):
    j = pl.program_id(1)
    last = pl.num_programs(1) - 1
    d_pool = wpool_ref.shape[0]
    n_heads = st_ref.shape[0]
    d_hgrn = n_heads * HEAD
    nc = tb // HGRN_CHUNK

    @pl.when(j == 0)
    def _():
        ext_ref[0:POOL_HIST, :] = jnp.zeros((POOL_HIST, d_pool), F32)
        st_ref[...] = jnp.zeros(st_ref.shape, F32)

    h = _rms(x_ref[...], g_ref[...]).astype(BF16)
    proj_ref[...] = jnp.dot(h, win_ref[...], preferred_element_type=F32)

    u = proj_ref[:, 0:d_pool]
    ext_ref[POOL_HIST:POOL_HIST + tb, :] = u
    acc = u
    snaps = {}
    for s in range(1, POOL_WINDOWS[-1]):
        acc = acc + ext_ref[pl.ds(POOL_HIST - s, tb), :]
        if s + 1 in POOL_WINDOWS:
            snaps[s + 1] = acc
    ssum, win = _pool_select(snaps, u.shape)
    pos = j * tb + lax.broadcasted_iota(jnp.int32, u.shape, 0)
    cnt = jnp.minimum(win, pos + 1).astype(F32)
    diff = (ssum / cnt - u).astype(BF16)
    pool_out = jnp.dot(diff, wpool_ref[...], preferred_element_type=F32) * pscale_ref[...]
    mixed_ref[:, 0:d_pool] = pool_out.astype(BF16)

    @pl.when(j == last)
    def _():
        pstate_ref[...] = ext_ref[pl.ds(POOL_HIST + tb - POOL_BUF, POOL_BUF), :]

    ext_ref[0:POOL_HIST, :] = ext_ref[tb:tb + POOL_HIST, :]

    lb = _lower_bound(lbl_ref[...], layer)
    r = lax.broadcasted_iota(jnp.int32, (tb, tb), 0)
    c_ = lax.broadcasted_iota(jnp.int32, (tb, tb), 1)
    tri = ((r // HGRN_CHUNK == c_ // HGRN_CHUNK) & (c_ <= r)).astype(BF16)
    rr = lax.broadcasted_iota(jnp.int32, (HGRN_CHUNK, HGRN_CHUNK), 0)
    cc = lax.broadcasted_iota(jnp.int32, (HGRN_CHUNK, HGRN_CHUNK), 1)
    causal = (cc <= rr)[None]

    for hd in range(n_heads):
        o0 = d_pool + hd * HEAD
        q = proj_ref[:, o0:o0 + HEAD]
        fz = proj_ref[:, o0 + d_hgrn:o0 + d_hgrn + HEAD]
        iz = proj_ref[:, o0 + 2 * d_hgrn:o0 + 2 * d_hgrn + HEAD]
        gz = proj_ref[:, o0 + 3 * d_hgrn:o0 + 3 * d_hgrn + HEAD]
        lbh = lb[:, hd * HEAD:(hd + 1) * HEAD]
        logf = jnp.log(lbh + (1.0 - lbh) * _sigmoid(fz))
        k = (1.0 - lbh) * _sigmoid(-fz)
        v = iz * _sigmoid(iz)

        l_hi = logf.astype(BF16)
        r1 = logf - l_hi.astype(F32)
        l_mid = r1.astype(BF16)
        l_lo = (r1 - l_mid.astype(F32)).astype(BF16)
        bcum = (jnp.dot(tri, l_hi, preferred_element_type=F32)
                + jnp.dot(tri, l_mid, preferred_element_type=F32)
                + jnp.dot(tri, l_lo, preferred_element_type=F32))
        bc3 = bcum.reshape(nc, HGRN_CHUNK, HEAD)
        blast = bc3[:, HGRN_CHUNK - 1:HGRN_CHUNK, :]
        k3 = k.reshape(nc, HGRN_CHUNK, HEAD)
        qd3 = (q * jnp.exp(bcum)).astype(BF16).reshape(nc, HGRN_CHUNK, HEAD)
        kd3 = (k * jnp.exp(-bcum)).astype(BF16).reshape(nc, HGRN_CHUNK, HEAD)
        ke3 = (k3 * jnp.exp(blast - bc3)).astype(BF16)
        v3 = v.astype(BF16).reshape(nc, HGRN_CHUNK, HEAD)

        scores = jnp.einsum('ctk,csk->cts', qd3, kd3, preferred_element_type=F32)
        scores = jnp.where(causal, scores, 0.0).astype(BF16)
        intra = jnp.einsum('cts,csv->ctv', scores, v3, preferred_element_type=F32)

        st = st_ref[hd]
        inter = []
        for c in range(nc):
            inter.append(lax.dot_general(qd3[c], st.astype(BF16), (((1,), (1,)), ((), ())),
                                         preferred_element_type=F32))
            d_st = lax.dot_general(v3[c], ke3[c], (((0,), (0,)), ((), ())), preferred_element_type=F32)
            st = st * jnp.exp(blast[c]) + d_st
        st_ref[hd] = st
        o = intra.reshape(tb, HEAD) + jnp.concatenate(inter, axis=0)
        o = o * lax.rsqrt(jnp.mean(o * o, axis=-1, keepdims=True) + EPS)
        o = o * gain_ref[:, hd * HEAD:(hd + 1) * HEAD] * _sigmoid(gz)
        mixed_ref[:, o0:o0 + HEAD] = o.astype(BF16)

        @pl.when(j == last)
        def _():
            hstate_ref[hd] = st.T


def even_mix_prompt(x, g, w_in, wpool_bd, pscale, lbl, gain, *, layer, tb):
    b, t, d = x.shape
    n_in = w_in.shape[1]
    d_pool = wpool_bd.shape[0]
    d_hgrn = gain.shape[1]
    n_heads = d_hgrn // HEAD
    kern = functools.partial(_even_prompt_kernel, layer=layer, tb=tb)
    return pl.pallas_call(
        kern,
        grid=(b, t // tb),
        in_specs=[pl.BlockSpec((None, tb, d), lambda i, j: (i, j, 0)),
                  _resident((1, d)), _resident((d, n_in)), _resident((d_pool, d_pool)),
                  _resident((1, d_pool)), _resident(lbl.shape), _resident((1, d_hgrn))],
        out_specs=[pl.BlockSpec((None, tb, d), lambda i, j: (i, j, 0)),
                   pl.BlockSpec((None, POOL_BUF, d_pool), lambda i, j: (i, 0, 0)),
                   pl.BlockSpec((None, n_heads, HEAD, HEAD), lambda i, j: (i, 0, 0, 0))],
        out_shape=[jax.ShapeDtypeStruct((b, t, d), BF16),
                   jax.ShapeDtypeStruct((b, POOL_BUF, d_pool), F32),
                   jax.ShapeDtypeStruct((b, n_heads, HEAD, HEAD), F32)],
        scratch_shapes=[pltpu.VMEM((tb, n_in), F32),
                        pltpu.VMEM((POOL_HIST + tb, d_pool), F32),
                        pltpu.VMEM((n_heads, HEAD, HEAD), F32)],
        compiler_params=pltpu.CompilerParams(dimension_semantics=("arbitrary", "arbitrary"),
                                             vmem_limit_bytes=VMEM_LIMIT_V7X),
        name="even_mix_prompt",
    )(x, g, w_in, wpool_bd, pscale, lbl, gain)


def _even_sample_kernel(proj_ref, pool_ref, hst_ref, wpool_ref, pscale_ref, lbl_ref, gain_ref,
                        mixed_ref, npool_ref, nhst_ref, f_ref, k_ref, q_ref, v_ref, o_ref, *, layer, pos0):
    bb = proj_ref.shape[0]
    d_pool = wpool_ref.shape[0]
    n_heads = hst_ref.shape[1]
    d_hgrn = n_heads * HEAD

    u = proj_ref[:, 0:d_pool]
    acc = u
    snaps = {}
    for s in range(1, POOL_WINDOWS[-1]):
        r0 = (POOL_BUF - s) * d_pool
        acc = acc + pool_ref[:, r0:r0 + d_pool]
        if s + 1 in POOL_WINDOWS:
            snaps[s + 1] = acc
    ssum, win = _pool_select(snaps, u.shape)
    cnt = jnp.minimum(win, pos0 + 1).astype(F32)
    diff = (ssum / cnt - u).astype(BF16)
    pool_out = jnp.dot(diff, wpool_ref[...], preferred_element_type=F32) * pscale_ref[...]
    mixed_ref[:, 0:d_pool] = pool_out.astype(BF16)
    npool_ref[:, 0:(POOL_BUF - 1) * d_pool] = pool_ref[:, d_pool:POOL_BUF * d_pool]
    npool_ref[:, (POOL_BUF - 1) * d_pool:POOL_BUF * d_pool] = u

    lb = _lower_bound(lbl_ref[...], layer)
    fz = proj_ref[:, d_pool + d_hgrn:d_pool + 2 * d_hgrn]
    iz = proj_ref[:, d_pool + 2 * d_hgrn:d_pool + 3 * d_hgrn]
    f_ref[...] = lb + (1.0 - lb) * _sigmoid(fz)
    k_ref[...] = (1.0 - lb) * _sigmoid(-fz)
    q_ref[...] = proj_ref[:, d_pool:d_pool + d_hgrn]
    v_ref[...] = iz * _sigmoid(iz)
    eye = (lax.broadcasted_iota(jnp.int32, (HEAD, HEAD), 0)
           == lax.broadcasted_iota(jnp.int32, (HEAD, HEAD), 1))

    def to_column(row):
        return jnp.sum(jnp.where(eye, row, 0.0), axis=1, keepdims=True)

    for i in range(bb):
        for hd in range(n_heads):
            sl = slice(hd * HEAD, (hd + 1) * HEAD)
            f_col = to_column(f_ref[i:i + 1, sl])
            k_col = to_column(k_ref[i:i + 1, sl])
            q_col = to_column(q_ref[i:i + 1, sl])
            s_new = f_col * hst_ref[i, hd] + k_col * v_ref[i:i + 1, sl]
            nhst_ref[i, hd] = s_new
            o_ref[i:i + 1, sl] = jnp.sum(q_col * s_new, axis=0, keepdims=True)

    gz = proj_ref[:, d_pool + 3 * d_hgrn:d_pool + 4 * d_hgrn]
    gate = gain_ref[...] * _sigmoid(gz)
    for hd in range(n_heads):
        sl = slice(hd * HEAD, (hd + 1) * HEAD)
        o = o_ref[:, sl]
        o = o * lax.rsqrt(jnp.mean(o * o, axis=-1, keepdims=True) + EPS)
        mixed_ref[:, d_pool + hd * HEAD:d_pool + (hd + 1) * HEAD] = (o * gate[:, sl]).astype(BF16)


def even_mix_sample(proj, pool_flat, hstate, wpool_bd, pscale, lbl, gain, *, layer, pos0, bb):
    b, n_in = proj.shape
    d_pool = wpool_bd.shape[0]
    d_hgrn = gain.shape[1]
    n_heads = d_hgrn // HEAD
    d = d_pool + d_hgrn
    kern = functools.partial(_even_sample_kernel, layer=layer, pos0=pos0)
    return pl.pallas_call(
        kern,
        grid=(b // bb,),
        in_specs=[pl.BlockSpec((bb, n_in), lambda i: (i, 0)),
                  pl.BlockSpec((bb, POOL_BUF * d_pool), lambda i: (i, 0)),
                  pl.BlockSpec((bb, n_heads, HEAD, HEAD), lambda i: (i, 0, 0, 0)),
                  _resident((d_pool, d_pool)), _resident((1, d_pool)), _resident(lbl.shape),
                  _resident((1, d_hgrn))],
        out_specs=[pl.BlockSpec((bb, d), lambda i: (i, 0)),
                   pl.BlockSpec((bb, POOL_BUF * d_pool), lambda i: (i, 0)),
                   pl.BlockSpec((bb, n_heads, HEAD, HEAD), lambda i: (i, 0, 0, 0))],
        out_shape=[jax.ShapeDtypeStruct((b, d), BF16),
                   jax.ShapeDtypeStruct((b, POOL_BUF * d_pool), F32),
                   jax.ShapeDtypeStruct((b, n_heads, HEAD, HEAD), F32)],
        scratch_shapes=[pltpu.VMEM((bb, d_hgrn), F32)] * 5,
        compiler_params=pltpu.CompilerParams(dimension_semantics=("arbitrary",),
                                             vmem_limit_bytes=VMEM_LIMIT_V7X),
        name="even_mix_sample",
    )(proj, pool_flat, hstate, wpool_bd, pscale, lbl, gain)


def _odd_prompt_kernel(x_ref, g_ref, win_ref, cw_ref, mixed_ref, cstate_ref, zext_ref, *, tb):
    j = pl.program_id(1)
    last = pl.num_programs(1) - 1
    dc = cw_ref.shape[1]

    @pl.when(j == 0)
    def _():
        zext_ref[0:CONV_HIST, :] = jnp.zeros((CONV_HIST, dc), F32)

    h = _rms(x_ref[...], g_ref[...]).astype(BF16)
    cg = jnp.dot(h, win_ref[:, dc:2 * dc], preferred_element_type=F32)
    hv = jnp.dot(h, win_ref[:, 2 * dc:3 * dc], preferred_element_type=F32)
    z = cg * hv
    zext_ref[CONV_HIST:CONV_HIST + tb, :] = z
    conv = cw_ref[CONV_WIDTH - 1:CONV_WIDTH, :] * z
    for s in range(1, CONV_WIDTH):
        conv = conv + cw_ref[CONV_WIDTH - 1 - s:CONV_WIDTH - s, :] * zext_ref[pl.ds(CONV_HIST - s, tb), :]
    bg = jnp.dot(h, win_ref[:, 0:dc], preferred_element_type=F32)
    mixed_ref[...] = (bg * conv).astype(BF16)

    @pl.when(j == last)
    def _():
        cstate_ref[...] = zext_ref[pl.ds(CONV_HIST + tb - CONV_BUF, CONV_BUF), :]

    zext_ref[0:CONV_HIST, :] = zext_ref[tb:tb + CONV_HIST, :]


def odd_mix_prompt(x, g, w_in, conv_w, *, tb):
    b, t, d = x.shape
    dc = conv_w.shape[1]
    kern = functools.partial(_odd_prompt_kernel, tb=tb)
    return pl.pallas_call(
        kern,
        grid=(b, t // tb),
        in_specs=[pl.BlockSpec((None, tb, d), lambda i, j: (i, j, 0)),
                  _resident((1, d)), _resident((d, 3 * dc)), _resident((CONV_WIDTH, dc))],
        out_specs=[pl.BlockSpec((None, tb, dc), lambda i, j: (i, j, 0)),
                   pl.BlockSpec((None, CONV_BUF, dc), lambda i, j: (i, 0, 0))],
        out_shape=[jax.ShapeDtypeStruct((b, t, dc), BF16),
                   jax.ShapeDtypeStruct((b, CONV_BUF, dc), F32)],
        scratch_shapes=[pltpu.VMEM((CONV_HIST + tb, dc), F32)],
        compiler_params=pltpu.CompilerParams(dimension_semantics=("arbitrary", "arbitrary"),
                                             vmem_limit_bytes=VMEM_LIMIT_V7X),
        name="odd_mix_prompt",
    )(x, g, w_in, conv_w)


def _odd_sample_kernel(x_ref, g_ref, win_ref, cw_ref, cst_ref, mixed_ref, ncst_ref):
    dc = cw_ref.shape[1]
    h = _rms(x_ref[...], g_ref[...]).astype(BF16)
    cg = jnp.dot(h, win_ref[:, dc:2 * dc], preferred_element_type=F32)
    hv = jnp.dot(h, win_ref[:, 2 * dc:3 * dc], preferred_element_type=F32)
    z = cg * hv
    conv = cw_ref[CONV_WIDTH - 1:CONV_WIDTH, :] * z
    for s in range(1, CONV_WIDTH):
        r0 = (CONV_BUF - s) * dc
        conv = conv + cw_ref[CONV_WIDTH - 1 - s:CONV_WIDTH - s, :] * cst_ref[:, r0:r0 + dc]
    bg = jnp.dot(h, win_ref[:, 0:dc], preferred_element_type=F32)
    mixed_ref[...] = (bg * conv).astype(BF16)
    ncst_ref[:, 0:(CONV_BUF - 1) * dc] = cst_ref[:, dc:CONV_BUF * dc]
    ncst_ref[:, (CONV_BUF - 1) * dc:CONV_BUF * dc] = z


def odd_mix_sample(x, g, w_in, conv_w, cstate_flat):
    b, d = x.shape
    dc = conv_w.shape[1]
    return pl.pallas_call(
        _odd_sample_kernel,
        grid=(1,),
        in_specs=[_resident((b, d)), _resident((1, d)), _resident((d, 3 * dc)),
                  _resident((CONV_WIDTH, dc)), _resident((b, CONV_BUF * dc))],
        out_specs=[pl.BlockSpec((b, dc), lambda i: (0, 0)),
                   pl.BlockSpec((b, CONV_BUF * dc), lambda i: (0, 0))],
        out_shape=[jax.ShapeDtypeStruct((b, dc), BF16),
                   jax.ShapeDtypeStruct((b, CONV_BUF * dc), F32)],
        compiler_params=pltpu.CompilerParams(dimension_semantics=("arbitrary",),
                                             vmem_limit_bytes=VMEM_LIMIT_V7X),
        name="odd_mix_sample",
    )(x, g, w_in, conv_w, cstate_flat)


def _block_diag(w):
    g, c, _ = w.shape
    rows = [jnp.pad(w[i], ((0, 0), (i * c, (g - 1 - i) * c))) for i in range(g)]
    return jnp.concatenate(rows, axis=0)


def _trunk(x, states, pos0, p, *, prompt):
    depth = p["norm_mix"].shape[0]
    if prompt:
        b, t, d = x.shape
        xf = x.reshape(b * t, d)
        bm = 512
    else:
        b, d = x.shape
        xf = x
        bm = b
    new_pool, new_hgrn, new_conv = [], [], []
    for l in range(depth):
        g_mix = p["norm_mix"][l][None]
        final = l == depth - 1
        if l % 2 == 0:
            e = l // 2
            if prompt:
                mixed, p_new, s_new = even_mix_prompt(
                    xf.reshape(b, t, d), g_mix, p["even_w_in"][e], p["pool_bd"][e], p["pool_scale"][e][None],
                    p["hgrn_lb_logits"], p["hgrn_gain"][e][None], layer=l, tb=256)
                mixed = mixed.reshape(b * t, d)
            else:
                pool_st, hgrn_st, _ = states
                proj = norm_proj(xf, g_mix, p["even_w_in"][e], bm=bm)
                mixed, p_new, s_new = even_mix_sample(
                    proj, pool_st[e].reshape(b, -1), hgrn_st[e], p["pool_bd"][e], p["pool_scale"][e][None],
                    p["hgrn_lb_logits"], p["hgrn_gain"][e][None], layer=l, pos0=pos0, bb=8)
                p_new = p_new.reshape(b, POOL_BUF, -1)
            new_pool.append(p_new)
            new_hgrn.append(s_new)
            w_out = p["even_w_out"][e]
        else:
            o = l // 2
            if prompt:
                mixed, c_new = odd_mix_prompt(xf.reshape(b, t, d), g_mix, p["odd_w_in"][o], p["conv_w"][o], tb=512)
                mixed = mixed.reshape(b * t, d)
            else:
                conv_st = states[2]
                mixed, c_new = odd_mix_sample(xf, g_mix, p["odd_w_in"][o], p["conv_w"][o],
                                              conv_st[o].reshape(b, -1))
                c_new = c_new.reshape(b, CONV_BUF, -1)
            new_conv.append(c_new)
            w_out = p["odd_w_out"][o]
        xf = out_mlp(mixed, xf, w_out, p["norm_mlp"][l][None], p["ff_w1"][l], p["ff_w2"][l],
                     p["norm_final"][None], final_norm=final, bm=bm)
    y = xf.reshape(x.shape) if prompt else xf
    return y, jnp.stack(new_pool), jnp.stack(new_hgrn), jnp.stack(new_conv)


def kernel(x_prompt, x_sample, state_pool, state_hgrn, state_conv, norm_mix, norm_mlp, norm_final, even_w_in, pool_w, pool_scale, hgrn_lb_logits, hgrn_gain, even_w_out, odd_w_in, conv_w, odd_w_out, ff_w1, ff_w2):
    p = dict(norm_mix=norm_mix, norm_mlp=norm_mlp, norm_final=norm_final,
             even_w_in=even_w_in.astype(BF16), pool_bd=[_block_diag(w).astype(BF16) for w in pool_w],
             pool_scale=pool_scale, hgrn_lb_logits=hgrn_lb_logits, hgrn_gain=hgrn_gain,
             even_w_out=even_w_out.astype(BF16), odd_w_in=odd_w_in.astype(BF16), conv_w=conv_w,
             odd_w_out=odd_w_out.astype(BF16), ff_w1=ff_w1.astype(BF16), ff_w2=ff_w2.astype(BF16))
    y_p, pool_p, hgrn_p, conv_p = _trunk(x_prompt, None, 0, p, prompt=True)
    db, ds, d = x_sample.shape
    y_s, pool_s, hgrn_s, conv_s = _trunk(x_sample.reshape(db * ds, d), (state_pool, state_hgrn, state_conv),
                                         PAST_LEN, p, prompt=False)
    return (y_p, y_s.reshape(db, ds, d), pool_p, hgrn_p, conv_p, pool_s, hgrn_s, conv_s)
```

```python
import functools

import jax
import jax.numpy as jnp
from jax import lax
from jax.experimental import pallas as pl
from jax.experimental.pallas import tpu as pltpu

F32 = jnp.float32
BF16 = jnp.bfloat16

EPS = 1e-6
PAST_LEN = 16384
POOL_WINDOWS = (2, 4, 8, 16)
POOL_GROUP_DIM = 64
POOL_BUF = max(POOL_WINDOWS) - 1
POOL_HIST = 16
HGRN_CHUNK = 32
HGRN_BLOCK = 128
HEAD = 128
CONV_WIDTH = 3
CONV_BUF = CONV_WIDTH - 1
CONV_HIST = 8

VMEM_LIMIT_V7X = 56 * 1024 * 1024


def _resident(shape):
    nd = len(shape)
    return pl.BlockSpec(shape, lambda *_: (0,) * nd, pipeline_mode=pl.Buffered(1))


def _rms(x, g):
    return x * lax.rsqrt(jnp.mean(x * x, axis=-1, keepdims=True) + EPS) * g


def _sigmoid(x):
    return 1.0 / (1.0 + jnp.exp(-x))


def _lower_bound(lbl, layer):
    e = jnp.exp(lbl - jnp.max(lbl, axis=0, keepdims=True))
    p = e / jnp.sum(e, axis=0, keepdims=True)
    return jnp.sum(p[0:layer + 1], axis=0, keepdims=True)


def _pool_select(snaps, u_shape):
    grp = lax.broadcasted_iota(jnp.int32, u_shape, 1) // POOL_GROUP_DIM
    s = snaps[POOL_WINDOWS[-1]]
    for g in range(len(POOL_WINDOWS) - 2, -1, -1):
        s = jnp.where(grp == g, snaps[POOL_WINDOWS[g]], s)
    win = jnp.left_shift(2, grp)
    return s, win


def _norm_proj_kernel(x_ref, g_ref, w_ref, o_ref):
    h = _rms(x_ref[...], g_ref[...]).astype(BF16)
    o_ref[...] = jnp.dot(h, w_ref[...], preferred_element_type=F32)


def norm_proj(x, g, w, *, bm):
    m, d = x.shape
    n = w.shape[1]
    return pl.pallas_call(
        _norm_proj_kernel,
        grid=(m // bm,),
        in_specs=[pl.BlockSpec((bm, d), lambda i: (i, 0)), _resident((1, d)), _resident((d, n))],
        out_specs=pl.BlockSpec((bm, n), lambda i: (i, 0)),
        out_shape=jax.ShapeDtypeStruct((m, n), F32),
        compiler_params=pltpu.CompilerParams(dimension_semantics=("arbitrary",),
                                             vmem_limit_bytes=VMEM_LIMIT_V7X),
        name="norm_proj",
    )(x, g, w)


def _out_mlp_kernel(m_ref, x_ref, wout_ref, g_ref, w1_ref, w2_ref, gf_ref, o_ref, *, final_norm, ff_chunk):
    y = x_ref[...] + jnp.dot(m_ref[...], wout_ref[...], preferred_element_type=F32)
    h = _rms(y, g_ref[...]).astype(BF16)
    acc = y
    d_ff = w1_ref.shape[1]
    for c in range(d_ff // ff_chunk):
        a = jnp.dot(h, w1_ref[:, c * ff_chunk:(c + 1) * ff_chunk], preferred_element_type=F32)
        a = jnp.square(jnp.maximum(a, 0.0)).astype(BF16)
        acc = acc + jnp.dot(a, w2_ref[c * ff_chunk:(c + 1) * ff_chunk, :], preferred_element_type=F32)
    if final_norm:
        acc = _rms(acc, gf_ref[...])
    o_ref[...] = acc


def out_mlp(mixed, x, w_out, g_mlp, w1, w2, g_final, *, final_norm, bm):
    m, d = x.shape
    d_ff = w1.shape[1]
    kern = functools.partial(_out_mlp_kernel, final_norm=final_norm, ff_chunk=1024)
    return pl.pallas_call(
        kern,
        grid=(m // bm,),
        in_specs=[pl.BlockSpec((bm, d), lambda i: (i, 0)), pl.BlockSpec((bm, d), lambda i: (i, 0)),
                  _resident((d, d)), _resident((1, d)), _resident((d, d_ff)), _resident((d_ff, d)),
                  _resident((1, d))],
        out_specs=pl.BlockSpec((bm, d), lambda i: (i, 0)),
        out_shape=jax.ShapeDtypeStruct((m, d), F32),
        compiler_params=pltpu.CompilerParams(dimension_semantics=("arbitrary",),
                                             vmem_limit_bytes=VMEM_LIMIT_V7X),
        name="out_mlp",
    )(mixed, x, w_out, g_mlp, w1, w2, g_final)


def _even_prompt_kernel(x_ref, g_ref, win_ref, wpool_ref, pscale_ref, lbl_ref, gain_ref,
                        mixed_ref, pstate_ref, hstate_ref, proj_ref, ext_ref, st_ref, k_ref, lsplit_ref,
                        bcum_ref, dec_ref, v_ref, qd_ref, kd_ref, ke_ref, qb_ref, kb_ref, qr_ref, kc_ref,
                        *, layer, tb):
    j = pl.program_id(1)
    last = pl.num_programs(1) - 1
    d_pool = wpool_ref.shape[0]
    n_heads = st_ref.shape[0]
    d_hgrn = n_heads * HEAD
    nc = tb // HGRN_CHUNK

    @pl.when(j == 0)
    def _():
        ext_ref[0:POOL_HIST, :] = jnp.zeros((POOL_HIST, d_pool), F32)
        st_ref[...] = jnp.zeros(st_ref.shape, F32)

    h = _rms(x_ref[...], g_ref[...]).astype(BF16)
    proj_ref[...] = jnp.dot(h, win_ref[...], preferred_element_type=F32)

    u = proj_ref[:, 0:d_pool]
    ext_ref[POOL_HIST:POOL_HIST + tb, :] = u
    acc = u
    snaps = {}
    for s in range(1, POOL_WINDOWS[-1]):
        acc = acc + ext_ref[pl.ds(POOL_HIST - s, tb), :]
        if s + 1 in POOL_WINDOWS:
            snaps[s + 1] = acc
    ssum, win = _pool_select(snaps, u.shape)
    pos = j * tb + lax.broadcasted_iota(jnp.int32, u.shape, 0)
    cnt = jnp.minimum(win, pos + 1).astype(F32)
    diff = (ssum / cnt - u).astype(BF16)
    pool_out = jnp.dot(diff, wpool_ref[...], preferred_element_type=F32) * pscale_ref[...]
    mixed_ref[:, 0:d_pool] = pool_out.astype(BF16)

    @pl.when(j == last)
    def _():
        pstate_ref[...] = ext_ref[pl.ds(POOL_HIST + tb - POOL_BUF, POOL_BUF), :]

    ext_ref[0:POOL_HIST, :] = ext_ref[tb:tb + POOL_HIST, :]

    lb = _lower_bound(lbl_ref[...], layer)
    o_q, o_f, o_i, o_g = d_pool, d_pool + d_hgrn, d_pool + 2 * d_hgrn, d_pool + 3 * d_hgrn
    for c in range(nc):
        rows = slice(c * HGRN_CHUNK, (c + 1) * HGRN_CHUNK)
        fz = proj_ref[rows, o_f:o_f + d_hgrn]
        a = jnp.exp(-jnp.abs(fz))
        r = 1.0 / (1.0 + a)
        ar = a * r
        sig = jnp.where(fz >= 0, r, ar)
        sig_neg = jnp.where(fz >= 0, ar, r)
        logf = jnp.log(lb + (1.0 - lb) * sig)
        k_ref[rows, :] = (1.0 - lb) * sig_neg
        l_hi = logf.astype(BF16)
        r1 = logf - l_hi.astype(F32)
        l_mid = r1.astype(BF16)
        lsplit_ref[rows, 0:d_hgrn] = l_hi
        lsplit_ref[rows, d_hgrn:2 * d_hgrn] = l_mid
        lsplit_ref[rows, 2 * d_hgrn:3 * d_hgrn] = (r1 - l_mid.astype(F32)).astype(BF16)
        iz = proj_ref[rows, o_i:o_i + d_hgrn]
        v_ref[rows, :] = (iz * _sigmoid(iz)).astype(BF16)

    ri = lax.broadcasted_iota(jnp.int32, (tb, tb), 0)
    ci = lax.broadcasted_iota(jnp.int32, (tb, tb), 1)
    tri = ((ri // HGRN_CHUNK == ci // HGRN_CHUNK) & (ci <= ri)).astype(BF16)
    cs = jnp.dot(tri, lsplit_ref[...], preferred_element_type=F32)
    bcum_ref[...] = cs[:, 0:d_hgrn] + cs[:, d_hgrn:2 * d_hgrn] + cs[:, 2 * d_hgrn:3 * d_hgrn]

    nsub = HGRN_BLOCK // HGRN_CHUNK
    half = nsub // 2
    for b in range(tb // HGRN_BLOCK):
        blast = [bcum_ref[(b * nsub + i + 1) * HGRN_CHUNK - 1:(b * nsub + i + 1) * HGRN_CHUNK, :]
                 for i in range(nsub)]
        pre = [jnp.zeros_like(blast[0])]
        for i in range(nsub):
            pre.append(pre[-1] + blast[i])
        dec_ref[b:b + 1, :] = jnp.exp(pre[nsub])
        for i in range(nsub):
            c = b * nsub + i
            rows = slice(c * HGRN_CHUNK, (c + 1) * HGRN_CHUNK)
            bc = bcum_ref[rows, :]
            qd = proj_ref[rows, o_q:o_q + d_hgrn] * jnp.exp(bc)
            kd = k_ref[rows, :] * jnp.exp(-bc)
            ke = kd * jnp.exp(blast[i])
            qd_ref[rows, :] = qd.astype(BF16)
            kd_ref[rows, :] = kd.astype(BF16)
            ke_ref[rows, :] = ke.astype(BF16)
            qb_ref[rows, :] = (qd * jnp.exp(pre[i])).astype(BF16)
            kb_ref[rows, :] = (ke * jnp.exp(pre[nsub] - pre[i + 1])).astype(BF16)
            if i >= half:
                qr_ref[rows, :] = (qd * jnp.exp(pre[i] - pre[half])).astype(BF16)
                kc_ref[rows, :] = jnp.zeros((HGRN_CHUNK, d_hgrn), BF16)
            else:
                kc_ref[rows, :] = (ke * jnp.exp(pre[half] - pre[i + 1])).astype(BF16)

    rb = lax.broadcasted_iota(jnp.int32, (HGRN_BLOCK, HGRN_BLOCK), 0)
    cb = lax.broadcasted_iota(jnp.int32, (HGRN_BLOCK, HGRN_BLOCK), 1)
    rsub, csub = rb // HGRN_CHUNK, cb // HGRN_CHUNK
    m_diag = (rsub == csub) & (cb <= rb)
    m_adj = (csub == rsub - 1) & (rsub != half)
    eye = rb == cb
    hb = HGRN_BLOCK // 2

    for hd in range(n_heads):
        sl = slice(hd * HEAD, (hd + 1) * HEAD)
        st = st_ref[hd]
        for b in range(tb // HGRN_BLOCK):
            rows = slice(b * HGRN_BLOCK, (b + 1) * HGRN_BLOCK)
            far = slice(b * HGRN_BLOCK + hb, (b + 1) * HGRN_BLOCK)
            kk = jnp.concatenate([kd_ref[rows, sl], ke_ref[rows, sl]], axis=0)
            g12 = lax.dot_general(qd_ref[rows, sl], kk, (((1,), (1,)), ((), ())), preferred_element_type=F32)
            g3 = lax.dot_general(qr_ref[far, sl], kc_ref[rows, sl], (((1,), (1,)), ((), ())),
                                 preferred_element_type=F32)
            scores = jnp.where(m_diag, g12[:, 0:HGRN_BLOCK], jnp.where(m_adj, g12[:, HGRN_BLOCK:], 0.0))
            scores = jnp.concatenate([scores[0:hb], scores[hb:] + g3], axis=0).astype(BF16)
            v_blk = v_ref[rows, sl]
            lhs = jnp.concatenate([scores, qb_ref[rows, sl]], axis=1)
            rhs = jnp.concatenate([v_blk, st.astype(BF16)], axis=0)
            o = jnp.dot(lhs, rhs, preferred_element_type=F32)
            d_col = jnp.sum(jnp.where(eye, dec_ref[b:b + 1, sl], 0.0), axis=1, keepdims=True)
            st = st * d_col + lax.dot_general(kb_ref[rows, sl], v_blk, (((0,), (0,)), ((), ())),
                                              preferred_element_type=F32)
            o = o * lax.rsqrt(jnp.mean(o * o, axis=-1, keepdims=True) + EPS)
            gz = proj_ref[rows, o_g + hd * HEAD:o_g + (hd + 1) * HEAD]
            o = o * gain_ref[:, sl] * _sigmoid(gz)
            mixed_ref[rows, d_pool + hd * HEAD:d_pool + (hd + 1) * HEAD] = o.astype(BF16)
        st_ref[hd] = st

    @pl.when(j == last)
    def _():
        hstate_ref[...] = st_ref[...]


def even_mix_prompt(x, g, w_in, wpool_bd, pscale, lbl, gain, *, layer, tb):
    b, t, d = x.shape
    n_in = w_in.shape[1]
    d_pool = wpool_bd.shape[0]
    d_hgrn = gain.shape[1]
    n_heads = d_hgrn // HEAD
    kern = functools.partial(_even_prompt_kernel, layer=layer, tb=tb)
    return pl.pallas_call(
        kern,
        grid=(b, t // tb),
        in_specs=[pl.BlockSpec((None, tb, d), lambda i, j: (i, j, 0)),
                  _resident((1, d)), _resident((d, n_in)), _resident((d_pool, d_pool)),
                  _resident((1, d_pool)), _resident(lbl.shape), _resident((1, d_hgrn))],
        out_specs=[pl.BlockSpec((None, tb, d), lambda i, j: (i, j, 0)),
                   pl.BlockSpec((None, POOL_BUF, d_pool), lambda i, j: (i, 0, 0)),
                   pl.BlockSpec((None, n_heads, HEAD, HEAD), lambda i, j: (i, 0, 0, 0))],
        out_shape=[jax.ShapeDtypeStruct((b, t, d), BF16),
                   jax.ShapeDtypeStruct((b, POOL_BUF, d_pool), F32),
                   jax.ShapeDtypeStruct((b, n_heads, HEAD, HEAD), F32)],
        scratch_shapes=[pltpu.VMEM((tb, n_in), F32),
                        pltpu.VMEM((POOL_HIST + tb, d_pool), F32),
                        pltpu.VMEM((n_heads, HEAD, HEAD), F32),
                        pltpu.VMEM((tb, d_hgrn), F32),
                        pltpu.VMEM((tb, 3 * d_hgrn), BF16),
                        pltpu.VMEM((tb, d_hgrn), F32),
                        pltpu.VMEM((tb // HGRN_BLOCK, d_hgrn), F32)]
                       + [pltpu.VMEM((tb, d_hgrn), BF16)] * 8,
        compiler_params=pltpu.CompilerParams(dimension_semantics=("arbitrary", "arbitrary"),
                                             vmem_limit_bytes=VMEM_LIMIT_V7X),
        name="even_mix_prompt",
    )(x, g, w_in, wpool_bd, pscale, lbl, gain)


def _even_sample_kernel(proj_ref, pool_ref, hst_ref, wpool_ref, pscale_ref, lbl_ref, gain_ref,
                        mixed_ref, npool_ref, nhst_ref, f_ref, k_ref, q_ref, v_ref, o_ref, *, layer, pos0):
    bb = proj_ref.shape[0]
    d_pool = wpool_ref.shape[0]
    n_heads = hst_ref.shape[1]
    d_hgrn = n_heads * HEAD

    u = proj_ref[:, 0:d_pool]
    acc = u
    snaps = {}
    for s in range(1, POOL_WINDOWS[-1]):
        r0 = (POOL_BUF - s) * d_pool
        acc = acc + pool_ref[:, r0:r0 + d_pool]
        if s + 1 in POOL_WINDOWS:
            snaps[s + 1] = acc
    ssum, win = _pool_select(snaps, u.shape)
    cnt = jnp.minimum(win, pos0 + 1).astype(F32)
    diff = (ssum / cnt - u).astype(BF16)
    pool_out = jnp.dot(diff, wpool_ref[...], preferred_element_type=F32) * pscale_ref[...]
    mixed_ref[:, 0:d_pool] = pool_out.astype(BF16)
    npool_ref[:, 0:(POOL_BUF - 1) * d_pool] = pool_ref[:, d_pool:POOL_BUF * d_pool]
    npool_ref[:, (POOL_BUF - 1) * d_pool:POOL_BUF * d_pool] = u

    lb = _lower_bound(lbl_ref[...], layer)
    fz = proj_ref[:, d_pool + d_hgrn:d_pool + 2 * d_hgrn]
    iz = proj_ref[:, d_pool + 2 * d_hgrn:d_pool + 3 * d_hgrn]
    f_ref[...] = lb + (1.0 - lb) * _sigmoid(fz)
    k_ref[...] = (1.0 - lb) * _sigmoid(-fz)
    q_ref[...] = proj_ref[:, d_pool:d_pool + d_hgrn]
    v_ref[...] = iz * _sigmoid(iz)
    eye = (lax.broadcasted_iota(jnp.int32, (HEAD, HEAD), 0)
           == lax.broadcasted_iota(jnp.int32, (HEAD, HEAD), 1))

    def to_column(row):
        return jnp.sum(jnp.where(eye, row, 0.0), axis=1, keepdims=True)

    for i in range(bb):
        for hd in range(n_heads):
            sl = slice(hd * HEAD, (hd + 1) * HEAD)
            f_col = to_column(f_ref[i:i + 1, sl])
            k_col = to_column(k_ref[i:i + 1, sl])
            q_col = to_column(q_ref[i:i + 1, sl])
            s_new = f_col * hst_ref[i, hd] + k_col * v_ref[i:i + 1, sl]
            nhst_ref[i, hd] = s_new
            o_ref[i:i + 1, sl] = jnp.sum(q_col * s_new, axis=0, keepdims=True)

    gz = proj_ref[:, d_pool + 3 * d_hgrn:d_pool + 4 * d_hgrn]
    gate = gain_ref[...] * _sigmoid(gz)
    for hd in range(n_heads):
        sl = slice(hd * HEAD, (hd + 1) * HEAD)
        o = o_ref[:, sl]
        o = o * lax.rsqrt(jnp.mean(o * o, axis=-1, keepdims=True) + EPS)
        mixed_ref[:, d_pool + hd * HEAD:d_pool + (hd + 1) * HEAD] = (o * gate[:, sl]).astype(BF16)


def even_mix_sample(proj, pool_flat, hstate, wpool_bd, pscale, lbl, gain, *, layer, pos0, bb):
    b, n_in = proj.shape
    d_pool = wpool_bd.shape[0]
    d_hgrn = gain.shape[1]
    n_heads = d_hgrn // HEAD
    d = d_pool + d_hgrn
    kern = functools.partial(_even_sample_kernel, layer=layer, pos0=pos0)
    return pl.pallas_call(
        kern,
        grid=(b // bb,),
        in_specs=[pl.BlockSpec((bb, n_in), lambda i: (i, 0)),
                  pl.BlockSpec((bb, POOL_BUF * d_pool), lambda i: (i, 0)),
                  pl.BlockSpec((bb, n_heads, HEAD, HEAD), lambda i: (i, 0, 0, 0)),
                  _resident((d_pool, d_pool)), _resident((1, d_pool)), _resident(lbl.shape),
                  _resident((1, d_hgrn))],
        out_specs=[pl.BlockSpec((bb, d), lambda i: (i, 0)),
                   pl.BlockSpec((bb, POOL_BUF * d_pool), lambda i: (i, 0)),
                   pl.BlockSpec((bb, n_heads, HEAD, HEAD), lambda i: (i, 0, 0, 0))],
        out_shape=[jax.ShapeDtypeStruct((b, d), BF16),
                   jax.ShapeDtypeStruct((b, POOL_BUF * d_pool), F32),
                   jax.ShapeDtypeStruct((b, n_heads, HEAD, HEAD), F32)],
        scratch_shapes=[pltpu.VMEM((bb, d_hgrn), F32)] * 5,
        compiler_params=pltpu.CompilerParams(dimension_semantics=("arbitrary",),
                                             vmem_limit_bytes=VMEM_LIMIT_V7X),
        name="even_mix_sample",
    )(proj, pool_flat, hstate, wpool_bd, pscale, lbl, gain)


def _odd_prompt_kernel(x_ref, g_ref, win_ref, cw_ref, mixed_ref, cstate_ref, zext_ref, *, tb):
    j = pl.program_id(1)
    last = pl.num_programs(1) - 1
    dc = cw_ref.shape[1]

    @pl.when(j == 0)
    def _():
        zext_ref[0:CONV_HIST, :] = jnp.zeros((CONV_HIST, dc), F32)

    h = _rms(x_ref[...], g_ref[...]).astype(BF16)
    cg = jnp.dot(h, win_ref[:, dc:2 * dc], preferred_element_type=F32)
    hv = jnp.dot(h, win_ref[:, 2 * dc:3 * dc], preferred_element_type=F32)
    z = cg * hv
    zext_ref[CONV_HIST:CONV_HIST + tb, :] = z
    conv = cw_ref[CONV_WIDTH - 1:CONV_WIDTH, :] * z
    for s in range(1, CONV_WIDTH):
        conv = conv + cw_ref[CONV_WIDTH - 1 - s:CONV_WIDTH - s, :] * zext_ref[pl.ds(CONV_HIST - s, tb), :]
    bg = jnp.dot(h, win_ref[:, 0:dc], preferred_element_type=F32)
    mixed_ref[...] = (bg * conv).astype(BF16)

    @pl.when(j == last)
    def _():
        cstate_ref[...] = zext_ref[pl.ds(CONV_HIST + tb - CONV_BUF, CONV_BUF), :]

    zext_ref[0:CONV_HIST, :] = zext_ref[tb:tb + CONV_HIST, :]


def odd_mix_prompt(x, g, w_in, conv_w, *, tb):
    b, t, d = x.shape
    dc = conv_w.shape[1]
    kern = functools.partial(_odd_prompt_kernel, tb=tb)
    return pl.pallas_call(
        kern,
        grid=(b, t // tb),
        in_specs=[pl.BlockSpec((None, tb, d), lambda i, j: (i, j, 0)),
                  _resident((1, d)), _resident((d, 3 * dc)), _resident((CONV_WIDTH, dc))],
        out_specs=[pl.BlockSpec((None, tb, dc), lambda i, j: (i, j, 0)),
                   pl.BlockSpec((None, CONV_BUF, dc), lambda i, j: (i, 0, 0))],
        out_shape=[jax.ShapeDtypeStruct((b, t, dc), BF16),
                   jax.ShapeDtypeStruct((b, CONV_BUF, dc), F32)],
        scratch_shapes=[pltpu.VMEM((CONV_HIST + tb, dc), F32)],
        compiler_params=pltpu.CompilerParams(dimension_semantics=("arbitrary", "arbitrary"),
                                             vmem_limit_bytes=VMEM_LIMIT_V7X),
        name="odd_mix_prompt",
    )(x, g, w_in, conv_w)


def _odd_sample_kernel(x_ref, g_ref, win_ref, cw_ref, cst_ref, mixed_ref, ncst_ref):
    dc = cw_ref.shape[1]
    h = _rms(x_ref[...], g_ref[...]).astype(BF16)
    cg = jnp.dot(h, win_ref[:, dc:2 * dc], preferred_element_type=F32)
    hv = jnp.dot(h, win_ref[:, 2 * dc:3 * dc], preferred_element_type=F32)
    z = cg * hv
    conv = cw_ref[CONV_WIDTH - 1:CONV_WIDTH, :] * z
    for s in range(1, CONV_WIDTH):
        r0 = (CONV_BUF - s) * dc
        conv = conv + cw_ref[CONV_WIDTH - 1 - s:CONV_WIDTH - s, :] * cst_ref[:, r0:r0 + dc]
    bg = jnp.dot(h, win_ref[:, 0:dc], preferred_element_type=F32)
    mixed_ref[...] = (bg * conv).astype(BF16)
    ncst_ref[:, 0:(CONV_BUF - 1) * dc] = cst_ref[:, dc:CONV_BUF * dc]
    ncst_ref[:, (CONV_BUF - 1) * dc:CONV_BUF * dc] = z


def odd_mix_sample(x, g, w_in, conv_w, cstate_flat):
    b, d = x.shape
    dc = conv_w.shape[1]
    return pl.pallas_call(
        _odd_sample_kernel,
        grid=(1,),
        in_specs=[_resident((b, d)), _resident((1, d)), _resident((d, 3 * dc)),
                  _resident((CONV_WIDTH, dc)), _resident((b, CONV_BUF * dc))],
        out_specs=[pl.BlockSpec((b, dc), lambda i: (0, 0)),
                   pl.BlockSpec((b, CONV_BUF * dc), lambda i: (0, 0))],
        out_shape=[jax.ShapeDtypeStruct((b, dc), BF16),
                   jax.ShapeDtypeStruct((b, CONV_BUF * dc), F32)],
        compiler_params=pltpu.CompilerParams(dimension_semantics=("arbitrary",),
                                             vmem_limit_bytes=VMEM_LIMIT_V7X),
        name="odd_mix_sample",
    )(x, g, w_in, conv_w, cstate_flat)


def _block_diag(w):
    g, c, _ = w.shape
    rows = [jnp.pad(w[i], ((0, 0), (i * c, (g - 1 - i) * c))) for i in range(g)]
    return jnp.concatenate(rows, axis=0)


def _trunk(x, states, pos0, p, *, prompt):
    depth = p["norm_mix"].shape[0]
    if prompt:
        b, t, d = x.shape
        xf = x.reshape(b * t, d)
        bm = 512
    else:
        b, d = x.shape
        xf = x
        bm = b
    new_pool, new_hgrn, new_conv = [], [], []
    for l in range(depth):
        g_mix = p["norm_mix"][l][None]
        final = l == depth - 1
        if l % 2 == 0:
            e = l // 2
            if prompt:
                mixed, p_new, s_new = even_mix_prompt(
                    xf.reshape(b, t, d), g_mix, p["even_w_in"][e], p["pool_bd"][e], p["pool_scale"][e][None],
                    p["hgrn_lb_logits"], p["hgrn_gain"][e][None], layer=l, tb=256)
                mixed = mixed.reshape(b * t, d)
            else:
                pool_st, hgrn_st, _ = states
                proj = norm_proj(xf, g_mix, p["even_w_in"][e], bm=bm)
                mixed, p_new, s_new = even_mix_sample(
                    proj, pool_st[e].reshape(b, -1), hgrn_st[e], p["pool_bd"][e], p["pool_scale"][e][None],
                    p["hgrn_lb_logits"], p["hgrn_gain"][e][None], layer=l, pos0=pos0, bb=8)
                p_new = p_new.reshape(b, POOL_BUF, -1)
            new_pool.append(p_new)
            new_hgrn.append(s_new)
            w_out = p["even_w_out"][e]
        else:
            o = l // 2
            if prompt:
                mixed, c_new = odd_mix_prompt(xf.reshape(b, t, d), g_mix, p["odd_w_in"][o], p["conv_w"][o], tb=512)
                mixed = mixed.reshape(b * t, d)
            else:
                conv_st = states[2]
                mixed, c_new = odd_mix_sample(xf, g_mix, p["odd_w_in"][o], p["conv_w"][o],
                                              conv_st[o].reshape(b, -1))
                c_new = c_new.reshape(b, CONV_BUF, -1)
            new_conv.append(c_new)
            w_out = p["odd_w_out"][o]
        xf = out_mlp(mixed, xf, w_out, p["norm_mlp"][l][None], p["ff_w1"][l], p["ff_w2"][l],
                     p["norm_final"][None], final_norm=final, bm=bm)
    y = xf.reshape(x.shape) if prompt else xf
    return y, jnp.stack(new_pool), jnp.stack(new_hgrn), jnp.stack(new_conv)


def kernel(x_prompt, x_sample, state_pool, state_hgrn, state_conv, norm_mix, norm_mlp, norm_final, even_w_in, pool_w, pool_scale, hgrn_lb_logits, hgrn_gain, even_w_out, odd_w_in, conv_w, odd_w_out, ff_w1, ff_w2):
    p = dict(norm_mix=norm_mix, norm_mlp=norm_mlp, norm_final=norm_final,
             even_w_in=even_w_in.astype(BF16), pool_bd=[_block_diag(w).astype(BF16) for w in pool_w],
             pool_scale=pool_scale, hgrn_lb_logits=hgrn_lb_logits, hgrn_gain=hgrn_gain,
             even_w_out=even_w_out.astype(BF16), odd_w_in=odd_w_in.astype(BF16), conv_w=conv_w,
             odd_w_out=odd_w_out.astype(BF16), ff_w1=ff_w1.astype(BF16), ff_w2=ff_w2.astype(BF16))
    y_p, pool_p, hgrn_p, conv_p = _trunk(x_prompt, None, 0, p, prompt=True)
    db, ds, d = x_sample.shape
    y_s, pool_s, hgrn_s, conv_s = _trunk(x_sample.reshape(db * ds, d), (state_pool, state_hgrn, state_conv),
                                         PAST_LEN, p, prompt=False)
    return (y_p, y_s.reshape(db, ds, d), pool_p, hgrn_p, conv_p, pool_s, hgrn_s, conv_s)
```

```python
import functools

import jax
import jax.numpy as jnp
from jax import lax
from jax.experimental import pallas as pl
from jax.experimental.pallas import tpu as pltpu

F32 = jnp.float32
BF16 = jnp.bfloat16

EPS = 1e-6
PAST_LEN = 16384
POOL_WINDOWS = (2, 4, 8, 16)
POOL_GROUP_DIM = 64
POOL_BUF = max(POOL_WINDOWS) - 1
POOL_HIST = 16
HGRN_CHUNK = 32
HGRN_BLOCK = 128
HEAD = 128
CONV_WIDTH = 3
CONV_BUF = CONV_WIDTH - 1
CONV_HIST = 8

VMEM_LIMIT_V7X = 56 * 1024 * 1024
MXU_COLS_V7X = 256


def _resident(shape):
    nd = len(shape)
    return pl.BlockSpec(shape, lambda *_: (0,) * nd, pipeline_mode=pl.Buffered(1))


def _rms(x, g):
    return x * lax.rsqrt(jnp.mean(x * x, axis=-1, keepdims=True) + EPS) * g


def _sigmoid(x):
    return 1.0 / (1.0 + jnp.exp(-x))


def _lower_bound(lbl, layer):
    e = jnp.exp(lbl - jnp.max(lbl, axis=0, keepdims=True))
    p = e / jnp.sum(e, axis=0, keepdims=True)
    return jnp.sum(p[0:layer + 1], axis=0, keepdims=True)


def _pool_select(snaps, u_shape):
    grp = lax.broadcasted_iota(jnp.int32, u_shape, 1) // POOL_GROUP_DIM
    s = snaps[POOL_WINDOWS[-1]]
    for g in range(len(POOL_WINDOWS) - 2, -1, -1):
        s = jnp.where(grp == g, snaps[POOL_WINDOWS[g]], s)
    win = jnp.left_shift(2, grp)
    return s, win


def _norm_proj_kernel(x_ref, g_ref, w_ref, o_ref):
    h = _rms(x_ref[...], g_ref[...]).astype(BF16)
    o_ref[...] = jnp.dot(h, w_ref[...], preferred_element_type=F32)


def norm_proj(x, g, w, *, bm):
    m, d = x.shape
    n = w.shape[1]
    return pl.pallas_call(
        _norm_proj_kernel,
        grid=(m // bm,),
        in_specs=[pl.BlockSpec((bm, d), lambda i: (i, 0)), _resident((1, d)), _resident((d, n))],
        out_specs=pl.BlockSpec((bm, n), lambda i: (i, 0)),
        out_shape=jax.ShapeDtypeStruct((m, n), F32),
        compiler_params=pltpu.CompilerParams(dimension_semantics=("arbitrary",),
                                             vmem_limit_bytes=VMEM_LIMIT_V7X),
        name="norm_proj",
    )(x, g, w)


def _out_mlp_kernel(m_ref, x_ref, wout_ref, g_ref, w1_ref, w2_ref, gf_ref, o_ref, *, final_norm, ff_chunk):
    y = x_ref[...] + jnp.dot(m_ref[...], wout_ref[...], preferred_element_type=F32)
    h = _rms(y, g_ref[...]).astype(BF16)
    acc = y
    d_ff = w1_ref.shape[1]
    for c in range(d_ff // ff_chunk):
        a = jnp.dot(h, w1_ref[:, c * ff_chunk:(c + 1) * ff_chunk], preferred_element_type=F32)
        a = jnp.square(jnp.maximum(a, 0.0)).astype(BF16)
        acc = acc + jnp.dot(a, w2_ref[c * ff_chunk:(c + 1) * ff_chunk, :], preferred_element_type=F32)
    if final_norm:
        acc = _rms(acc, gf_ref[...])
    o_ref[...] = acc


def out_mlp(mixed, x, w_out, g_mlp, w1, w2, g_final, *, final_norm, bm):
    m, d = x.shape
    d_ff = w1.shape[1]
    kern = functools.partial(_out_mlp_kernel, final_norm=final_norm, ff_chunk=1024)
    return pl.pallas_call(
        kern,
        grid=(m // bm,),
        in_specs=[pl.BlockSpec((bm, d), lambda i: (i, 0)), pl.BlockSpec((bm, d), lambda i: (i, 0)),
                  _resident((d, d)), _resident((1, d)), _resident((d, d_ff)), _resident((d_ff, d)),
                  _resident((1, d))],
        out_specs=pl.BlockSpec((bm, d), lambda i: (i, 0)),
        out_shape=jax.ShapeDtypeStruct((m, d), F32),
        compiler_params=pltpu.CompilerParams(dimension_semantics=("arbitrary",),
                                             vmem_limit_bytes=VMEM_LIMIT_V7X),
        name="out_mlp",
    )(mixed, x, w_out, g_mlp, w1, w2, g_final)


def _even_prompt_kernel(xn_ref, x0_ref, g_ref, win_ref, wpool_ref, pscale_ref, lbl_ref, gain_ref,
                        mixed_ref, pstate_ref, hstate_ref,
                        pu_ref, pq_ref, pf_ref, pi_ref, pg0_ref, pg1_ref, pg2_ref, hn_ref,
                        ext_ref, st_ref, k_ref, lsplit_ref,
                        bcum_ref, dec_ref, v_ref, qd_ref, kd_ref, ke_ref, qb_ref, kb_ref, qr_ref, kc_ref,
                        *, layer, tb, tiles_per_seq):
    step = pl.program_id(0)
    d_pool = wpool_ref.shape[0]
    n_heads = st_ref.shape[0]
    d_hgrn = n_heads * HEAD
    nc = tb // HGRN_CHUNK
    j = step % tiles_per_seq
    last = tiles_per_seq - 1
    o_q, o_f, o_i, o_g = d_pool, d_pool + d_hgrn, d_pool + 2 * d_hgrn, d_pool + 3 * d_hgrn
    pg_refs = (pg0_ref, pg1_ref, pg2_ref)
    gw = d_hgrn // len(pg_refs)
    sections = [(pu_ref, 0), (pq_ref, o_q), (pf_ref, o_f), (pi_ref, o_i)] + \
               [(r_, o_g + n * gw) for n, r_ in enumerate(pg_refs)]

    def project(h_ref, dst_ref, col0):
        for c0 in range(0, dst_ref.shape[1], MXU_COLS_V7X):
            c1 = min(c0 + MXU_COLS_V7X, dst_ref.shape[1])
            dst_ref[:, c0:c1] = jnp.dot(h_ref[...], win_ref[:, col0 + c0:col0 + c1], preferred_element_type=F32)

    @pl.when(step == 0)
    def _():
        hn_ref[...] = _rms(x0_ref[...], g_ref[...]).astype(BF16)
        for dst_ref, col0 in sections:
            project(hn_ref, dst_ref, col0)

    @pl.when(j == 0)
    def _():
        ext_ref[0:POOL_HIST, :] = jnp.zeros((POOL_HIST, d_pool), F32)
        st_ref[...] = jnp.zeros(st_ref.shape, F32)

    hn_ref[...] = _rms(xn_ref[...], g_ref[...]).astype(BF16)

    u = pu_ref[...]
    ext_ref[POOL_HIST:POOL_HIST + tb, :] = u
    acc = u
    snaps = {}
    for sh in range(1, POOL_WINDOWS[-1]):
        acc = acc + ext_ref[pl.ds(POOL_HIST - sh, tb), :]
        if sh + 1 in POOL_WINDOWS:
            snaps[sh + 1] = acc
    ssum, win = _pool_select(snaps, u.shape)
    pos = j * tb + lax.broadcasted_iota(jnp.int32, u.shape, 0)
    cnt = jnp.minimum(win, pos + 1).astype(F32)
    diff = (ssum / cnt - u).astype(BF16)
    pool_out = jnp.dot(diff, wpool_ref[...], preferred_element_type=F32) * pscale_ref[...]
    mixed_ref[:, 0:d_pool] = pool_out.astype(BF16)

    @pl.when(j == last)
    def _():
        pstate_ref[...] = ext_ref[pl.ds(POOL_HIST + tb - POOL_BUF, POOL_BUF), :]

    ext_ref[0:POOL_HIST, :] = ext_ref[tb:tb + POOL_HIST, :]
    project(hn_ref, pu_ref, 0)

    lb = _lower_bound(lbl_ref[...], layer)
    for c in range(nc):
        rows = slice(c * HGRN_CHUNK, (c + 1) * HGRN_CHUNK)
        fz = pf_ref[rows, :]
        a = jnp.exp(-jnp.abs(fz))
        r = 1.0 / (1.0 + a)
        ar = a * r
        sig = jnp.where(fz >= 0, r, ar)
        sig_neg = jnp.where(fz >= 0, ar, r)
        logf = jnp.log(lb + (1.0 - lb) * sig)
        k_ref[rows, :] = (1.0 - lb) * sig_neg
        l_hi = logf.astype(BF16)
        r1 = logf - l_hi.astype(F32)
        l_mid = r1.astype(BF16)
        lsplit_ref[rows, 0:d_hgrn] = l_hi
        lsplit_ref[rows, d_hgrn:2 * d_hgrn] = l_mid
        lsplit_ref[rows, 2 * d_hgrn:3 * d_hgrn] = (r1 - l_mid.astype(F32)).astype(BF16)
        iz = pi_ref[rows, :]
        v_ref[rows, :] = (iz * _sigmoid(iz)).astype(BF16)
    project(hn_ref, pf_ref, o_f)
    project(hn_ref, pi_ref, o_i)

    ri = lax.broadcasted_iota(jnp.int32, (tb, tb), 0)
    ci = lax.broadcasted_iota(jnp.int32, (tb, tb), 1)
    tri = ((ri // HGRN_CHUNK == ci // HGRN_CHUNK) & (ci <= ri)).astype(BF16)
    cs = jnp.dot(tri, lsplit_ref[...], preferred_element_type=F32)
    bcum_ref[...] = cs[:, 0:d_hgrn] + cs[:, d_hgrn:2 * d_hgrn] + cs[:, 2 * d_hgrn:3 * d_hgrn]

    nsub = HGRN_BLOCK // HGRN_CHUNK
    half = nsub // 2
    for b in range(tb // HGRN_BLOCK):
        blast = [bcum_ref[(b * nsub + i + 1) * HGRN_CHUNK - 1:(b * nsub + i + 1) * HGRN_CHUNK, :]
                 for i in range(nsub)]
        pre = [jnp.zeros_like(blast[0])]
        for i in range(nsub):
            pre.append(pre[-1] + blast[i])
        dec_ref[b:b + 1, :] = jnp.exp(pre[nsub])
        for i in range(nsub):
            c = b * nsub + i
            rows = slice(c * HGRN_CHUNK, (c + 1) * HGRN_CHUNK)
            bc = bcum_ref[rows, :]
            qd = pq_ref[rows, :] * jnp.exp(bc)
            kd = k_ref[rows, :] * jnp.exp(-bc)
            ke = kd * jnp.exp(blast[i])
            qd_ref[rows, :] = qd.astype(BF16)
            kd_ref[rows, :] = kd.astype(BF16)
            ke_ref[rows, :] = ke.astype(BF16)
            qb_ref[rows, :] = (qd * jnp.exp(pre[i])).astype(BF16)
            kb_ref[rows, :] = (ke * jnp.exp(pre[nsub] - pre[i + 1])).astype(BF16)
            if i >= half:
                qr_ref[rows, :] = (qd * jnp.exp(pre[i] - pre[half])).astype(BF16)
                kc_ref[rows, :] = jnp.zeros((HGRN_CHUNK, d_hgrn), BF16)
            else:
                kc_ref[rows, :] = (ke * jnp.exp(pre[half] - pre[i + 1])).astype(BF16)

    project(hn_ref, pq_ref, o_q)

    rb = lax.broadcasted_iota(jnp.int32, (HGRN_BLOCK, HGRN_BLOCK), 0)
    cb = lax.broadcasted_iota(jnp.int32, (HGRN_BLOCK, HGRN_BLOCK), 1)
    rsub, csub = rb // HGRN_CHUNK, cb // HGRN_CHUNK
    m_diag = (rsub == csub) & (cb <= rb)
    m_adj = (csub == rsub - 1) & (rsub != half)
    eye = rb == cb
    hb = HGRN_BLOCK // 2

    for hd in range(n_heads):
        sl = slice(hd * HEAD, (hd + 1) * HEAD)
        st = st_ref[hd]
        for b in range(tb // HGRN_BLOCK):
            rows = slice(b * HGRN_BLOCK, (b + 1) * HGRN_BLOCK)
            far = slice(b * HGRN_BLOCK + hb, (b + 1) * HGRN_BLOCK)
            kk = jnp.concatenate([kd_ref[rows, sl], ke_ref[rows, sl]], axis=0)
            g12 = lax.dot_general(qd_ref[rows, sl], kk, (((1,), (1,)), ((), ())), preferred_element_type=F32)
            g3 = lax.dot_general(qr_ref[far, sl], kc_ref[rows, sl], (((1,), (1,)), ((), ())),
                                 preferred_element_type=F32)
            scores = jnp.where(m_diag, g12[:, 0:HGRN_BLOCK], jnp.where(m_adj, g12[:, HGRN_BLOCK:], 0.0))
            scores = jnp.concatenate([scores[0:hb], scores[hb:] + g3], axis=0).astype(BF16)
            v_blk = v_ref[rows, sl]
            lhs = jnp.concatenate([scores, qb_ref[rows, sl]], axis=1)
            rhs = jnp.concatenate([v_blk, st.astype(BF16)], axis=0)
            o = jnp.dot(lhs, rhs, preferred_element_type=F32)
            d_col = jnp.sum(jnp.where(eye, dec_ref[b:b + 1, sl], 0.0), axis=1, keepdims=True)
            st = st * d_col + lax.dot_general(kb_ref[rows, sl], v_blk, (((0,), (0,)), ((), ())),
                                              preferred_element_type=F32)
            o = o * lax.rsqrt(jnp.mean(o * o, axis=-1, keepdims=True) + EPS)
            g_lo = hd * HEAD % gw
            gz = pg_refs[hd * HEAD // gw][rows, g_lo:g_lo + HEAD]
            o = o * gain_ref[:, sl] * _sigmoid(gz)
            mixed_ref[rows, d_pool + hd * HEAD:d_pool + (hd + 1) * HEAD] = o.astype(BF16)
        st_ref[hd] = st
        if (hd + 1) * HEAD % gw == 0:
            n = hd * HEAD // gw
            project(hn_ref, pg_refs[n], o_g + n * gw)

    @pl.when(j == last)
    def _():
        hstate_ref[...] = st_ref[...]


def even_mix_prompt(x, g, w_in, wpool_bd, pscale, lbl, gain, *, layer, tb):
    b, t, d = x.shape
    n_in = w_in.shape[1]
    d_pool = wpool_bd.shape[0]
    d_hgrn = gain.shape[1]
    n_heads = d_hgrn // HEAD
    tps = t // tb
    n_tiles = b * tps
    kern = functools.partial(_even_prompt_kernel, layer=layer, tb=tb, tiles_per_seq=tps)

    def next_tile(s):
        tile = jnp.minimum(s + 1, n_tiles - 1)
        return tile // tps, tile % tps

    gate_sections = 3
    assert d_hgrn % gate_sections == 0 and (d_hgrn // gate_sections) % HEAD == 0
    return pl.pallas_call(
        kern,
        grid=(n_tiles,),
        in_specs=[pl.BlockSpec((None, tb, d), lambda s: (*next_tile(s), 0)),
                  pl.BlockSpec((None, tb, d), lambda s: (0, 0, 0)),
                  _resident((1, d)), _resident((d, n_in)), _resident((d_pool, d_pool)),
                  _resident((1, d_pool)), _resident(lbl.shape), _resident((1, d_hgrn))],
        out_specs=[pl.BlockSpec((None, tb, d), lambda s: (s // tps, s % tps, 0)),
                   pl.BlockSpec((None, POOL_BUF, d_pool), lambda s: (s // tps, 0, 0)),
                   pl.BlockSpec((None, n_heads, HEAD, HEAD), lambda s: (s // tps, 0, 0, 0))],
        out_shape=[jax.ShapeDtypeStruct((b, t, d), BF16),
                   jax.ShapeDtypeStruct((b, POOL_BUF, d_pool), F32),
                   jax.ShapeDtypeStruct((b, n_heads, HEAD, HEAD), F32)],
        scratch_shapes=[pltpu.VMEM((tb, d_pool), F32)]
                       + [pltpu.VMEM((tb, d_hgrn), F32)] * 3
                       + [pltpu.VMEM((tb, d_hgrn // gate_sections), F32)] * gate_sections
                       + [pltpu.VMEM((tb, d), BF16),
                        pltpu.VMEM((POOL_HIST + tb, d_pool), F32),
                        pltpu.VMEM((n_heads, HEAD, HEAD), F32),
                        pltpu.VMEM((tb, d_hgrn), F32),
                        pltpu.VMEM((tb, 3 * d_hgrn), BF16),
                        pltpu.VMEM((tb, d_hgrn), F32),
                        pltpu.VMEM((tb // HGRN_BLOCK, d_hgrn), F32)]
                       + [pltpu.VMEM((tb, d_hgrn), BF16)] * 8,
        compiler_params=pltpu.CompilerParams(dimension_semantics=("arbitrary",),
                                             vmem_limit_bytes=VMEM_LIMIT_V7X),
        name="even_mix_prompt",
    )(x, x, g, w_in, wpool_bd, pscale, lbl, gain)


def _even_sample_kernel(proj_ref, pool_ref, hst_ref, wpool_ref, pscale_ref, lbl_ref, gain_ref,
                        mixed_ref, npool_ref, nhst_ref, f_ref, k_ref, q_ref, v_ref, o_ref, *, layer, pos0):
    bb = proj_ref.shape[0]
    d_pool = wpool_ref.shape[0]
    n_heads = hst_ref.shape[1]
    d_hgrn = n_heads * HEAD

    u = proj_ref[:, 0:d_pool]
    acc = u
    snaps = {}
    for s in range(1, POOL_WINDOWS[-1]):
        r0 = (POOL_BUF - s) * d_pool
        acc = acc + pool_ref[:, r0:r0 + d_pool]
        if s + 1 in POOL_WINDOWS:
            snaps[s + 1] = acc
    ssum, win = _pool_select(snaps, u.shape)
    cnt = jnp.minimum(win, pos0 + 1).astype(F32)
    diff = (ssum / cnt - u).astype(BF16)
    pool_out = jnp.dot(diff, wpool_ref[...], preferred_element_type=F32) * pscale_ref[...]
    mixed_ref[:, 0:d_pool] = pool_out.astype(BF16)
    npool_ref[:, 0:(POOL_BUF - 1) * d_pool] = pool_ref[:, d_pool:POOL_BUF * d_pool]
    npool_ref[:, (POOL_BUF - 1) * d_pool:POOL_BUF * d_pool] = u

    lb = _lower_bound(lbl_ref[...], layer)
    fz = proj_ref[:, d_pool + d_hgrn:d_pool + 2 * d_hgrn]
    iz = proj_ref[:, d_pool + 2 * d_hgrn:d_pool + 3 * d_hgrn]
    f_ref[...] = lb + (1.0 - lb) * _sigmoid(fz)
    k_ref[...] = (1.0 - lb) * _sigmoid(-fz)
    q_ref[...] = proj_ref[:, d_pool:d_pool + d_hgrn]
    v_ref[...] = iz * _sigmoid(iz)
    eye = (lax.broadcasted_iota(jnp.int32, (HEAD, HEAD), 0)
           == lax.broadcasted_iota(jnp.int32, (HEAD, HEAD), 1))

    def to_column(row):
        return jnp.sum(jnp.where(eye, row, 0.0), axis=1, keepdims=True)

    for i in range(bb):
        for hd in range(n_heads):
            sl = slice(hd * HEAD, (hd + 1) * HEAD)
            f_col = to_column(f_ref[i:i + 1, sl])
            k_col = to_column(k_ref[i:i + 1, sl])
            q_col = to_column(q_ref[i:i + 1, sl])
            s_new = f_col * hst_ref[i, hd] + k_col * v_ref[i:i + 1, sl]
            nhst_ref[i, hd] = s_new
            o_ref[i:i + 1, sl] = jnp.sum(q_col * s_new, axis=0, keepdims=True)

    gz = proj_ref[:, d_pool + 3 * d_hgrn:d_pool + 4 * d_hgrn]
    gate = gain_ref[...] * _sigmoid(gz)
    for hd in range(n_heads):
        sl = slice(hd * HEAD, (hd + 1) * HEAD)
        o = o_ref[:, sl]
        o = o * lax.rsqrt(jnp.mean(o * o, axis=-1, keepdims=True) + EPS)
        mixed_ref[:, d_pool + hd * HEAD:d_pool + (hd + 1) * HEAD] = (o * gate[:, sl]).astype(BF16)


def even_mix_sample(proj, pool_flat, hstate, wpool_bd, pscale, lbl, gain, *, layer, pos0, bb):
    b, n_in = proj.shape
    d_pool = wpool_bd.shape[0]
    d_hgrn = gain.shape[1]
    n_heads = d_hgrn // HEAD
    d = d_pool + d_hgrn
    kern = functools.partial(_even_sample_kernel, layer=layer, pos0=pos0)
    return pl.pallas_call(
        kern,
        grid=(b // bb,),
        in_specs=[pl.BlockSpec((bb, n_in), lambda i: (i, 0)),
                  pl.BlockSpec((bb, POOL_BUF * d_pool), lambda i: (i, 0)),
                  pl.BlockSpec((bb, n_heads, HEAD, HEAD), lambda i: (i, 0, 0, 0)),
                  _resident((d_pool, d_pool)), _resident((1, d_pool)), _resident(lbl.shape),
                  _resident((1, d_hgrn))],
        out_specs=[pl.BlockSpec((bb, d), lambda i: (i, 0)),
                   pl.BlockSpec((bb, POOL_BUF * d_pool), lambda i: (i, 0)),
                   pl.BlockSpec((bb, n_heads, HEAD, HEAD), lambda i: (i, 0, 0, 0))],
        out_shape=[jax.ShapeDtypeStruct((b, d), BF16),
                   jax.ShapeDtypeStruct((b, POOL_BUF * d_pool), F32),
                   jax.ShapeDtypeStruct((b, n_heads, HEAD, HEAD), F32)],
        scratch_shapes=[pltpu.VMEM((bb, d_hgrn), F32)] * 5,
        compiler_params=pltpu.CompilerParams(dimension_semantics=("arbitrary",),
                                             vmem_limit_bytes=VMEM_LIMIT_V7X),
        name="even_mix_sample",
    )(proj, pool_flat, hstate, wpool_bd, pscale, lbl, gain)


def _odd_prompt_kernel(x_ref, g_ref, win_ref, cw_ref, mixed_ref, cstate_ref, zext_ref, *, tb):
    j = pl.program_id(1)
    last = pl.num_programs(1) - 1
    dc = cw_ref.shape[1]

    @pl.when(j == 0)
    def _():
        zext_ref[0:CONV_HIST, :] = jnp.zeros((CONV_HIST, dc), F32)

    h = _rms(x_ref[...], g_ref[...]).astype(BF16)
    cg = jnp.dot(h, win_ref[:, dc:2 * dc], preferred_element_type=F32)
    hv = jnp.dot(h, win_ref[:, 2 * dc:3 * dc], preferred_element_type=F32)
    z = cg * hv
    zext_ref[CONV_HIST:CONV_HIST + tb, :] = z
    conv = cw_ref[CONV_WIDTH - 1:CONV_WIDTH, :] * z
    for s in range(1, CONV_WIDTH):
        conv = conv + cw_ref[CONV_WIDTH - 1 - s:CONV_WIDTH - s, :] * zext_ref[pl.ds(CONV_HIST - s, tb), :]
    bg = jnp.dot(h, win_ref[:, 0:dc], preferred_element_type=F32)
    mixed_ref[...] = (bg * conv).astype(BF16)

    @pl.when(j == last)
    def _():
        cstate_ref[...] = zext_ref[pl.ds(CONV_HIST + tb - CONV_BUF, CONV_BUF), :]

    zext_ref[0:CONV_HIST, :] = zext_ref[tb:tb + CONV_HIST, :]


def odd_mix_prompt(x, g, w_in, conv_w, *, tb):
    b, t, d = x.shape
    dc = conv_w.shape[1]
    kern = functools.partial(_odd_prompt_kernel, tb=tb)
    return pl.pallas_call(
        kern,
        grid=(b, t // tb),
        in_specs=[pl.BlockSpec((None, tb, d), lambda i, j: (i, j, 0)),
                  _resident((1, d)), _resident((d, 3 * dc)), _resident((CONV_WIDTH, dc))],
        out_specs=[pl.BlockSpec((None, tb, dc), lambda i, j: (i, j, 0)),
                   pl.BlockSpec((None, CONV_BUF, dc), lambda i, j: (i, 0, 0))],
        out_shape=[jax.ShapeDtypeStruct((b, t, dc), BF16),
                   jax.ShapeDtypeStruct((b, CONV_BUF, dc), F32)],
        scratch_shapes=[pltpu.VMEM((CONV_HIST + tb, dc), F32)],
        compiler_params=pltpu.CompilerParams(dimension_semantics=("arbitrary", "arbitrary"),
                                             vmem_limit_bytes=VMEM_LIMIT_V7X),
        name="odd_mix_prompt",
    )(x, g, w_in, conv_w)


def _odd_sample_kernel(x_ref, g_ref, win_ref, cw_ref, cst_ref, mixed_ref, ncst_ref):
    dc = cw_ref.shape[1]
    h = _rms(x_ref[...], g_ref[...]).astype(BF16)
    cg = jnp.dot(h, win_ref[:, dc:2 * dc], preferred_element_type=F32)
    hv = jnp.dot(h, win_ref[:, 2 * dc:3 * dc], preferred_element_type=F32)
    z = cg * hv
    conv = cw_ref[CONV_WIDTH - 1:CONV_WIDTH, :] * z
    for s in range(1, CONV_WIDTH):
        r0 = (CONV_BUF - s) * dc
        conv = conv + cw_ref[CONV_WIDTH - 1 - s:CONV_WIDTH - s, :] * cst_ref[:, r0:r0 + dc]
    bg = jnp.dot(h, win_ref[:, 0:dc], preferred_element_type=F32)
    mixed_ref[...] = (bg * conv).astype(BF16)
    ncst_ref[:, 0:(CONV_BUF - 1) * dc] = cst_ref[:, dc:CONV_BUF * dc]
    ncst_ref[:, (CONV_BUF - 1) * dc:CONV_BUF * dc] = z


def odd_mix_sample(x, g, w_in, conv_w, cstate_flat):
    b, d = x.shape
    dc = conv_w.shape[1]
    return pl.pallas_call(
        _odd_sample_kernel,
        grid=(1,),
        in_specs=[_resident((b, d)), _resident((1, d)), _resident((d, 3 * dc)),
                  _resident((CONV_WIDTH, dc)), _resident((b, CONV_BUF * dc))],
        out_specs=[pl.BlockSpec((b, dc), lambda i: (0, 0)),
                   pl.BlockSpec((b, CONV_BUF * dc), lambda i: (0, 0))],
        out_shape=[jax.ShapeDtypeStruct((b, dc), BF16),
                   jax.ShapeDtypeStruct((b, CONV_BUF * dc), F32)],
        compiler_params=pltpu.CompilerParams(dimension_semantics=("arbitrary",),
                                             vmem_limit_bytes=VMEM_LIMIT_V7X),
        name="odd_mix_sample",
    )(x, g, w_in, conv_w, cstate_flat)


def _block_diag(w):
    g, c, _ = w.shape
    rows = [jnp.pad(w[i], ((0, 0), (i * c, (g - 1 - i) * c))) for i in range(g)]
    return jnp.concatenate(rows, axis=0)


def _trunk(x, states, pos0, p, *, prompt):
    depth = p["norm_mix"].shape[0]
    if prompt:
        b, t, d = x.shape
        xf = x.reshape(b * t, d)
        bm = 512
    else:
        b, d = x.shape
        xf = x
        bm = b
    new_pool, new_hgrn, new_conv = [], [], []
    for l in range(depth):
        g_mix = p["norm_mix"][l][None]
        final = l == depth - 1
        if l % 2 == 0:
            e = l // 2
            if prompt:
                mixed, p_new, s_new = even_mix_prompt(
                    xf.reshape(b, t, d), g_mix, p["even_w_in"][e], p["pool_bd"][e], p["pool_scale"][e][None],
                    p["hgrn_lb_logits"], p["hgrn_gain"][e][None], layer=l, tb=256)
                mixed = mixed.reshape(b * t, d)
            else:
                pool_st, hgrn_st, _ = states
                proj = norm_proj(xf, g_mix, p["even_w_in"][e], bm=bm)
                mixed, p_new, s_new = even_mix_sample(
                    proj, pool_st[e].reshape(b, -1), hgrn_st[e], p["pool_bd"][e], p["pool_scale"][e][None],
                    p["hgrn_lb_logits"], p["hgrn_gain"][e][None], layer=l, pos0=pos0, bb=8)
                p_new = p_new.reshape(b, POOL_BUF, -1)
            new_pool.append(p_new)
            new_hgrn.append(s_new)
            w_out = p["even_w_out"][e]
        else:
            o = l // 2
            if prompt:
                mixed, c_new = odd_mix_prompt(xf.reshape(b, t, d), g_mix, p["odd_w_in"][o], p["conv_w"][o], tb=512)
                mixed = mixed.reshape(b * t, d)
            else:
                conv_st = states[2]
                mixed, c_new = odd_mix_sample(xf, g_mix, p["odd_w_in"][o], p["conv_w"][o],
                                              conv_st[o].reshape(b, -1))
                c_new = c_new.reshape(b, CONV_BUF, -1)
            new_conv.append(c_new)
            w_out = p["odd_w_out"][o]
        xf = out_mlp(mixed, xf, w_out, p["norm_mlp"][l][None], p["ff_w1"][l], p["ff_w2"][l],
                     p["norm_final"][None], final_norm=final, bm=bm)
    y = xf.reshape(x.shape) if prompt else xf
    return y, jnp.stack(new_pool), jnp.stack(new_hgrn), jnp.stack(new_conv)


def kernel(x_prompt, x_sample, state_pool, state_hgrn, state_conv, norm_mix, norm_mlp, norm_final, even_w_in, pool_w, pool_scale, hgrn_lb_logits, hgrn_gain, even_w_out, odd_w_in, conv_w, odd_w_out, ff_w1, ff_w2):
    p = dict(norm_mix=norm_mix, norm_mlp=norm_mlp, norm_final=norm_final,
             even_w_in=even_w_in.astype(BF16), pool_bd=[_block_diag(w).astype(BF16) for w in pool_w],
             pool_scale=pool_scale, hgrn_lb_logits=hgrn_lb_logits, hgrn_gain=hgrn_gain,
             even_w_out=even_w_out.astype(BF16), odd_w_in=odd_w_in.astype(BF16), conv_w=conv_w,
             odd_w_out=odd_w_out.astype(BF16), ff_w1=ff_w1.astype(BF16), ff_w2=ff_w2.astype(BF16))
    y_p, pool_p, hgrn_p, conv_p = _trunk(x_prompt, None, 0, p, prompt=True)
    db, ds, d = x_sample.shape
    y_s, pool_s, hgrn_s, conv_s = _trunk(x_sample.reshape(db * ds, d), (state_pool, state_hgrn, state_conv),
                                         PAST_LEN, p, prompt=False)
    return (y_p, y_s.reshape(db, ds, d), pool_p, hgrn_p, conv_p, pool_s, hgrn_s, conv_s)
```

```python
import functools

import jax
import jax.numpy as jnp
from jax import lax
from jax.experimental import pallas as pl
from jax.experimental.pallas import tpu as pltpu

F32 = jnp.float32
BF16 = jnp.bfloat16

EPS = 1e-6
PAST_LEN = 16384
POOL_WINDOWS = (2, 4, 8, 16)
POOL_GROUP_DIM = 64
POOL_BUF = max(POOL_WINDOWS) - 1
POOL_HIST = 16
POOL_PAD = 8
LANES = 128
HGRN_CHUNK = 32
HGRN_BLOCK = 128
HEAD = 128
CONV_WIDTH = 3
CONV_BUF = CONV_WIDTH - 1
CONV_HIST = 8

VMEM_LIMIT_V7X = 56 * 1024 * 1024
MXU_COLS_V7X = 256


def _resident(shape):
    nd = len(shape)
    return pl.BlockSpec(shape, lambda *_: (0,) * nd, pipeline_mode=pl.Buffered(1))


def _rms(x, g):
    return x * lax.rsqrt(jnp.mean(x * x, axis=-1, keepdims=True) + EPS) * g


def _sigmoid(x):
    return 1.0 / (1.0 + jnp.exp(-x))


def _lower_bound(lbl, layer):
    e = jnp.exp(lbl - jnp.max(lbl, axis=0, keepdims=True))
    p = e / jnp.sum(e, axis=0, keepdims=True)
    return jnp.sum(p[0:layer + 1], axis=0, keepdims=True)


def _pool_select(snaps, u_shape):
    grp = lax.broadcasted_iota(jnp.int32, u_shape, 1) // POOL_GROUP_DIM
    s = snaps[POOL_WINDOWS[-1]]
    for g in range(len(POOL_WINDOWS) - 2, -1, -1):
        s = jnp.where(grp == g, snaps[POOL_WINDOWS[g]], s)
    win = jnp.left_shift(2, grp)
    return s, win


def _norm_proj_kernel(x_ref, g_ref, w_ref, o_ref):
    h = _rms(x_ref[...], g_ref[...]).astype(BF16)
    o_ref[...] = jnp.dot(h, w_ref[...], preferred_element_type=F32)


def norm_proj(x, g, w, *, bm):
    m, d = x.shape
    n = w.shape[1]
    return pl.pallas_call(
        _norm_proj_kernel,
        grid=(m // bm,),
        in_specs=[pl.BlockSpec((bm, d), lambda i: (i, 0)), _resident((1, d)), _resident((d, n))],
        out_specs=pl.BlockSpec((bm, n), lambda i: (i, 0)),
        out_shape=jax.ShapeDtypeStruct((m, n), F32),
        compiler_params=pltpu.CompilerParams(dimension_semantics=("arbitrary",),
                                             vmem_limit_bytes=VMEM_LIMIT_V7X),
        name="norm_proj",
    )(x, g, w)


def _out_mlp_kernel(m_ref, x_ref, wout_ref, g_ref, w1_ref, w2_ref, gf_ref, o_ref, *, final_norm, ff_chunk):
    y = x_ref[...] + jnp.dot(m_ref[...], wout_ref[...], preferred_element_type=F32)
    h = _rms(y, g_ref[...]).astype(BF16)
    acc = y
    d_ff = w1_ref.shape[1]
    for c in range(d_ff // ff_chunk):
        a = jnp.dot(h, w1_ref[:, c * ff_chunk:(c + 1) * ff_chunk], preferred_element_type=F32)
        a = jnp.square(jnp.maximum(a, 0.0)).astype(BF16)
        acc = acc + jnp.dot(a, w2_ref[c * ff_chunk:(c + 1) * ff_chunk, :], preferred_element_type=F32)
    if final_norm:
        acc = _rms(acc, gf_ref[...])
    o_ref[...] = acc


def out_mlp(mixed, x, w_out, g_mlp, w1, w2, g_final, *, final_norm, bm):
    m, d = x.shape
    d_ff = w1.shape[1]
    kern = functools.partial(_out_mlp_kernel, final_norm=final_norm, ff_chunk=1024)
    return pl.pallas_call(
        kern,
        grid=(m // bm,),
        in_specs=[pl.BlockSpec((bm, d), lambda i: (i, 0)), pl.BlockSpec((bm, d), lambda i: (i, 0)),
                  _resident((d, d)), _resident((1, d)), _resident((d, d_ff)), _resident((d_ff, d)),
                  _resident((1, d))],
        out_specs=pl.BlockSpec((bm, d), lambda i: (i, 0)),
        out_shape=jax.ShapeDtypeStruct((m, d), F32),
        compiler_params=pltpu.CompilerParams(dimension_semantics=("arbitrary",),
                                             vmem_limit_bytes=VMEM_LIMIT_V7X),
        name="out_mlp",
    )(mixed, x, w_out, g_mlp, w1, w2, g_final)


def _even_prompt_kernel(xn_ref, x0_ref, g_ref, win_ref, wpool_ref, pscale_ref, lbl_ref, gain_ref,
                        mixed_ref, pstate_ref, hstate_ref,
                        pu_ref, pq_ref, pf_ref, pi_ref, pg_ref, hn_ref, ext_ref, lvl_ref, st_ref, k_ref, lsplit_ref,
                        bcum_ref, dec_ref, gate_ref, v_ref, qd_ref, kd_ref, ke_ref, qb_ref, kb_ref, qr_ref, kc_ref,
                        *, layer, tb, tiles_per_seq):
    step = pl.program_id(0)
    d_pool = wpool_ref.shape[0]
    n_heads = st_ref.shape[0]
    d_hgrn = n_heads * HEAD
    nc = tb // HGRN_CHUNK
    nblk = tb // HGRN_BLOCK
    j = step % tiles_per_seq
    last = tiles_per_seq - 1
    o_q, o_f, o_i, o_g = d_pool, d_pool + d_hgrn, d_pool + 2 * d_hgrn, d_pool + 3 * d_hgrn
    sections = [(pu_ref, 0), (pq_ref, o_q), (pf_ref, o_f), (pi_ref, o_i), (pg_ref, o_g)]

    def chunks(dst_ref, col0):
        def make(c0, c1):
            def run():
                dst_ref[:, c0:c1] = jnp.dot(hn_ref[...], win_ref[:, col0 + c0:col0 + c1],
                                            preferred_element_type=F32)
            return run
        width = dst_ref.shape[1]
        return [make(c0, min(c0 + MXU_COLS_V7X, width)) for c0 in range(0, width, MXU_COLS_V7X)]

    @pl.when(step == 0)
    def _():
        hn_ref[...] = _rms(x0_ref[...], g_ref[...]).astype(BF16)
        for dst_ref, col0 in sections:
            for run in chunks(dst_ref, col0):
                run()

    r1_, r2_ = POOL_PAD + POOL_HIST, POOL_PAD + POOL_HIST + tb

    @pl.when(j == 0)
    def _():
        ext_ref[0:r1_, :] = jnp.zeros((r1_, d_pool), F32)
        lvl_ref[:, 0:POOL_PAD, :] = jnp.zeros((lvl_ref.shape[0], POOL_PAD, LANES), F32)
        st_ref[...] = jnp.zeros(st_ref.shape, F32)

    hn_ref[...] = _rms(xn_ref[...], g_ref[...]).astype(BF16)

    u = pu_ref[...]
    ext_ref[r1_:r2_, :] = u
    groups_per_tile = LANES // POOL_GROUP_DIM
    lane_grp = lax.broadcasted_iota(jnp.int32, (tb, LANES), 1) // POOL_GROUP_DIM
    pos = j * tb + lax.broadcasted_iota(jnp.int32, (tb, LANES), 0)
    means = []
    for lt in range(d_pool // LANES):
        wins = POOL_WINDOWS[lt * groups_per_tile:(lt + 1) * groups_per_tile]
        src, src_lanes, w, nbuf, got = ext_ref, slice(lt * LANES, (lt + 1) * LANES), 1, 0, {}
        while w < wins[-1]:
            new = src[POOL_PAD:r2_, src_lanes] + src[POOL_PAD - w:r2_ - w, src_lanes]
            w *= 2
            if w in wins:
                got[w] = new[POOL_HIST:]
            if w < wins[-1]:
                lvl_ref[nbuf, POOL_PAD:r2_, :] = new
                src, src_lanes, nbuf = lvl_ref.at[nbuf], slice(0, LANES), 1 - nbuf
        ssum, win = got[wins[-1]], jnp.full((tb, LANES), wins[-1], jnp.int32)
        for g_ in range(groups_per_tile - 2, -1, -1):
            ssum = jnp.where(lane_grp == g_, got[wins[g_]], ssum)
            win = jnp.where(lane_grp == g_, wins[g_], win)
        cnt = jnp.minimum(win, pos + 1).astype(F32)
        means.append(ssum / cnt)
    diff = (jnp.concatenate(means, axis=1) - u).astype(BF16)
    pool_out = jnp.dot(diff, wpool_ref[...], preferred_element_type=F32) * pscale_ref[...]
    mixed_ref[:, 0:d_pool] = pool_out.astype(BF16)

    @pl.when(j == last)
    def _():
        pstate_ref[...] = ext_ref[pl.ds(r2_ - POOL_BUF, POOL_BUF), :]

    ext_ref[POOL_PAD:r1_, :] = ext_ref[POOL_PAD + tb:r1_ + tb, :]
    for run in chunks(pu_ref, 0):
        run()

    lb = _lower_bound(lbl_ref[...], layer)
    for c in range(nc):
        rows = slice(c * HGRN_CHUNK, (c + 1) * HGRN_CHUNK)
        fz = pf_ref[rows, :]
        a = jnp.exp(-jnp.abs(fz))
        r = 1.0 / (1.0 + a)
        ar = a * r
        sig = jnp.where(fz >= 0, r, ar)
        sig_neg = jnp.where(fz >= 0, ar, r)
        logf = jnp.log(lb + (1.0 - lb) * sig)
        k_ref[rows, :] = (1.0 - lb) * sig_neg
        l_hi = logf.astype(BF16)
        r1 = logf - l_hi.astype(F32)
        l_mid = r1.astype(BF16)
        lsplit_ref[rows, 0:d_hgrn] = l_hi
        lsplit_ref[rows, d_hgrn:2 * d_hgrn] = l_mid
        lsplit_ref[rows, 2 * d_hgrn:3 * d_hgrn] = (r1 - l_mid.astype(F32)).astype(BF16)
        iz = pi_ref[rows, :]
        v_ref[rows, :] = (iz * _sigmoid(iz)).astype(BF16)
        gate_ref[rows, :] = gain_ref[...] * _sigmoid(pg_ref[rows, :])

    ri = lax.broadcasted_iota(jnp.int32, (tb, tb), 0)
    ci = lax.broadcasted_iota(jnp.int32, (tb, tb), 1)
    tri = ((ri // HGRN_CHUNK == ci // HGRN_CHUNK) & (ci <= ri)).astype(BF16)
    cs = jnp.dot(tri, lsplit_ref[...], preferred_element_type=F32)
    bcum_ref[...] = cs[:, 0:d_hgrn] + cs[:, d_hgrn:2 * d_hgrn] + cs[:, 2 * d_hgrn:3 * d_hgrn]
    for run in chunks(pf_ref, o_f) + chunks(pi_ref, o_i):
        run()

    nsub = HGRN_BLOCK // HGRN_CHUNK
    half = nsub // 2
    for b in range(nblk):
        blast = [bcum_ref[(b * nsub + i + 1) * HGRN_CHUNK - 1:(b * nsub + i + 1) * HGRN_CHUNK, :]
                 for i in range(nsub)]
        pre = [jnp.zeros_like(blast[0])]
        for i in range(nsub):
            pre.append(pre[-1] + blast[i])
        dec_ref[b:b + 1, :] = jnp.exp(pre[nsub])
        for i in range(nsub):
            c = b * nsub + i
            rows = slice(c * HGRN_CHUNK, (c + 1) * HGRN_CHUNK)
            bc = bcum_ref[rows, :]
            qd = pq_ref[rows, :] * jnp.exp(bc)
            kd = k_ref[rows, :] * jnp.exp(-bc)
            ke = kd * jnp.exp(blast[i])
            qd_ref[rows, :] = qd.astype(BF16)
            kd_ref[rows, :] = kd.astype(BF16)
            ke_ref[rows, :] = ke.astype(BF16)
            qb_ref[rows, :] = (qd * jnp.exp(pre[i])).astype(BF16)
            kb_ref[rows, :] = (ke * jnp.exp(pre[nsub] - pre[i + 1])).astype(BF16)
            if i >= half:
                qr_ref[rows, :] = (qd * jnp.exp(pre[i] - pre[half])).astype(BF16)
                kc_ref[rows, :] = jnp.zeros((HGRN_CHUNK, d_hgrn), BF16)
            else:
                kc_ref[rows, :] = (ke * jnp.exp(pre[half] - pre[i + 1])).astype(BF16)

    rb = lax.broadcasted_iota(jnp.int32, (HGRN_BLOCK, HGRN_BLOCK), 0)
    cb = lax.broadcasted_iota(jnp.int32, (HGRN_BLOCK, HGRN_BLOCK), 1)
    rsub, csub = rb // HGRN_CHUNK, cb // HGRN_CHUNK
    m_diag = (rsub == csub) & (cb <= rb)
    m_adj = (csub == rsub - 1) & (rsub != half)
    eye = rb == cb
    hb = HGRN_BLOCK // 2
    nt_dims = (((1,), (1,)), ((), ()))
    tn_dims = (((0,), (0,)), ((), ()))
    fillers = chunks(pq_ref, o_q) + chunks(pg_ref, o_g)

    def fill(n):
        for _ in range(min(n, len(fillers))):
            fillers.pop(0)()

    per_stage = -(-len(fillers) // (3 * nblk))
    for b in range(nblk):
        rows = slice(b * HGRN_BLOCK, (b + 1) * HGRN_BLOCK)
        far = slice(b * HGRN_BLOCK + hb, (b + 1) * HGRN_BLOCK)
        g12s, g3s = [], []
        for hd in range(n_heads):
            sl = slice(hd * HEAD, (hd + 1) * HEAD)
            kk = jnp.concatenate([kd_ref[rows, sl], ke_ref[rows, sl]], axis=0)
            g12s.append(lax.dot_general(qd_ref[rows, sl], kk, nt_dims, preferred_element_type=F32))
            g3s.append(lax.dot_general(qr_ref[far, sl], kc_ref[rows, sl], nt_dims, preferred_element_type=F32))
        fill(per_stage)
        outs = []
        for hd in range(n_heads):
            sl = slice(hd * HEAD, (hd + 1) * HEAD)
            g12 = g12s[hd]
            scores = jnp.where(m_diag, g12[:, 0:HGRN_BLOCK], jnp.where(m_adj, g12[:, HGRN_BLOCK:], 0.0))
            scores = jnp.concatenate([scores[0:hb], scores[hb:] + g3s[hd]], axis=0).astype(BF16)
            v_blk = v_ref[rows, sl]
            st = st_ref[hd]
            lhs = jnp.concatenate([scores, qb_ref[rows, sl]], axis=1)
            rhs = jnp.concatenate([v_blk, st.astype(BF16)], axis=0)
            outs.append(jnp.dot(lhs, rhs, preferred_element_type=F32))
            d_col = jnp.sum(jnp.where(eye, dec_ref[b:b + 1, sl], 0.0), axis=1, keepdims=True)
            st_ref[hd] = st * d_col + lax.dot_general(kb_ref[rows, sl], v_blk, tn_dims,
                                                      preferred_element_type=F32)
        fill(per_stage)
        for hd in range(n_heads):
            sl = slice(hd * HEAD, (hd + 1) * HEAD)
            o = outs[hd]
            o = o * lax.rsqrt(jnp.mean(o * o, axis=-1, keepdims=True) + EPS)
            mixed_ref[rows, d_pool + hd * HEAD:d_pool + (hd + 1) * HEAD] = (o * gate_ref[rows, sl]).astype(BF16)
        fill(per_stage)
    fill(len(fillers))

    @pl.when(j == last)
    def _():
        hstate_ref[...] = st_ref[...]


def even_mix_prompt(x, g, w_in, wpool_bd, pscale, lbl, gain, *, layer, tb):
    b, t, d = x.shape
    n_in = w_in.shape[1]
    d_pool = wpool_bd.shape[0]
    d_hgrn = gain.shape[1]
    n_heads = d_hgrn // HEAD
    tps = t // tb
    n_tiles = b * tps
    kern = functools.partial(_even_prompt_kernel, layer=layer, tb=tb, tiles_per_seq=tps)

    def next_tile(s):
        tile = jnp.minimum(s + 1, n_tiles - 1)
        return tile // tps, tile % tps

    assert d_pool % LANES == 0 and LANES % POOL_GROUP_DIM == 0 and tb % HGRN_BLOCK == 0
    return pl.pallas_call(
        kern,
        grid=(n_tiles,),
        in_specs=[pl.BlockSpec((None, tb, d), lambda s: (*next_tile(s), 0)),
                  pl.BlockSpec((None, tb, d), lambda s: (0, 0, 0)),
                  _resident((1, d)), _resident((d, n_in)), _resident((d_pool, d_pool)),
                  _resident((1, d_pool)), _resident(lbl.shape), _resident((1, d_hgrn))],
        out_specs=[pl.BlockSpec((None, tb, d), lambda s: (s // tps, s % tps, 0)),
                   pl.BlockSpec((None, POOL_BUF, d_pool), lambda s: (s // tps, 0, 0)),
                   pl.BlockSpec((None, n_heads, HEAD, HEAD), lambda s: (s // tps, 0, 0, 0))],
        out_shape=[jax.ShapeDtypeStruct((b, t, d), BF16),
                   jax.ShapeDtypeStruct((b, POOL_BUF, d_pool), F32),
                   jax.ShapeDtypeStruct((b, n_heads, HEAD, HEAD), F32)],
        scratch_shapes=[pltpu.VMEM((tb, d_pool), F32)]
                       + [pltpu.VMEM((tb, d_hgrn), F32)] * 4
                       + [pltpu.VMEM((tb, d), BF16),
                        pltpu.VMEM((POOL_PAD + POOL_HIST + tb, d_pool), F32),
                        pltpu.VMEM((2, POOL_PAD + POOL_HIST + tb, LANES), F32),
                        pltpu.VMEM((n_heads, HEAD, HEAD), F32),
                        pltpu.VMEM((tb, d_hgrn), F32),
                        pltpu.VMEM((tb, 3 * d_hgrn), BF16),
                        pltpu.VMEM((tb, d_hgrn), F32),
                        pltpu.VMEM((tb // HGRN_BLOCK, d_hgrn), F32),
                        pltpu.VMEM((tb, d_hgrn), F32)]
                       + [pltpu.VMEM((tb, d_hgrn), BF16)] * 8,
        compiler_params=pltpu.CompilerParams(dimension_semantics=("arbitrary",),
                                             vmem_limit_bytes=VMEM_LIMIT_V7X),
        name="even_mix_prompt",
    )(x, x, g, w_in, wpool_bd, pscale, lbl, gain)


def _even_sample_kernel(proj_ref, pool_ref, hst_ref, wpool_ref, pscale_ref, lbl_ref, gain_ref,
                        mixed_ref, npool_ref, nhst_ref, f_ref, k_ref, q_ref, v_ref, o_ref, *, layer, pos0):
    bb = proj_ref.shape[0]
    d_pool = wpool_ref.shape[0]
    n_heads = hst_ref.shape[1]
    d_hgrn = n_heads * HEAD

    u = proj_ref[:, 0:d_pool]
    acc = u
    snaps = {}
    for s in range(1, POOL_WINDOWS[-1]):
        r0 = (POOL_BUF - s) * d_pool
        acc = acc + pool_ref[:, r0:r0 + d_pool]
        if s + 1 in POOL_WINDOWS:
            snaps[s + 1] = acc
    ssum, win = _pool_select(snaps, u.shape)
    cnt = jnp.minimum(win, pos0 + 1).astype(F32)
    diff = (ssum / cnt - u).astype(BF16)
    pool_out = jnp.dot(diff, wpool_ref[...], preferred_element_type=F32) * pscale_ref[...]
    mixed_ref[:, 0:d_pool] = pool_out.astype(BF16)
    npool_ref[:, 0:(POOL_BUF - 1) * d_pool] = pool_ref[:, d_pool:POOL_BUF * d_pool]
    npool_ref[:, (POOL_BUF - 1) * d_pool:POOL_BUF * d_pool] = u

    lb = _lower_bound(lbl_ref[...], layer)
    fz = proj_ref[:, d_pool + d_hgrn:d_pool + 2 * d_hgrn]
    iz = proj_ref[:, d_pool + 2 * d_hgrn:d_pool + 3 * d_hgrn]
    f_ref[...] = lb + (1.0 - lb) * _sigmoid(fz)
    k_ref[...] = (1.0 - lb) * _sigmoid(-fz)
    q_ref[...] = proj_ref[:, d_pool:d_pool + d_hgrn]
    v_ref[...] = iz * _sigmoid(iz)
    eye = (lax.broadcasted_iota(jnp.int32, (HEAD, HEAD), 0)
           == lax.broadcasted_iota(jnp.int32, (HEAD, HEAD), 1))

    def to_column(row):
        return jnp.sum(jnp.where(eye, row, 0.0), axis=1, keepdims=True)

    for i in range(bb):
        for hd in range(n_heads):
            sl = slice(hd * HEAD, (hd + 1) * HEAD)
            f_col = to_column(f_ref[i:i + 1, sl])
            k_col = to_column(k_ref[i:i + 1, sl])
            q_col = to_column(q_ref[i:i + 1, sl])
            s_new = f_col * hst_ref[i, hd] + k_col * v_ref[i:i + 1, sl]
            nhst_ref[i, hd] = s_new
            o_ref[i:i + 1, sl] = jnp.sum(q_col * s_new, axis=0, keepdims=True)

    gz = proj_ref[:, d_pool + 3 * d_hgrn:d_pool + 4 * d_hgrn]
    gate = gain_ref[...] * _sigmoid(gz)
    for hd in range(n_heads):
        sl = slice(hd * HEAD, (hd + 1) * HEAD)
        o = o_ref[:, sl]
        o = o * lax.rsqrt(jnp.mean(o * o, axis=-1, keepdims=True) + EPS)
        mixed_ref[:, d_pool + hd * HEAD:d_pool + (hd + 1) * HEAD] = (o * gate[:, sl]).astype(BF16)


def even_mix_sample(proj, pool_flat, hstate, wpool_bd, pscale, lbl, gain, *, layer, pos0, bb):
    b, n_in = proj.shape
    d_pool = wpool_bd.shape[0]
    d_hgrn = gain.shape[1]
    n_heads = d_hgrn // HEAD
    d = d_pool + d_hgrn
    kern = functools.partial(_even_sample_kernel, layer=layer, pos0=pos0)
    return pl.pallas_call(
        kern,
        grid=(b // bb,),
        in_specs=[pl.BlockSpec((bb, n_in), lambda i: (i, 0)),
                  pl.BlockSpec((bb, POOL_BUF * d_pool), lambda i: (i, 0)),
                  pl.BlockSpec((bb, n_heads, HEAD, HEAD), lambda i: (i, 0, 0, 0)),
                  _resident((d_pool, d_pool)), _resident((1, d_pool)), _resident(lbl.shape),
                  _resident((1, d_hgrn))],
        out_specs=[pl.BlockSpec((bb, d), lambda i: (i, 0)),
                   pl.BlockSpec((bb, POOL_BUF * d_pool), lambda i: (i, 0)),
                   pl.BlockSpec((bb, n_heads, HEAD, HEAD), lambda i: (i, 0, 0, 0))],
        out_shape=[jax.ShapeDtypeStruct((b, d), BF16),
                   jax.ShapeDtypeStruct((b, POOL_BUF * d_pool), F32),
                   jax.ShapeDtypeStruct((b, n_heads, HEAD, HEAD), F32)],
        scratch_shapes=[pltpu.VMEM((bb, d_hgrn), F32)] * 5,
        compiler_params=pltpu.CompilerParams(dimension_semantics=("arbitrary",),
                                             vmem_limit_bytes=VMEM_LIMIT_V7X),
        name="even_mix_sample",
    )(proj, pool_flat, hstate, wpool_bd, pscale, lbl, gain)


def _odd_prompt_kernel(x_ref, g_ref, win_ref, cw_ref, mixed_ref, cstate_ref, zext_ref, *, tb):
    j = pl.program_id(1)
    last = pl.num_programs(1) - 1
    dc = cw_ref.shape[1]

    @pl.when(j == 0)
    def _():
        zext_ref[0:CONV_HIST, :] = jnp.zeros((CONV_HIST, dc), F32)

    h = _rms(x_ref[...], g_ref[...]).astype(BF16)
    cg = jnp.dot(h, win_ref[:, dc:2 * dc], preferred_element_type=F32)
    hv = jnp.dot(h, win_ref[:, 2 * dc:3 * dc], preferred_element_type=F32)
    z = cg * hv
    zext_ref[CONV_HIST:CONV_HIST + tb, :] = z
    conv = cw_ref[CONV_WIDTH - 1:CONV_WIDTH, :] * z
    for s in range(1, CONV_WIDTH):
        conv = conv + cw_ref[CONV_WIDTH - 1 - s:CONV_WIDTH - s, :] * zext_ref[pl.ds(CONV_HIST - s, tb), :]
    bg = jnp.dot(h, win_ref[:, 0:dc], preferred_element_type=F32)
    mixed_ref[...] = (bg * conv).astype(BF16)

    @pl.when(j == last)
    def _():
        cstate_ref[...] = zext_ref[pl.ds(CONV_HIST + tb - CONV_BUF, CONV_BUF), :]

    zext_ref[0:CONV_HIST, :] = zext_ref[tb:tb + CONV_HIST, :]


def odd_mix_prompt(x, g, w_in, conv_w, *, tb):
    b, t, d = x.shape
    dc = conv_w.shape[1]
    kern = functools.partial(_odd_prompt_kernel, tb=tb)
    return pl.pallas_call(
        kern,
        grid=(b, t // tb),
        in_specs=[pl.BlockSpec((None, tb, d), lambda i, j: (i, j, 0)),
                  _resident((1, d)), _resident((d, 3 * dc)), _resident((CONV_WIDTH, dc))],
        out_specs=[pl.BlockSpec((None, tb, dc), lambda i, j: (i, j, 0)),
                   pl.BlockSpec((None, CONV_BUF, dc), lambda i, j: (i, 0, 0))],
        out_shape=[jax.ShapeDtypeStruct((b, t, dc), BF16),
                   jax.ShapeDtypeStruct((b, CONV_BUF, dc), F32)],
        scratch_shapes=[pltpu.VMEM((CONV_HIST + tb, dc), F32)],
        compiler_params=pltpu.CompilerParams(dimension_semantics=("arbitrary", "arbitrary"),
                                             vmem_limit_bytes=VMEM_LIMIT_V7X),
        name="odd_mix_prompt",
    )(x, g, w_in, conv_w)


def _odd_sample_kernel(x_ref, g_ref, win_ref, cw_ref, cst_ref, mixed_ref, ncst_ref):
    dc = cw_ref.shape[1]
    h = _rms(x_ref[...], g_ref[...]).astype(BF16)
    cg = jnp.dot(h, win_ref[:, dc:2 * dc], preferred_element_type=F32)
    hv = jnp.dot(h, win_ref[:, 2 * dc:3 * dc], preferred_element_type=F32)
    z = cg * hv
    conv = cw_ref[CONV_WIDTH - 1:CONV_WIDTH, :] * z
    for s in range(1, CONV_WIDTH):
        r0 = (CONV_BUF - s) * dc
        conv = conv + cw_ref[CONV_WIDTH - 1 - s:CONV_WIDTH - s, :] * cst_ref[:, r0:r0 + dc]
    bg = jnp.dot(h, win_ref[:, 0:dc], preferred_element_type=F32)
    mixed_ref[...] = (bg * conv).astype(BF16)
    ncst_ref[:, 0:(CONV_BUF - 1) * dc] = cst_ref[:, dc:CONV_BUF * dc]
    ncst_ref[:, (CONV_BUF - 1) * dc:CONV_BUF * dc] = z


def odd_mix_sample(x, g, w_in, conv_w, cstate_flat):
    b, d = x.shape
    dc = conv_w.shape[1]
    return pl.pallas_call(
        _odd_sample_kernel,
        grid=(1,),
        in_specs=[_resident((b, d)), _resident((1, d)), _resident((d, 3 * dc)),
                  _resident((CONV_WIDTH, dc)), _resident((b, CONV_BUF * dc))],
        out_specs=[pl.BlockSpec((b, dc), lambda i: (0, 0)),
                   pl.BlockSpec((b, CONV_BUF * dc), lambda i: (0, 0))],
        out_shape=[jax.ShapeDtypeStruct((b, dc), BF16),
                   jax.ShapeDtypeStruct((b, CONV_BUF * dc), F32)],
        compiler_params=pltpu.CompilerParams(dimension_semantics=("arbitrary",),
                                             vmem_limit_bytes=VMEM_LIMIT_V7X),
        name="odd_mix_sample",
    )(x, g, w_in, conv_w, cstate_flat)


def _block_diag(w):
    g, c, _ = w.shape
    rows = [jnp.pad(w[i], ((0, 0), (i * c, (g - 1 - i) * c))) for i in range(g)]
    return jnp.concatenate(rows, axis=0)


def _trunk(x, states, pos0, p, *, prompt):
    depth = p["norm_mix"].shape[0]
    if prompt:
        b, t, d = x.shape
        xf = x.reshape(b * t, d)
        bm = 512
    else:
        b, d = x.shape
        xf = x
        bm = b
    new_pool, new_hgrn, new_conv = [], [], []
    for l in range(depth):
        g_mix = p["norm_mix"][l][None]
        final = l == depth - 1
        if l % 2 == 0:
            e = l // 2
            if prompt:
                mixed, p_new, s_new = even_mix_prompt(
                    xf.reshape(b, t, d), g_mix, p["even_w_in"][e], p["pool_bd"][e], p["pool_scale"][e][None],
                    p["hgrn_lb_logits"], p["hgrn_gain"][e][None], layer=l, tb=256)
                mixed = mixed.reshape(b * t, d)
            else:
                pool_st, hgrn_st, _ = states
                proj = norm_proj(xf, g_mix, p["even_w_in"][e], bm=bm)
                mixed, p_new, s_new = even_mix_sample(
                    proj, pool_st[e].reshape(b, -1), hgrn_st[e], p["pool_bd"][e], p["pool_scale"][e][None],
                    p["hgrn_lb_logits"], p["hgrn_gain"][e][None], layer=l, pos0=pos0, bb=8)
                p_new = p_new.reshape(b, POOL_BUF, -1)
            new_pool.append(p_new)
            new_hgrn.append(s_new)
            w_out = p["even_w_out"][e]
        else:
            o = l // 2
            if prompt:
                mixed, c_new = odd_mix_prompt(xf.reshape(b, t, d), g_mix, p["odd_w_in"][o], p["conv_w"][o], tb=512)
                mixed = mixed.reshape(b * t, d)
            else:
                conv_st = states[2]
                mixed, c_new = odd_mix_sample(xf, g_mix, p["odd_w_in"][o], p["conv_w"][o],
                                              conv_st[o].reshape(b, -1))
                c_new = c_new.reshape(b, CONV_BUF, -1)
            new_conv.append(c_new)
            w_out = p["odd_w_out"][o]
        xf = out_mlp(mixed, xf, w_out, p["norm_mlp"][l][None], p["ff_w1"][l], p["ff_w2"][l],
                     p["norm_final"][None], final_norm=final, bm=bm)
    y = xf.reshape(x.shape) if prompt else xf
    return y, jnp.stack(new_pool), jnp.stack(new_hgrn), jnp.stack(new_conv)


def kernel(x_prompt, x_sample, state_pool, state_hgrn, state_conv, norm_mix, norm_mlp, norm_final, even_w_in, pool_w, pool_scale, hgrn_lb_logits, hgrn_gain, even_w_out, odd_w_in, conv_w, odd_w_out, ff_w1, ff_w2):
    p = dict(norm_mix=norm_mix, norm_mlp=norm_mlp, norm_final=norm_final,
             even_w_in=even_w_in.astype(BF16), pool_bd=[_block_diag(w).astype(BF16) for w in pool_w],
             pool_scale=pool_scale, hgrn_lb_logits=hgrn_lb_logits, hgrn_gain=hgrn_gain,
             even_w_out=even_w_out.astype(BF16), odd_w_in=odd_w_in.astype(BF16), conv_w=conv_w,
             odd_w_out=odd_w_out.astype(BF16), ff_w1=ff_w1.astype(BF16), ff_w2=ff_w2.astype(BF16))
    y_p, pool_p, hgrn_p, conv_p = _trunk(x_prompt, None, 0, p, prompt=True)
    db, ds, d = x_sample.shape
    y_s, pool_s, hgrn_s, conv_s = _trunk(x_sample.reshape(db * ds, d), (state_pool, state_hgrn, state_conv),
                                         PAST_LEN, p, prompt=False)
    return (y_p, y_s.reshape(db, ds, d), pool_p, hgrn_p, conv_p, pool_s, hgrn_s, conv_s)
```

```python
import functools

import jax
import jax.numpy as jnp
from jax import lax
from jax.experimental import pallas as pl
from jax.experimental.pallas import tpu as pltpu

F32 = jnp.float32
BF16 = jnp.bfloat16

EPS = 1e-6
PAST_LEN = 16384
POOL_WINDOWS = (2, 4, 8, 16)
POOL_GROUP_DIM = 64
POOL_BUF = max(POOL_WINDOWS) - 1
POOL_HIST = 16
POOL_PAD = 8
LANES = 128
HGRN_CHUNK = 32
HGRN_BLOCK = 128
HEAD = 128
CONV_WIDTH = 3
CONV_BUF = CONV_WIDTH - 1
CONV_HIST = 8

VMEM_LIMIT_V7X = 56 * 1024 * 1024
MXU_COLS_V7X = 256


def _resident(shape):
    nd = len(shape)
    return pl.BlockSpec(shape, lambda *_: (0,) * nd, pipeline_mode=pl.Buffered(1))


def _rms(x, g):
    return x * lax.rsqrt(jnp.mean(x * x, axis=-1, keepdims=True) + EPS) * g


def _sigmoid(x):
    return 1.0 / (1.0 + jnp.exp(-x))


def _lower_bound(lbl, layer):
    e = jnp.exp(lbl - jnp.max(lbl, axis=0, keepdims=True))
    p = e / jnp.sum(e, axis=0, keepdims=True)
    return jnp.sum(p[0:layer + 1], axis=0, keepdims=True)


def _pool_select(snaps, u_shape):
    grp = lax.broadcasted_iota(jnp.int32, u_shape, 1) // POOL_GROUP_DIM
    s = snaps[POOL_WINDOWS[-1]]
    for g in range(len(POOL_WINDOWS) - 2, -1, -1):
        s = jnp.where(grp == g, snaps[POOL_WINDOWS[g]], s)
    win = jnp.left_shift(2, grp)
    return s, win


def _norm_proj_kernel(x_ref, g_ref, w_ref, o_ref):
    h = _rms(x_ref[...], g_ref[...]).astype(BF16)
    o_ref[...] = jnp.dot(h, w_ref[...], preferred_element_type=F32)


def norm_proj(x, g, w, *, bm):
    m, d = x.shape
    n = w.shape[1]
    return pl.pallas_call(
        _norm_proj_kernel,
        grid=(m // bm,),
        in_specs=[pl.BlockSpec((bm, d), lambda i: (i, 0)), _resident((1, d)), _resident((d, n))],
        out_specs=pl.BlockSpec((bm, n), lambda i: (i, 0)),
        out_shape=jax.ShapeDtypeStruct((m, n), F32),
        compiler_params=pltpu.CompilerParams(dimension_semantics=("arbitrary",),
                                             vmem_limit_bytes=VMEM_LIMIT_V7X),
        name="norm_proj",
    )(x, g, w)


def _out_mlp_kernel(m_ref, x_ref, wout_ref, g_ref, w1_ref, w2_ref, gf_ref, o_ref, *, final_norm, ff_chunk):
    y = x_ref[...] + jnp.dot(m_ref[...], wout_ref[...], preferred_element_type=F32)
    h = _rms(y, g_ref[...]).astype(BF16)
    acc = y
    d_ff = w1_ref.shape[1]
    for c in range(d_ff // ff_chunk):
        a = jnp.dot(h, w1_ref[:, c * ff_chunk:(c + 1) * ff_chunk], preferred_element_type=F32)
        a = jnp.square(jnp.maximum(a, 0.0)).astype(BF16)
        acc = acc + jnp.dot(a, w2_ref[c * ff_chunk:(c + 1) * ff_chunk, :], preferred_element_type=F32)
    if final_norm:
        acc = _rms(acc, gf_ref[...])
    o_ref[...] = acc


def out_mlp(mixed, x, w_out, g_mlp, w1, w2, g_final, *, final_norm, bm):
    m, d = x.shape
    d_ff = w1.shape[1]
    kern = functools.partial(_out_mlp_kernel, final_norm=final_norm, ff_chunk=1024)
    return pl.pallas_call(
        kern,
        grid=(m // bm,),
        in_specs=[pl.BlockSpec((bm, d), lambda i: (i, 0)), pl.BlockSpec((bm, d), lambda i: (i, 0)),
                  _resident((d, d)), _resident((1, d)), _resident((d, d_ff)), _resident((d_ff, d)),
                  _resident((1, d))],
        out_specs=pl.BlockSpec((bm, d), lambda i: (i, 0)),
        out_shape=jax.ShapeDtypeStruct((m, d), F32),
        compiler_params=pltpu.CompilerParams(dimension_semantics=("arbitrary",),
                                             vmem_limit_bytes=VMEM_LIMIT_V7X),
        name="out_mlp",
    )(mixed, x, w_out, g_mlp, w1, w2, g_final)


def _even_prompt_kernel(xn_ref, x0_ref, g_ref, win_ref, wpool_ref, pscale_ref, lbl_ref, gain_ref,
                        mixed_ref, pstate_ref, hstate_ref,
                        pu_ref, pq_ref, pf_ref, pi_ref, pg_ref, hn_ref, ext_ref, lvl_ref, st_ref, k_ref, lsplit_ref,
                        bcum_ref, dec_ref, gate_ref, v_ref, qd_ref, kd_ref, ke_ref, qb_ref, kb_ref, qr_ref, kc_ref,
                        *, layer, tb, tiles_per_seq):
    step = pl.program_id(0)
    d_pool = wpool_ref.shape[0]
    n_heads = st_ref.shape[0]
    d_hgrn = n_heads * HEAD
    nc = tb // HGRN_CHUNK
    nblk = tb // HGRN_BLOCK
    j = step % tiles_per_seq
    last = tiles_per_seq - 1
    o_q, o_f, o_i, o_g = d_pool, d_pool + d_hgrn, d_pool + 2 * d_hgrn, d_pool + 3 * d_hgrn
    sections = [(pu_ref, 0), (pq_ref, o_q), (pf_ref, o_f), (pi_ref, o_i), (pg_ref, o_g)]

    def chunks(dst_ref, col0):
        def make(c0, c1):
            def run():
                dst_ref[:, c0:c1] = jnp.dot(hn_ref[...], win_ref[:, col0 + c0:col0 + c1],
                                            preferred_element_type=F32)
            return run
        width = dst_ref.shape[1]
        return [make(c0, min(c0 + MXU_COLS_V7X, width)) for c0 in range(0, width, MXU_COLS_V7X)]

    @pl.when(step == 0)
    def _():
        hn_ref[...] = _rms(x0_ref[...], g_ref[...]).astype(BF16)
        for dst_ref, col0 in sections:
            for run in chunks(dst_ref, col0):
                run()

    r1_, r2_ = POOL_PAD + POOL_HIST, POOL_PAD + POOL_HIST + tb

    @pl.when(j == 0)
    def _():
        ext_ref[0:r1_, :] = jnp.zeros((r1_, d_pool), F32)
        lvl_ref[:, 0:POOL_PAD, :] = jnp.zeros((lvl_ref.shape[0], POOL_PAD, LANES), F32)
        st_ref[...] = jnp.zeros(st_ref.shape, F32)

    hn_ref[...] = _rms(xn_ref[...], g_ref[...]).astype(BF16)

    u = pu_ref[...]
    ext_ref[r1_:r2_, :] = u
    groups_per_tile = LANES // POOL_GROUP_DIM
    lane_grp = lax.broadcasted_iota(jnp.int32, (tb, LANES), 1) // POOL_GROUP_DIM
    pos = j * tb + lax.broadcasted_iota(jnp.int32, (tb, LANES), 0)
    means = []
    for lt in range(d_pool // LANES):
        wins = POOL_WINDOWS[lt * groups_per_tile:(lt + 1) * groups_per_tile]
        src, src_lanes, w, nbuf, got = ext_ref, slice(lt * LANES, (lt + 1) * LANES), 1, 0, {}
        while w < wins[-1]:
            new = src[POOL_PAD:r2_, src_lanes] + src[POOL_PAD - w:r2_ - w, src_lanes]
            w *= 2
            if w in wins:
                got[w] = new[POOL_HIST:]
            if w < wins[-1]:
                lvl_ref[nbuf, POOL_PAD:r2_, :] = new
                src, src_lanes, nbuf = lvl_ref.at[nbuf], slice(0, LANES), 1 - nbuf
        ssum, win = got[wins[-1]], jnp.full((tb, LANES), wins[-1], jnp.int32)
        for g_ in range(groups_per_tile - 2, -1, -1):
            ssum = jnp.where(lane_grp == g_, got[wins[g_]], ssum)
            win = jnp.where(lane_grp == g_, wins[g_], win)
        cnt = jnp.minimum(win, pos + 1).astype(F32)
        means.append(ssum / cnt)
    diff = (jnp.concatenate(means, axis=1) - u).astype(BF16)
    pool_out = jnp.dot(diff, wpool_ref[...], preferred_element_type=F32) * pscale_ref[...]
    mixed_ref[:, 0:d_pool] = pool_out.astype(BF16)

    @pl.when(j == last)
    def _():
        pstate_ref[...] = ext_ref[pl.ds(r2_ - POOL_BUF, POOL_BUF), :]

    ext_ref[POOL_PAD:r1_, :] = ext_ref[POOL_PAD + tb:r1_ + tb, :]
    for run in chunks(pu_ref, 0):
        run()

    lb = _lower_bound(lbl_ref[...], layer)
    for c in range(nc):
        rows = slice(c * HGRN_CHUNK, (c + 1) * HGRN_CHUNK)
        fz = pf_ref[rows, :]
        a = jnp.exp(-jnp.abs(fz))
        r = 1.0 / (1.0 + a)
        ar = a * r
        sig = jnp.where(fz >= 0, r, ar)
        sig_neg = jnp.where(fz >= 0, ar, r)
        logf = jnp.log(lb + (1.0 - lb) * sig)
        k_ref[rows, :] = (1.0 - lb) * sig_neg
        l_hi = logf.astype(BF16)
        r1 = logf - l_hi.astype(F32)
        l_mid = r1.astype(BF16)
        lsplit_ref[rows, 0:d_hgrn] = l_hi
        lsplit_ref[rows, d_hgrn:2 * d_hgrn] = l_mid
        lsplit_ref[rows, 2 * d_hgrn:3 * d_hgrn] = (r1 - l_mid.astype(F32)).astype(BF16)
        iz = pi_ref[rows, :]
        v_ref[rows, :] = (iz * _sigmoid(iz)).astype(BF16)
        gate_ref[rows, :] = gain_ref[...] * _sigmoid(pg_ref[rows, :])

    ri = lax.broadcasted_iota(jnp.int32, (tb, tb), 0)
    ci = lax.broadcasted_iota(jnp.int32, (tb, tb), 1)
    tri = ((ri // HGRN_CHUNK == ci // HGRN_CHUNK) & (ci <= ri)).astype(BF16)
    cs = jnp.dot(tri, lsplit_ref[...], preferred_element_type=F32)
    bcum_ref[...] = cs[:, 0:d_hgrn] + cs[:, d_hgrn:2 * d_hgrn] + cs[:, 2 * d_hgrn:3 * d_hgrn]
    for run in chunks(pf_ref, o_f) + chunks(pi_ref, o_i):
        run()

    nsub = HGRN_BLOCK // HGRN_CHUNK
    half = nsub // 2
    for b in range(nblk):
        blast = [bcum_ref[(b * nsub + i + 1) * HGRN_CHUNK - 1:(b * nsub + i + 1) * HGRN_CHUNK, :]
                 for i in range(nsub)]
        pre = [jnp.zeros_like(blast[0])]
        for i in range(nsub):
            pre.append(pre[-1] + blast[i])
        dec_ref[b:b + 1, :] = jnp.exp(pre[nsub])
        for i in range(nsub):
            c = b * nsub + i
            rows = slice(c * HGRN_CHUNK, (c + 1) * HGRN_CHUNK)
            bc = bcum_ref[rows, :]
            qd = pq_ref[rows, :] * jnp.exp(bc)
            kd = k_ref[rows, :] * jnp.exp(-bc)
            ke = kd * jnp.exp(blast[i])
            qd_ref[rows, :] = qd.astype(BF16)
            kd_ref[rows, :] = kd.astype(BF16)
            ke_ref[rows, :] = ke.astype(BF16)
            qb_ref[rows, :] = (qd * jnp.exp(pre[i])).astype(BF16)
            kb_ref[rows, :] = (ke * jnp.exp(pre[nsub] - pre[i + 1])).astype(BF16)
            if i >= half:
                qr_ref[rows, :] = (qd * jnp.exp(pre[i] - pre[half])).astype(BF16)
                kc_ref[rows, :] = jnp.zeros((HGRN_CHUNK, d_hgrn), BF16)
            else:
                kc_ref[rows, :] = (ke * jnp.exp(pre[half] - pre[i + 1])).astype(BF16)

    rb = lax.broadcasted_iota(jnp.int32, (HGRN_BLOCK, HGRN_BLOCK), 0)
    cb = lax.broadcasted_iota(jnp.int32, (HGRN_BLOCK, HGRN_BLOCK), 1)
    rsub, csub = rb // HGRN_CHUNK, cb // HGRN_CHUNK
    m_diag = (rsub == csub) & (cb <= rb)
    m_adj = (csub == rsub - 1) & (rsub != half)
    eye = rb == cb
    hb = HGRN_BLOCK // 2
    nt_dims = (((1,), (1,)), ((), ()))
    tn_dims = (((0,), (0,)), ((), ()))
    fillers = chunks(pq_ref, o_q) + chunks(pg_ref, o_g)

    def fill(n):
        for _ in range(min(n, len(fillers))):
            fillers.pop(0)()

    per_stage = -(-len(fillers) // (3 * nblk))
    for b in range(nblk):
        rows = slice(b * HGRN_BLOCK, (b + 1) * HGRN_BLOCK)
        far = slice(b * HGRN_BLOCK + hb, (b + 1) * HGRN_BLOCK)
        g12s, g3s = [], []
        for hd in range(n_heads):
            sl = slice(hd * HEAD, (hd + 1) * HEAD)
            kk = jnp.concatenate([kd_ref[rows, sl], ke_ref[rows, sl]], axis=0)
            g12s.append(lax.dot_general(qd_ref[rows, sl], kk, nt_dims, preferred_element_type=F32))
            g3s.append(lax.dot_general(qr_ref[far, sl], kc_ref[rows, sl], nt_dims, preferred_element_type=F32))
        fill(per_stage)
        outs = []
        for hd in range(n_heads):
            sl = slice(hd * HEAD, (hd + 1) * HEAD)
            g12 = g12s[hd]
            scores = jnp.where(m_diag, g12[:, 0:HGRN_BLOCK], jnp.where(m_adj, g12[:, HGRN_BLOCK:], 0.0))
            scores = jnp.concatenate([scores[0:hb], scores[hb:] + g3s[hd]], axis=0).astype(BF16)
            v_blk = v_ref[rows, sl]
            st = st_ref[hd]
            lhs = jnp.concatenate([scores, qb_ref[rows, sl]], axis=1)
            rhs = jnp.concatenate([v_blk, st.astype(BF16)], axis=0)
            outs.append(jnp.dot(lhs, rhs, preferred_element_type=F32))
            d_col = jnp.sum(jnp.where(eye, dec_ref[b:b + 1, sl], 0.0), axis=1, keepdims=True)
            st_ref[hd] = st * d_col + lax.dot_general(kb_ref[rows, sl], v_blk, tn_dims,
                                                      preferred_element_type=F32)
        fill(per_stage)
        for hd in range(n_heads):
            sl = slice(hd * HEAD, (hd + 1) * HEAD)
            o = outs[hd]
            o = o * lax.rsqrt(jnp.mean(o * o, axis=-1, keepdims=True) + EPS)
            mixed_ref[rows, d_pool + hd * HEAD:d_pool + (hd + 1) * HEAD] = (o * gate_ref[rows, sl]).astype(BF16)
        fill(per_stage)
    fill(len(fillers))

    @pl.when(j == last)
    def _():
        hstate_ref[...] = st_ref[...]


def even_mix_prompt(x, g, w_in, wpool_bd, pscale, lbl, gain, *, layer, tb):
    b, t, d = x.shape
    n_in = w_in.shape[1]
    d_pool = wpool_bd.shape[0]
    d_hgrn = gain.shape[1]
    n_heads = d_hgrn // HEAD
    tps = t // tb
    n_tiles = b * tps
    kern = functools.partial(_even_prompt_kernel, layer=layer, tb=tb, tiles_per_seq=tps)

    def next_tile(s):
        tile = jnp.minimum(s + 1, n_tiles - 1)
        return tile // tps, tile % tps

    assert d_pool % LANES == 0 and LANES % POOL_GROUP_DIM == 0 and tb % HGRN_BLOCK == 0
    return pl.pallas_call(
        kern,
        grid=(n_tiles,),
        in_specs=[pl.BlockSpec((None, tb, d), lambda s: (*next_tile(s), 0)),
                  pl.BlockSpec((None, tb, d), lambda s: (0, 0, 0)),
                  _resident((1, d)), _resident((d, n_in)), _resident((d_pool, d_pool)),
                  _resident((1, d_pool)), _resident(lbl.shape), _resident((1, d_hgrn))],
        out_specs=[pl.BlockSpec((None, tb, d), lambda s: (s // tps, s % tps, 0)),
                   pl.BlockSpec((None, POOL_BUF, d_pool), lambda s: (s // tps, 0, 0)),
                   pl.BlockSpec((None, n_heads, HEAD, HEAD), lambda s: (s // tps, 0, 0, 0))],
        out_shape=[jax.ShapeDtypeStruct((b, t, d), BF16),
                   jax.ShapeDtypeStruct((b, POOL_BUF, d_pool), F32),
                   jax.ShapeDtypeStruct((b, n_heads, HEAD, HEAD), F32)],
        scratch_shapes=[pltpu.VMEM((tb, d_pool), F32)]
                       + [pltpu.VMEM((tb, d_hgrn), F32)] * 4
                       + [pltpu.VMEM((tb, d), BF16),
                        pltpu.VMEM((POOL_PAD + POOL_HIST + tb, d_pool), F32),
                        pltpu.VMEM((2, POOL_PAD + POOL_HIST + tb, LANES), F32),
                        pltpu.VMEM((n_heads, HEAD, HEAD), F32),
                        pltpu.VMEM((tb, d_hgrn), F32),
                        pltpu.VMEM((tb, 3 * d_hgrn), BF16),
                        pltpu.VMEM((tb, d_hgrn), F32),
                        pltpu.VMEM((tb // HGRN_BLOCK, d_hgrn), F32),
                        pltpu.VMEM((tb, d_hgrn), F32)]
                       + [pltpu.VMEM((tb, d_hgrn), BF16)] * 8,
        compiler_params=pltpu.CompilerParams(dimension_semantics=("arbitrary",),
                                             vmem_limit_bytes=VMEM_LIMIT_V7X),
        name="even_mix_prompt",
    )(x, x, g, w_in, wpool_bd, pscale, lbl, gain)


def _even_sample_kernel(proj_ref, pool_ref, hst_ref, wpool_ref, pscale_ref, lbl_ref, gain_ref,
                        mixed_ref, npool_ref, nhst_ref, f_ref, k_ref, q_ref, v_ref, o_ref, *, layer, pos0):
    bb = proj_ref.shape[0]
    d_pool = wpool_ref.shape[0]
    n_heads = hst_ref.shape[1]
    d_hgrn = n_heads * HEAD

    u = proj_ref[:, 0:d_pool]
    acc = u
    snaps = {}
    for s in range(1, POOL_WINDOWS[-1]):
        r0 = (POOL_BUF - s) * d_pool
        acc = acc + pool_ref[:, r0:r0 + d_pool]
        if s + 1 in POOL_WINDOWS:
            snaps[s + 1] = acc
    ssum, win = _pool_select(snaps, u.shape)
    cnt = jnp.minimum(win, pos0 + 1).astype(F32)
    diff = (ssum / cnt - u).astype(BF16)
    pool_out = jnp.dot(diff, wpool_ref[...], preferred_element_type=F32) * pscale_ref[...]
    mixed_ref[:, 0:d_pool] = pool_out.astype(BF16)
    npool_ref[:, 0:(POOL_BUF - 1) * d_pool] = pool_ref[:, d_pool:POOL_BUF * d_pool]
    npool_ref[:, (POOL_BUF - 1) * d_pool:POOL_BUF * d_pool] = u

    lb = _lower_bound(lbl_ref[...], layer)
    fz = proj_ref[:, d_pool + d_hgrn:d_pool + 2 * d_hgrn]
    iz = proj_ref[:, d_pool + 2 * d_hgrn:d_pool + 3 * d_hgrn]
    f_ref[...] = lb + (1.0 - lb) * _sigmoid(fz)
    k_ref[...] = (1.0 - lb) * _sigmoid(-fz)
    q_ref[...] = proj_ref[:, d_pool:d_pool + d_hgrn]
    v_ref[...] = iz * _sigmoid(iz)
    eye = (lax.broadcasted_iota(jnp.int32, (HEAD, HEAD), 0)
           == lax.broadcasted_iota(jnp.int32, (HEAD, HEAD), 1))

    def to_column(row):
        return jnp.sum(jnp.where(eye, row, 0.0), axis=1, keepdims=True)

    for i in range(bb):
        for hd in range(n_heads):
            sl = slice(hd * HEAD, (hd + 1) * HEAD)
            f_col = to_column(f_ref[i:i + 1, sl])
            k_col = to_column(k_ref[i:i + 1, sl])
            q_col = to_column(q_ref[i:i + 1, sl])
            s_new = f_col * hst_ref[i, hd] + k_col * v_ref[i:i + 1, sl]
            nhst_ref[i, hd] = s_new
            o_ref[i:i + 1, sl] = jnp.sum(q_col * s_new, axis=0, keepdims=True)

    gz = proj_ref[:, d_pool + 3 * d_hgrn:d_pool + 4 * d_hgrn]
    gate = gain_ref[...] * _sigmoid(gz)
    for hd in range(n_heads):
        sl = slice(hd * HEAD, (hd + 1) * HEAD)
        o = o_ref[:, sl]
        o = o * lax.rsqrt(jnp.mean(o * o, axis=-1, keepdims=True) + EPS)
        mixed_ref[:, d_pool + hd * HEAD:d_pool + (hd + 1) * HEAD] = (o * gate[:, sl]).astype(BF16)


def even_mix_sample(proj, pool_flat, hstate, wpool_bd, pscale, lbl, gain, *, layer, pos0, bb):
    b, n_in = proj.shape
    d_pool = wpool_bd.shape[0]
    d_hgrn = gain.shape[1]
    n_heads = d_hgrn // HEAD
    d = d_pool + d_hgrn
    kern = functools.partial(_even_sample_kernel, layer=layer, pos0=pos0)
    return pl.pallas_call(
        kern,
        grid=(b // bb,),
        in_specs=[pl.BlockSpec((bb, n_in), lambda i: (i, 0)),
                  pl.BlockSpec((bb, POOL_BUF * d_pool), lambda i: (i, 0)),
                  pl.BlockSpec((bb, n_heads, HEAD, HEAD), lambda i: (i, 0, 0, 0)),
                  _resident((d_pool, d_pool)), _resident((1, d_pool)), _resident(lbl.shape),
                  _resident((1, d_hgrn))],
        out_specs=[pl.BlockSpec((bb, d), lambda i: (i, 0)),
                   pl.BlockSpec((bb, POOL_BUF * d_pool), lambda i: (i, 0)),
                   pl.BlockSpec((bb, n_heads, HEAD, HEAD), lambda i: (i, 0, 0, 0))],
        out_shape=[jax.ShapeDtypeStruct((b, d), BF16),
                   jax.ShapeDtypeStruct((b, POOL_BUF * d_pool), F32),
                   jax.ShapeDtypeStruct((b, n_heads, HEAD, HEAD), F32)],
        scratch_shapes=[pltpu.VMEM((bb, d_hgrn), F32)] * 5,
        compiler_params=pltpu.CompilerParams(dimension_semantics=("arbitrary",),
                                             vmem_limit_bytes=VMEM_LIMIT_V7X),
        name="even_mix_sample",
    )(proj, pool_flat, hstate, wpool_bd, pscale, lbl, gain)


def _odd_prompt_kernel(x_ref, g_ref, win_ref, cw_ref, mixed_ref, cstate_ref, zext_ref, *, tb):
    j = pl.program_id(1)
    last = pl.num_programs(1) - 1
    dc = cw_ref.shape[1]

    @pl.when(j == 0)
    def _():
        zext_ref[0:CONV_HIST, :] = jnp.zeros((CONV_HIST, dc), F32)

    h = _rms(x_ref[...], g_ref[...]).astype(BF16)
    cg = jnp.dot(h, win_ref[:, dc:2 * dc], preferred_element_type=F32)
    hv = jnp.dot(h, win_ref[:, 2 * dc:3 * dc], preferred_element_type=F32)
    z = cg * hv
    zext_ref[CONV_HIST:CONV_HIST + tb, :] = z
    conv = cw_ref[CONV_WIDTH - 1:CONV_WIDTH, :] * z
    for s in range(1, CONV_WIDTH):
        conv = conv + cw_ref[CONV_WIDTH - 1 - s:CONV_WIDTH - s, :] * zext_ref[pl.ds(CONV_HIST - s, tb), :]
    bg = jnp.dot(h, win_ref[:, 0:dc], preferred_element_type=F32)
    mixed_ref[...] = (bg * conv).astype(BF16)

    @pl.when(j == last)
    def _():
        cstate_ref[...] = zext_ref[pl.ds(CONV_HIST + tb - CONV_BUF, CONV_BUF), :]

    zext_ref[0:CONV_HIST, :] = zext_ref[tb:tb + CONV_HIST, :]


def odd_mix_prompt(x, g, w_in, conv_w, *, tb):
    b, t, d = x.shape
    dc = conv_w.shape[1]
    kern = functools.partial(_odd_prompt_kernel, tb=tb)
    return pl.pallas_call(
        kern,
        grid=(b, t // tb),
        in_specs=[pl.BlockSpec((None, tb, d), lambda i, j: (i, j, 0)),
                  _resident((1, d)), _resident((d, 3 * dc)), _resident((CONV_WIDTH, dc))],
        out_specs=[pl.BlockSpec((None, tb, dc), lambda i, j: (i, j, 0)),
                   pl.BlockSpec((None, CONV_BUF, dc), lambda i, j: (i, 0, 0))],
        out_shape=[jax.ShapeDtypeStruct((b, t, dc), BF16),
                   jax.ShapeDtypeStruct((b, CONV_BUF, dc), F32)],
        scratch_shapes=[pltpu.VMEM((CONV_HIST + tb, dc), F32)],
        compiler_params=pltpu.CompilerParams(dimension_semantics=("arbitrary", "arbitrary"),
                                             vmem_limit_bytes=VMEM_LIMIT_V7X),
        name="odd_mix_prompt",
    )(x, g, w_in, conv_w)


def _odd_sample_kernel(x_ref, g_ref, win_ref, cw_ref, cst_ref, mixed_ref, ncst_ref):
    dc = cw_ref.shape[1]
    h = _rms(x_ref[...], g_ref[...]).astype(BF16)
    cg = jnp.dot(h, win_ref[:, dc:2 * dc], preferred_element_type=F32)
    hv = jnp.dot(h, win_ref[:, 2 * dc:3 * dc], preferred_element_type=F32)
    z = cg * hv
    conv = cw_ref[CONV_WIDTH - 1:CONV_WIDTH, :] * z
    for s in range(1, CONV_WIDTH):
        r0 = (CONV_BUF - s) * dc
        conv = conv + cw_ref[CONV_WIDTH - 1 - s:CONV_WIDTH - s, :] * cst_ref[:, r0:r0 + dc]
    bg = jnp.dot(h, win_ref[:, 0:dc], preferred_element_type=F32)
    mixed_ref[...] = (bg * conv).astype(BF16)
    ncst_ref[:, 0:(CONV_BUF - 1) * dc] = cst_ref[:, dc:CONV_BUF * dc]
    ncst_ref[:, (CONV_BUF - 1) * dc:CONV_BUF * dc] = z


def odd_mix_sample(x, g, w_in, conv_w, cstate_flat):
    b, d = x.shape
    dc = conv_w.shape[1]
    return pl.pallas_call(
        _odd_sample_kernel,
        grid=(1,),
        in_specs=[_resident((b, d)), _resident((1, d)), _resident((d, 3 * dc)),
                  _resident((CONV_WIDTH, dc)), _resident((b, CONV_BUF * dc))],
        out_specs=[pl.BlockSpec((b, dc), lambda i: (0, 0)),
                   pl.BlockSpec((b, CONV_BUF * dc), lambda i: (0, 0))],
        out_shape=[jax.ShapeDtypeStruct((b, dc), BF16),
                   jax.ShapeDtypeStruct((b, CONV_BUF * dc), F32)],
        compiler_params=pltpu.CompilerParams(dimension_semantics=("arbitrary",),
                                             vmem_limit_bytes=VMEM_LIMIT_V7X),
        name="odd_mix_sample",
    )(x, g, w_in, conv_w, cstate_flat)


def _block_diag(w):
    g, c, _ = w.shape
    rows = [jnp.pad(w[i], ((0, 0), (i * c, (g - 1 - i) * c))) for i in range(g)]
    return jnp.concatenate(rows, axis=0)


def _trunk(x, states, pos0, p, *, prompt):
    depth = p["norm_mix"].shape[0]
    if prompt:
        b, t, d = x.shape
        xf = x.reshape(b * t, d)
        bm = 512
    else:
        b, d = x.shape
        xf = x
        bm = b
    new_pool, new_hgrn, new_conv = [], [], []
    for l in range(depth):
        g_mix = p["norm_mix"][l][None]
        final = l == depth - 1
        if l % 2 == 0:
            e = l // 2
            if prompt:
                mixed, p_new, s_new = even_mix_prompt(
                    xf.reshape(b, t, d), g_mix, p["even_w_in"][e], p["pool_bd"][e], p["pool_scale"][e][None],
                    p["hgrn_lb_logits"], p["hgrn_gain"][e][None], layer=l, tb=256)
                mixed = mixed.reshape(b * t, d)
            else:
                pool_st, hgrn_st, _ = states
                proj = norm_proj(xf, g_mix, p["even_w_in"][e], bm=bm)
                mixed, p_new, s_new = even_mix_sample(
                    proj, pool_st[e].reshape(b, -1), hgrn_st[e], p["pool_bd"][e], p["pool_scale"][e][None],
                    p["hgrn_lb_logits"], p["hgrn_gain"][e][None], layer=l, pos0=pos0, bb=8)
                p_new = p_new.reshape(b, POOL_BUF, -1)
            new_pool.append(p_new)
            new_hgrn.append(s_new)
            w_out = p["even_w_out"][e]
        else:
            o = l // 2
            if prompt:
                mixed, c_new = odd_mix_prompt(xf.reshape(b, t, d), g_mix, p["odd_w_in"][o], p["conv_w"][o], tb=512)
                mixed = mixed.reshape(b * t, d)
            else:
                conv_st = states[2]
                mixed, c_new = odd_mix_sample(xf, g_mix, p["odd_w_in"][o], p["conv_w"][o],
                                              conv_st[o].reshape(b, -1))
                c_new = c_new.reshape(b, CONV_BUF, -1)
            new_conv.append(c_new)
            w_out = p["odd_w_out"][o]
        xf = out_mlp(mixed, xf, w_out, p["norm_mlp"][l][None], p["ff_w1"][l], p["ff_w2"][l],
                     p["norm_final"][None], final_norm=final, bm=bm)
    y = xf.reshape(x.shape) if prompt else xf
    return y, jnp.stack(new_pool), jnp.stack(new_hgrn), jnp.stack(new_conv)


def kernel(x_prompt, x_sample, state_pool, state_hgrn, state_conv, norm_mix, norm_mlp, norm_final, even_w_in, pool_w, pool_scale, hgrn_lb_logits, hgrn_gain, even_w_out, odd_w_in, conv_w, odd_w_out, ff_w1, ff_w2):
    p = dict(norm_mix=norm_mix, norm_mlp=norm_mlp, norm_final=norm_final,
             even_w_in=even_w_in.astype(BF16), pool_bd=[_block_diag(w).astype(BF16) for w in pool_w],
             pool_scale=pool_scale, hgrn_lb_logits=hgrn_lb_logits, hgrn_gain=hgrn_gain,
             even_w_out=even_w_out, odd_w_in=odd_w_in, conv_w=conv_w,
             odd_w_out=odd_w_out, ff_w1=ff_w1, ff_w2=ff_w2)
    y_p, pool_p, hgrn_p, conv_p = _trunk(x_prompt, None, 0, p, prompt=True)
    db, ds, d = x_sample.shape
    y_s, pool_s, hgrn_s, conv_s = _trunk(x_sample.reshape(db * ds, d), (state_pool, state_hgrn, state_conv),
                                         PAST_LEN, p, prompt=False)
    return (y_p, y_s.reshape(db, ds, d), pool_p, hgrn_p, conv_p, pool_s, hgrn_s, conv_s)
```

```python
import functools

import jax
import jax.numpy as jnp
from jax import lax
from jax.experimental import pallas as pl
from jax.experimental.pallas import tpu as pltpu

F32 = jnp.float32
BF16 = jnp.bfloat16

EPS = 1e-6
PAST_LEN = 16384
POOL_WINDOWS = (2, 4, 8, 16)
POOL_GROUP_DIM = 64
POOL_BUF = max(POOL_WINDOWS) - 1
POOL_HIST = 16
POOL_PAD = 8
LANES = 128
HGRN_CHUNK = 32
HGRN_BLOCK = 128
HEAD = 128
CONV_WIDTH = 3
CONV_BUF = CONV_WIDTH - 1
CONV_HIST = 8

VMEM_LIMIT_V7X = 56 * 1024 * 1024
MXU_COLS_V7X = 256


def _resident(shape):
    nd = len(shape)
    return pl.BlockSpec(shape, lambda *_: (0,) * nd, pipeline_mode=pl.Buffered(1))


def _layer_block(stacked, layer):
    nd = stacked.ndim
    return pl.BlockSpec((None,) + stacked.shape[1:], lambda *_: (layer,) + (0,) * (nd - 1),
                        pipeline_mode=pl.Buffered(1))


def _rms(x, g):
    return x * lax.rsqrt(jnp.mean(x * x, axis=-1, keepdims=True) + EPS) * g


def _sigmoid(x):
    return 1.0 / (1.0 + jnp.exp(-x))


def _lower_bound(lbl, layer):
    e = jnp.exp(lbl - jnp.max(lbl, axis=0, keepdims=True))
    p = e / jnp.sum(e, axis=0, keepdims=True)
    return jnp.sum(p[0:layer + 1], axis=0, keepdims=True)


def _pool_select(snaps, u_shape):
    grp = lax.broadcasted_iota(jnp.int32, u_shape, 1) // POOL_GROUP_DIM
    s = snaps[POOL_WINDOWS[-1]]
    for g in range(len(POOL_WINDOWS) - 2, -1, -1):
        s = jnp.where(grp == g, snaps[POOL_WINDOWS[g]], s)
    win = jnp.left_shift(2, grp)
    return s, win


def _norm_proj_kernel(x_ref, g_ref, w_ref, o_ref):
    h = _rms(x_ref[...], g_ref[...]).astype(BF16)
    o_ref[...] = jnp.dot(h, w_ref[...], preferred_element_type=F32)


def norm_proj(x, g, w, *, bm):
    m, d = x.shape
    n = w.shape[1]
    return pl.pallas_call(
        _norm_proj_kernel,
        grid=(m // bm,),
        in_specs=[pl.BlockSpec((bm, d), lambda i: (i, 0)), _resident((1, d)), _resident((d, n))],
        out_specs=pl.BlockSpec((bm, n), lambda i: (i, 0)),
        out_shape=jax.ShapeDtypeStruct((m, n), F32),
        compiler_params=pltpu.CompilerParams(dimension_semantics=("arbitrary",),
                                             vmem_limit_bytes=VMEM_LIMIT_V7X),
        name="norm_proj",
    )(x, g, w)


def _out_mlp_kernel(m_ref, x_ref, ms_ref, xs_ref, wout_ref, g_ref, w1_ref, w2_ref, gf_ref, o_ref, os_ref,
                    *, final_norm, ff_chunk):
    d_ff = w1_ref.shape[1]

    def block(mixed, x):
        y = x + jnp.dot(mixed, wout_ref[...], preferred_element_type=F32)
        h = _rms(y, g_ref[...]).astype(BF16)
        acc = y
        for c in range(d_ff // ff_chunk):
            a = jnp.dot(h, w1_ref[:, c * ff_chunk:(c + 1) * ff_chunk], preferred_element_type=F32)
            a = jnp.square(jnp.maximum(a, 0.0)).astype(BF16)
            acc = acc + jnp.dot(a, w2_ref[c * ff_chunk:(c + 1) * ff_chunk, :], preferred_element_type=F32)
        return _rms(acc, gf_ref[...]) if final_norm else acc

    o_ref[...] = block(m_ref[...], x_ref[...])

    @pl.when(pl.program_id(0) == pl.num_programs(0) - 1)
    def _():
        os_ref[...] = block(ms_ref[...], xs_ref[...])


def out_mlp(mixed, x, mixed_s, x_s, w_out, g_mlp, w1, w2, g_final, *, layer, final_norm, bm):
    m, d = x.shape
    ms = x_s.shape[0]
    kern = functools.partial(_out_mlp_kernel, final_norm=final_norm, ff_chunk=1024)
    return pl.pallas_call(
        kern,
        grid=(m // bm,),
        in_specs=[pl.BlockSpec((bm, d), lambda i: (i, 0)), pl.BlockSpec((bm, d), lambda i: (i, 0)),
                  _resident((ms, d)), _resident((ms, d)),
                  _resident((d, d)), _layer_block(g_mlp, layer), _layer_block(w1, layer),
                  _layer_block(w2, layer), _resident((1, d))],
        out_specs=[pl.BlockSpec((bm, d), lambda i: (i, 0)), pl.BlockSpec((ms, d), lambda i: (0, 0))],
        out_shape=[jax.ShapeDtypeStruct((m, d), F32), jax.ShapeDtypeStruct((ms, d), F32)],
        compiler_params=pltpu.CompilerParams(dimension_semantics=("arbitrary",),
                                             vmem_limit_bytes=VMEM_LIMIT_V7X),
        name="out_mlp",
    )(mixed, x, mixed_s, x_s, w_out, g_mlp, w1, w2, g_final)


def _even_prompt_kernel(xn_ref, x0_ref, g_ref, win_ref, wpool_ref, pscale_ref, lbl_ref, gain_ref,
                        mixed_ref, pstate_ref, hstate_ref,
                        pu_ref, pq_ref, pf_ref, pi_ref, pg_ref, hn_ref, ext_ref, lvl_ref, st_ref, k_ref, lsplit_ref,
                        bcum_ref, dec_ref, gate_ref, v_ref, qd_ref, kd_ref, ke_ref, qb_ref, kb_ref, qr_ref, kc_ref,
                        *, layer, tb, tiles_per_seq):
    step = pl.program_id(0)
    d_pool = wpool_ref.shape[0]
    n_heads = st_ref.shape[0]
    d_hgrn = n_heads * HEAD
    nc = tb // HGRN_CHUNK
    nblk = tb // HGRN_BLOCK
    j = step % tiles_per_seq
    last = tiles_per_seq - 1
    o_q, o_f, o_i, o_g = d_pool, d_pool + d_hgrn, d_pool + 2 * d_hgrn, d_pool + 3 * d_hgrn
    sections = [(pu_ref, 0), (pq_ref, o_q), (pf_ref, o_f), (pi_ref, o_i), (pg_ref, o_g)]

    def chunks(dst_ref, col0):
        def make(c0, c1):
            def run():
                dst_ref[:, c0:c1] = jnp.dot(hn_ref[...], win_ref[:, col0 + c0:col0 + c1],
                                            preferred_element_type=F32)
            return run
        width = dst_ref.shape[1]
        return [make(c0, min(c0 + MXU_COLS_V7X, width)) for c0 in range(0, width, MXU_COLS_V7X)]

    @pl.when(step == 0)
    def _():
        hn_ref[...] = _rms(x0_ref[...], g_ref[...]).astype(BF16)
        for dst_ref, col0 in sections:
            for run in chunks(dst_ref, col0):
                run()

    r1_, r2_ = POOL_PAD + POOL_HIST, POOL_PAD + POOL_HIST + tb

    @pl.when(j == 0)
    def _():
        ext_ref[0:r1_, :] = jnp.zeros((r1_, d_pool), F32)
        lvl_ref[:, 0:POOL_PAD, :] = jnp.zeros((lvl_ref.shape[0], POOL_PAD, LANES), F32)
        st_ref[...] = jnp.zeros(st_ref.shape, F32)

    hn_ref[...] = _rms(xn_ref[...], g_ref[...]).astype(BF16)

    u = pu_ref[...]
    ext_ref[r1_:r2_, :] = u
    groups_per_tile = LANES // POOL_GROUP_DIM
    lane_grp = lax.broadcasted_iota(jnp.int32, (tb, LANES), 1) // POOL_GROUP_DIM
    pos = j * tb + lax.broadcasted_iota(jnp.int32, (tb, LANES), 0)
    means = []
    for lt in range(d_pool // LANES):
        wins = POOL_WINDOWS[lt * groups_per_tile:(lt + 1) * groups_per_tile]
        src, src_lanes, w, nbuf, got = ext_ref, slice(lt * LANES, (lt + 1) * LANES), 1, 0, {}
        while w < wins[-1]:
            new = src[POOL_PAD:r2_, src_lanes] + src[POOL_PAD - w:r2_ - w, src_lanes]
            w *= 2
            if w in wins:
                got[w] = new[POOL_HIST:]
            if w < wins[-1]:
                lvl_ref[nbuf, POOL_PAD:r2_, :] = new
                src, src_lanes, nbuf = lvl_ref.at[nbuf], slice(0, LANES), 1 - nbuf
        ssum, win = got[wins[-1]], jnp.full((tb, LANES), wins[-1], jnp.int32)
        for g_ in range(groups_per_tile - 2, -1, -1):
            ssum = jnp.where(lane_grp == g_, got[wins[g_]], ssum)
            win = jnp.where(lane_grp == g_, wins[g_], win)
        cnt = jnp.minimum(win, pos + 1).astype(F32)
        means.append(ssum / cnt)
    diff = (jnp.concatenate(means, axis=1) - u).astype(BF16)
    pool_out = jnp.dot(diff, wpool_ref[...], preferred_element_type=F32) * pscale_ref[...]
    mixed_ref[:, 0:d_pool] = pool_out.astype(BF16)

    @pl.when(j == last)
    def _():
        pstate_ref[...] = ext_ref[pl.ds(r2_ - POOL_BUF, POOL_BUF), :]

    ext_ref[POOL_PAD:r1_, :] = ext_ref[POOL_PAD + tb:r1_ + tb, :]
    for run in chunks(pu_ref, 0):
        run()

    lb = _lower_bound(lbl_ref[...], layer)
    for c in range(nc):
        rows = slice(c * HGRN_CHUNK, (c + 1) * HGRN_CHUNK)
        fz = pf_ref[rows, :]
        a = jnp.exp(-jnp.abs(fz))
        r = 1.0 / (1.0 + a)
        ar = a * r
        sig = jnp.where(fz >= 0, r, ar)
        sig_neg = jnp.where(fz >= 0, ar, r)
        logf = jnp.log(lb + (1.0 - lb) * sig)
        k_ref[rows, :] = (1.0 - lb) * sig_neg
        l_hi = logf.astype(BF16)
        r1 = logf - l_hi.astype(F32)
        l_mid = r1.astype(BF16)
        lsplit_ref[rows, 0:d_hgrn] = l_hi
        lsplit_ref[rows, d_hgrn:2 * d_hgrn] = l_mid
        lsplit_ref[rows, 2 * d_hgrn:3 * d_hgrn] = (r1 - l_mid.astype(F32)).astype(BF16)
        iz = pi_ref[rows, :]
        v_ref[rows, :] = (iz * _sigmoid(iz)).astype(BF16)
        gate_ref[rows, :] = gain_ref[...] * _sigmoid(pg_ref[rows, :])

    ri = lax.broadcasted_iota(jnp.int32, (tb, tb), 0)
    ci = lax.broadcasted_iota(jnp.int32, (tb, tb), 1)
    tri = ((ri // HGRN_CHUNK == ci // HGRN_CHUNK) & (ci <= ri)).astype(BF16)
    cs = jnp.dot(tri, lsplit_ref[...], preferred_element_type=F32)
    bcum_ref[...] = cs[:, 0:d_hgrn] + cs[:, d_hgrn:2 * d_hgrn] + cs[:, 2 * d_hgrn:3 * d_hgrn]
    for run in chunks(pf_ref, o_f) + chunks(pi_ref, o_i):
        run()

    nsub = HGRN_BLOCK // HGRN_CHUNK
    half = nsub // 2
    for b in range(nblk):
        blast = [bcum_ref[(b * nsub + i + 1) * HGRN_CHUNK - 1:(b * nsub + i + 1) * HGRN_CHUNK, :]
                 for i in range(nsub)]
        pre = [jnp.zeros_like(blast[0])]
        for i in range(nsub):
            pre.append(pre[-1] + blast[i])
        dec_ref[b:b + 1, :] = jnp.exp(pre[nsub])
        for i in range(nsub):
            c = b * nsub + i
            rows = slice(c * HGRN_CHUNK, (c + 1) * HGRN_CHUNK)
            bc = bcum_ref[rows, :]
            qd = pq_ref[rows, :] * jnp.exp(bc)
            kd = k_ref[rows, :] * jnp.exp(-bc)
            ke = kd * jnp.exp(blast[i])
            qd_ref[rows, :] = qd.astype(BF16)
            kd_ref[rows, :] = kd.astype(BF16)
            ke_ref[rows, :] = ke.astype(BF16)
            qb_ref[rows, :] = (qd * jnp.exp(pre[i])).astype(BF16)
            kb_ref[rows, :] = (ke * jnp.exp(pre[nsub] - pre[i + 1])).astype(BF16)
            if i >= half:
                qr_ref[rows, :] = (qd * jnp.exp(pre[i] - pre[half])).astype(BF16)
                kc_ref[rows, :] = jnp.zeros((HGRN_CHUNK, d_hgrn), BF16)
            else:
                kc_ref[rows, :] = (ke * jnp.exp(pre[half] - pre[i + 1])).astype(BF16)

    rb = lax.broadcasted_iota(jnp.int32, (HGRN_BLOCK, HGRN_BLOCK), 0)
    cb = lax.broadcasted_iota(jnp.int32, (HGRN_BLOCK, HGRN_BLOCK), 1)
    rsub, csub = rb // HGRN_CHUNK, cb // HGRN_CHUNK
    m_diag = (rsub == csub) & (cb <= rb)
    m_adj = (csub == rsub - 1) & (rsub != half)
    eye = rb == cb
    hb = HGRN_BLOCK // 2
    nt_dims = (((1,), (1,)), ((), ()))
    tn_dims = (((0,), (0,)), ((), ()))
    fillers = chunks(pq_ref, o_q) + chunks(pg_ref, o_g)

    def fill(n):
        for _ in range(min(n, len(fillers))):
            fillers.pop(0)()

    per_stage = -(-len(fillers) // (3 * nblk))
    for b in range(nblk):
        rows = slice(b * HGRN_BLOCK, (b + 1) * HGRN_BLOCK)
        far = slice(b * HGRN_BLOCK + hb, (b + 1) * HGRN_BLOCK)
        g12s, g3s = [], []
        for hd in range(n_heads):
            sl = slice(hd * HEAD, (hd + 1) * HEAD)
            kk = jnp.concatenate([kd_ref[rows, sl], ke_ref[rows, sl]], axis=0)
            g12s.append(lax.dot_general(qd_ref[rows, sl], kk, nt_dims, preferred_element_type=F32))
            g3s.append(lax.dot_general(qr_ref[far, sl], kc_ref[rows, sl], nt_dims, preferred_element_type=F32))
        fill(per_stage)
        outs = []
        for hd in range(n_heads):
            sl = slice(hd * HEAD, (hd + 1) * HEAD)
            g12 = g12s[hd]
            scores = jnp.where(m_diag, g12[:, 0:HGRN_BLOCK], jnp.where(m_adj, g12[:, HGRN_BLOCK:], 0.0))
            scores = jnp.concatenate([scores[0:hb], scores[hb:] + g3s[hd]], axis=0).astype(BF16)
            v_blk = v_ref[rows, sl]
            st = st_ref[hd]
            lhs = jnp.concatenate([scores, qb_ref[rows, sl]], axis=1)
            rhs = jnp.concatenate([v_blk, st.astype(BF16)], axis=0)
            outs.append(jnp.dot(lhs, rhs, preferred_element_type=F32))
            d_col = jnp.sum(jnp.where(eye, dec_ref[b:b + 1, sl], 0.0), axis=1, keepdims=True)
            st_ref[hd] = st * d_col + lax.dot_general(kb_ref[rows, sl], v_blk, tn_dims,
                                                      preferred_element_type=F32)
        fill(per_stage)
        for hd in range(n_heads):
            sl = slice(hd * HEAD, (hd + 1) * HEAD)
            o = outs[hd]
            o = o * lax.rsqrt(jnp.mean(o * o, axis=-1, keepdims=True) + EPS)
            mixed_ref[rows, d_pool + hd * HEAD:d_pool + (hd + 1) * HEAD] = (o * gate_ref[rows, sl]).astype(BF16)
        fill(per_stage)
    fill(len(fillers))

    @pl.when(j == last)
    def _():
        hstate_ref[...] = st_ref[...]


def even_mix_prompt(x, g, w_in, wpool_bd, pscale, lbl, gain, *, layer, tb):
    b, t, d = x.shape
    n_in = w_in.shape[1]
    d_pool = wpool_bd.shape[0]
    d_hgrn = gain.shape[1]
    n_heads = d_hgrn // HEAD
    tps = t // tb
    n_tiles = b * tps
    kern = functools.partial(_even_prompt_kernel, layer=layer, tb=tb, tiles_per_seq=tps)

    def next_tile(s):
        tile = jnp.minimum(s + 1, n_tiles - 1)
        return tile // tps, tile % tps

    assert d_pool % LANES == 0 and LANES % POOL_GROUP_DIM == 0 and tb % HGRN_BLOCK == 0
    return pl.pallas_call(
        kern,
        grid=(n_tiles,),
        in_specs=[pl.BlockSpec((None, tb, d), lambda s: (*next_tile(s), 0)),
                  pl.BlockSpec((None, tb, d), lambda s: (0, 0, 0)),
                  _resident((1, d)), _resident((d, n_in)), _resident((d_pool, d_pool)),
                  _resident((1, d_pool)), _resident(lbl.shape), _resident((1, d_hgrn))],
        out_specs=[pl.BlockSpec((None, tb, d), lambda s: (s // tps, s % tps, 0)),
                   pl.BlockSpec((None, POOL_BUF, d_pool), lambda s: (s // tps, 0, 0)),
                   pl.BlockSpec((None, n_heads, HEAD, HEAD), lambda s: (s // tps, 0, 0, 0))],
        out_shape=[jax.ShapeDtypeStruct((b, t, d), BF16),
                   jax.ShapeDtypeStruct((b, POOL_BUF, d_pool), F32),
                   jax.ShapeDtypeStruct((b, n_heads, HEAD, HEAD), F32)],
        scratch_shapes=[pltpu.VMEM((tb, d_pool), F32)]
                       + [pltpu.VMEM((tb, d_hgrn), F32)] * 4
                       + [pltpu.VMEM((tb, d), BF16),
                        pltpu.VMEM((POOL_PAD + POOL_HIST + tb, d_pool), F32),
                        pltpu.VMEM((2, POOL_PAD + POOL_HIST + tb, LANES), F32),
                        pltpu.VMEM((n_heads, HEAD, HEAD), F32),
                        pltpu.VMEM((tb, d_hgrn), F32),
                        pltpu.VMEM((tb, 3 * d_hgrn), BF16),
                        pltpu.VMEM((tb, d_hgrn), F32),
                        pltpu.VMEM((tb // HGRN_BLOCK, d_hgrn), F32),
                        pltpu.VMEM((tb, d_hgrn), F32)]
                       + [pltpu.VMEM((tb, d_hgrn), BF16)] * 8,
        compiler_params=pltpu.CompilerParams(dimension_semantics=("arbitrary",),
                                             vmem_limit_bytes=VMEM_LIMIT_V7X),
        name="even_mix_prompt",
    )(x, x, g, w_in, wpool_bd, pscale, lbl, gain)


def _even_sample_kernel(proj_ref, pool_ref, hst_ref, wpool_ref, pscale_ref, lbl_ref, gain_ref,
                        mixed_ref, npool_ref, nhst_ref, f_ref, k_ref, q_ref, v_ref, o_ref, *, layer, pos0):
    bb = proj_ref.shape[0]
    d_pool = wpool_ref.shape[0]
    n_heads = hst_ref.shape[1]
    d_hgrn = n_heads * HEAD

    u = proj_ref[:, 0:d_pool]
    acc = u
    snaps = {}
    for s in range(1, POOL_WINDOWS[-1]):
        r0 = (POOL_BUF - s) * d_pool
        acc = acc + pool_ref[:, r0:r0 + d_pool]
        if s + 1 in POOL_WINDOWS:
            snaps[s + 1] = acc
    ssum, win = _pool_select(snaps, u.shape)
    cnt = jnp.minimum(win, pos0 + 1).astype(F32)
    diff = (ssum / cnt - u).astype(BF16)
    pool_out = jnp.dot(diff, wpool_ref[...], preferred_element_type=F32) * pscale_ref[...]
    mixed_ref[:, 0:d_pool] = pool_out.astype(BF16)
    npool_ref[:, 0:(POOL_BUF - 1) * d_pool] = pool_ref[:, d_pool:POOL_BUF * d_pool]
    npool_ref[:, (POOL_BUF - 1) * d_pool:POOL_BUF * d_pool] = u

    lb = _lower_bound(lbl_ref[...], layer)
    fz = proj_ref[:, d_pool + d_hgrn:d_pool + 2 * d_hgrn]
    iz = proj_ref[:, d_pool + 2 * d_hgrn:d_pool + 3 * d_hgrn]
    f_ref[...] = lb + (1.0 - lb) * _sigmoid(fz)
    k_ref[...] = (1.0 - lb) * _sigmoid(-fz)
    q_ref[...] = proj_ref[:, d_pool:d_pool + d_hgrn]
    v_ref[...] = iz * _sigmoid(iz)
    eye = (lax.broadcasted_iota(jnp.int32, (HEAD, HEAD), 0)
           == lax.broadcasted_iota(jnp.int32, (HEAD, HEAD), 1))

    def to_column(row):
        return jnp.sum(jnp.where(eye, row, 0.0), axis=1, keepdims=True)

    for i in range(bb):
        for hd in range(n_heads):
            sl = slice(hd * HEAD, (hd + 1) * HEAD)
            f_col = to_column(f_ref[i:i + 1, sl])
            k_col = to_column(k_ref[i:i + 1, sl])
            q_col = to_column(q_ref[i:i + 1, sl])
            s_new = f_col * hst_ref[i, hd] + k_col * v_ref[i:i + 1, sl]
            nhst_ref[i, hd] = s_new
            o_ref[i:i + 1, sl] = jnp.sum(q_col * s_new, axis=0, keepdims=True)

    gz = proj_ref[:, d_pool + 3 * d_hgrn:d_pool + 4 * d_hgrn]
    gate = gain_ref[...] * _sigmoid(gz)
    for hd in range(n_heads):
        sl = slice(hd * HEAD, (hd + 1) * HEAD)
        o = o_ref[:, sl]
        o = o * lax.rsqrt(jnp.mean(o * o, axis=-1, keepdims=True) + EPS)
        mixed_ref[:, d_pool + hd * HEAD:d_pool + (hd + 1) * HEAD] = (o * gate[:, sl]).astype(BF16)


def even_mix_sample(proj, pool_flat, hstate, wpool_bd, pscale, lbl, gain, *, layer, pos0, bb):
    b, n_in = proj.shape
    d_pool = wpool_bd.shape[0]
    d_hgrn = gain.shape[1]
    n_heads = d_hgrn // HEAD
    d = d_pool + d_hgrn
    kern = functools.partial(_even_sample_kernel, layer=layer, pos0=pos0)
    return pl.pallas_call(
        kern,
        grid=(b // bb,),
        in_specs=[pl.BlockSpec((bb, n_in), lambda i: (i, 0)),
                  pl.BlockSpec((bb, POOL_BUF * d_pool), lambda i: (i, 0)),
                  pl.BlockSpec((bb, n_heads, HEAD, HEAD), lambda i: (i, 0, 0, 0)),
                  _resident((d_pool, d_pool)), _resident((1, d_pool)), _resident(lbl.shape),
                  _resident((1, d_hgrn))],
        out_specs=[pl.BlockSpec((bb, d), lambda i: (i, 0)),
                   pl.BlockSpec((bb, POOL_BUF * d_pool), lambda i: (i, 0)),
                   pl.BlockSpec((bb, n_heads, HEAD, HEAD), lambda i: (i, 0, 0, 0))],
        out_shape=[jax.ShapeDtypeStruct((b, d), BF16),
                   jax.ShapeDtypeStruct((b, POOL_BUF * d_pool), F32),
                   jax.ShapeDtypeStruct((b, n_heads, HEAD, HEAD), F32)],
        scratch_shapes=[pltpu.VMEM((bb, d_hgrn), F32)] * 5,
        compiler_params=pltpu.CompilerParams(dimension_semantics=("arbitrary",),
                                             vmem_limit_bytes=VMEM_LIMIT_V7X),
        name="even_mix_sample",
    )(proj, pool_flat, hstate, wpool_bd, pscale, lbl, gain)


def _odd_prompt_kernel(x_ref, g_ref, win_ref, cw_ref, mixed_ref, cstate_ref, zext_ref, *, tb):
    j = pl.program_id(1)
    last = pl.num_programs(1) - 1
    dc = cw_ref.shape[1]

    @pl.when(j == 0)
    def _():
        zext_ref[0:CONV_HIST, :] = jnp.zeros((CONV_HIST, dc), F32)

    h = _rms(x_ref[...], g_ref[...]).astype(BF16)
    cg = jnp.dot(h, win_ref[:, dc:2 * dc], preferred_element_type=F32)
    hv = jnp.dot(h, win_ref[:, 2 * dc:3 * dc], preferred_element_type=F32)
    z = cg * hv
    zext_ref[CONV_HIST:CONV_HIST + tb, :] = z
    conv = cw_ref[CONV_WIDTH - 1:CONV_WIDTH, :] * z
    for s in range(1, CONV_WIDTH):
        conv = conv + cw_ref[CONV_WIDTH - 1 - s:CONV_WIDTH - s, :] * zext_ref[pl.ds(CONV_HIST - s, tb), :]
    bg = jnp.dot(h, win_ref[:, 0:dc], preferred_element_type=F32)
    mixed_ref[...] = (bg * conv).astype(BF16)

    @pl.when(j == last)
    def _():
        cstate_ref[...] = zext_ref[pl.ds(CONV_HIST + tb - CONV_BUF, CONV_BUF), :]

    zext_ref[0:CONV_HIST, :] = zext_ref[tb:tb + CONV_HIST, :]


def odd_mix_prompt(x, g, w_in, conv_w, *, tb):
    b, t, d = x.shape
    dc = conv_w.shape[1]
    kern = functools.partial(_odd_prompt_kernel, tb=tb)
    return pl.pallas_call(
        kern,
        grid=(b, t // tb),
        in_specs=[pl.BlockSpec((None, tb, d), lambda i, j: (i, j, 0)),
                  _resident((1, d)), _resident((d, 3 * dc)), _resident((CONV_WIDTH, dc))],
        out_specs=[pl.BlockSpec((None, tb, dc), lambda i, j: (i, j, 0)),
                   pl.BlockSpec((None, CONV_BUF, dc), lambda i, j: (i, 0, 0))],
        out_shape=[jax.ShapeDtypeStruct((b, t, dc), BF16),
                   jax.ShapeDtypeStruct((b, CONV_BUF, dc), F32)],
        scratch_shapes=[pltpu.VMEM((CONV_HIST + tb, dc), F32)],
        compiler_params=pltpu.CompilerParams(dimension_semantics=("arbitrary", "arbitrary"),
                                             vmem_limit_bytes=VMEM_LIMIT_V7X),
        name="odd_mix_prompt",
    )(x, g, w_in, conv_w)


def _odd_sample_kernel(x_ref, g_ref, win_ref, cw_ref, cst_ref, mixed_ref, ncst_ref):
    dc = cw_ref.shape[1]
    h = _rms(x_ref[...], g_ref[...]).astype(BF16)
    cg = jnp.dot(h, win_ref[:, dc:2 * dc], preferred_element_type=F32)
    hv = jnp.dot(h, win_ref[:, 2 * dc:3 * dc], preferred_element_type=F32)
    z = cg * hv
    conv = cw_ref[CONV_WIDTH - 1:CONV_WIDTH, :] * z
    for s in range(1, CONV_WIDTH):
        r0 = (CONV_BUF - s) * dc
        conv = conv + cw_ref[CONV_WIDTH - 1 - s:CONV_WIDTH - s, :] * cst_ref[:, r0:r0 + dc]
    bg = jnp.dot(h, win_ref[:, 0:dc], preferred_element_type=F32)
    mixed_ref[...] = (bg * conv).astype(BF16)
    ncst_ref[:, 0:(CONV_BUF - 1) * dc] = cst_ref[:, dc:CONV_BUF * dc]
    ncst_ref[:, (CONV_BUF - 1) * dc:CONV_BUF * dc] = z


def odd_mix_sample(x, g, w_in, conv_w, cstate_flat):
    b, d = x.shape
    dc = conv_w.shape[1]
    return pl.pallas_call(
        _odd_sample_kernel,
        grid=(1,),
        in_specs=[_resident((b, d)), _resident((1, d)), _resident((d, 3 * dc)),
                  _resident((CONV_WIDTH, dc)), _resident((b, CONV_BUF * dc))],
        out_specs=[pl.BlockSpec((b, dc), lambda i: (0, 0)),
                   pl.BlockSpec((b, CONV_BUF * dc), lambda i: (0, 0))],
        out_shape=[jax.ShapeDtypeStruct((b, dc), BF16),
                   jax.ShapeDtypeStruct((b, CONV_BUF * dc), F32)],
        compiler_params=pltpu.CompilerParams(dimension_semantics=("arbitrary",),
                                             vmem_limit_bytes=VMEM_LIMIT_V7X),
        name="odd_mix_sample",
    )(x, g, w_in, conv_w, cstate_flat)


def _block_diag(w):
    g, c, _ = w.shape
    rows = [jnp.pad(w[i], ((0, 0), (i * c, (g - 1 - i) * c))) for i in range(g)]
    return jnp.concatenate(rows, axis=0)


def kernel(x_prompt, x_sample, state_pool, state_hgrn, state_conv, norm_mix, norm_mlp, norm_final, even_w_in, pool_w, pool_scale, hgrn_lb_logits, hgrn_gain, even_w_out, odd_w_in, conv_w, odd_w_out, ff_w1, ff_w2):
    depth = norm_mix.shape[0]
    b, t, d = x_prompt.shape
    db, ds, _ = x_sample.shape
    assert ds == 1, "the sample group carries one token per sequence"
    xp = x_prompt.reshape(b * t, d)
    xs = x_sample.reshape(db, d)
    g_mlp = norm_mlp[:, None, :]
    pool_p, hgrn_p, conv_p, pool_s, hgrn_s, conv_s = [], [], [], [], [], []
    for l in range(depth):
        g_mix = norm_mix[l][None]
        if l % 2 == 0:
            e = l // 2
            w_in = even_w_in[e].astype(BF16)
            pool_bd = _block_diag(pool_w[e]).astype(BF16)
            shared = (pool_bd, pool_scale[e][None], hgrn_lb_logits, hgrn_gain[e][None])
            mixed_p, p_new, s_new = even_mix_prompt(xp.reshape(b, t, d), g_mix, w_in, *shared, layer=l, tb=256)
            pool_p.append(p_new)
            hgrn_p.append(s_new)
            proj_s = norm_proj(xs, g_mix, w_in, bm=db)
            mixed_s, p_new, s_new = even_mix_sample(proj_s, state_pool[e].reshape(db, -1), state_hgrn[e], *shared,
                                                    layer=l, pos0=PAST_LEN, bb=8)
            pool_s.append(p_new.reshape(db, POOL_BUF, -1))
            hgrn_s.append(s_new)
            w_out = even_w_out[e]
        else:
            o = l // 2
            mixed_p, c_new = odd_mix_prompt(xp.reshape(b, t, d), g_mix, odd_w_in[o], conv_w[o], tb=512)
            conv_p.append(c_new)
            mixed_s, c_new = odd_mix_sample(xs, g_mix, odd_w_in[o], conv_w[o], state_conv[o].reshape(db, -1))
            conv_s.append(c_new.reshape(db, CONV_BUF, -1))
            w_out = odd_w_out[o]
        xp, xs = out_mlp(mixed_p.reshape(b * t, d), xp, mixed_s, xs, w_out, g_mlp, ff_w1, ff_w2, norm_final[None],
                         layer=l, final_norm=l == depth - 1, bm=512)
    return (xp.reshape(b, t, d), xs.reshape(db, ds, d), jnp.stack(pool_p), jnp.stack(hgrn_p), jnp.stack(conv_p),
            jnp.stack(pool_s), jnp.stack(hgrn_s), jnp.stack(conv_s))
```

```python
import functools

import jax
import jax.numpy as jnp
from jax import lax
from jax.experimental import pallas as pl
from jax.experimental.pallas import tpu as pltpu

F32 = jnp.float32
BF16 = jnp.bfloat16

EPS = 1e-6
PAST_LEN = 16384
POOL_WINDOWS = (2, 4, 8, 16)
POOL_GROUP_DIM = 64
POOL_BUF = max(POOL_WINDOWS) - 1
POOL_HIST = 16
POOL_PAD = 8
LANES = 128
HGRN_CHUNK = 32
HGRN_BLOCK = 128
HEAD = 128
CONV_WIDTH = 3
CONV_BUF = CONV_WIDTH - 1
CONV_HIST = 8

VMEM_LIMIT_V7X = 56 * 1024 * 1024
MXU_COLS_V7X = 256


def _resident(shape):
    nd = len(shape)
    return pl.BlockSpec(shape, lambda *_: (0,) * nd, pipeline_mode=pl.Buffered(1))


def _layer_block(stacked, layer):
    nd = stacked.ndim
    return pl.BlockSpec((None,) + stacked.shape[1:], lambda *_: (layer,) + (0,) * (nd - 1),
                        pipeline_mode=pl.Buffered(1))


def _rms(x, g):
    return x * lax.rsqrt(jnp.mean(x * x, axis=-1, keepdims=True) + EPS) * g


def _sigmoid(x):
    return 1.0 / (1.0 + jnp.exp(-x))


def _lower_bound(lbl, layer):
    e = jnp.exp(lbl - jnp.max(lbl, axis=0, keepdims=True))
    p = e / jnp.sum(e, axis=0, keepdims=True)
    return jnp.sum(p[0:layer + 1], axis=0, keepdims=True)


def _pool_select(snaps, u_shape):
    grp = lax.broadcasted_iota(jnp.int32, u_shape, 1) // POOL_GROUP_DIM
    s = snaps[POOL_WINDOWS[-1]]
    for g in range(len(POOL_WINDOWS) - 2, -1, -1):
        s = jnp.where(grp == g, snaps[POOL_WINDOWS[g]], s)
    win = jnp.left_shift(2, grp)
    return s, win


def _norm_proj_kernel(x_ref, g_ref, w_ref, o_ref):
    h = _rms(x_ref[...], g_ref[...]).astype(BF16)
    o_ref[...] = jnp.dot(h, w_ref[...], preferred_element_type=F32)


def norm_proj(x, g, w, *, bm):
    m, d = x.shape
    n = w.shape[1]
    return pl.pallas_call(
        _norm_proj_kernel,
        grid=(m // bm,),
        in_specs=[pl.BlockSpec((bm, d), lambda i: (i, 0)), _resident((1, d)), _resident((d, n))],
        out_specs=pl.BlockSpec((bm, n), lambda i: (i, 0)),
        out_shape=jax.ShapeDtypeStruct((m, n), F32),
        compiler_params=pltpu.CompilerParams(dimension_semantics=("arbitrary",),
                                             vmem_limit_bytes=VMEM_LIMIT_V7X),
        name="norm_proj",
    )(x, g, w)


def _out_mlp_kernel(m_ref, x_ref, ms_ref, xs_ref, wout_ref, g_ref, w1_ref, w2_ref, gf_ref, o_ref, os_ref,
                    *, final_norm, ff_chunk):
    d_ff = w1_ref.shape[1]

    def block(mixed, x):
        y = x + jnp.dot(mixed, wout_ref[...], preferred_element_type=F32)
        h = _rms(y, g_ref[...]).astype(BF16)
        acc = y
        for c in range(d_ff // ff_chunk):
            a = jnp.dot(h, w1_ref[:, c * ff_chunk:(c + 1) * ff_chunk], preferred_element_type=F32)
            a = jnp.square(jnp.maximum(a, 0.0)).astype(BF16)
            acc = acc + jnp.dot(a, w2_ref[c * ff_chunk:(c + 1) * ff_chunk, :], preferred_element_type=F32)
        return _rms(acc, gf_ref[...]) if final_norm else acc

    o_ref[...] = block(m_ref[...], x_ref[...])

    @pl.when(pl.program_id(0) == pl.num_programs(0) - 1)
    def _():
        os_ref[...] = block(ms_ref[...], xs_ref[...])


def out_mlp(mixed, x, mixed_s, x_s, w_out, g_mlp, w1, w2, g_final, *, layer, final_norm, bm):
    m, d = x.shape
    ms = x_s.shape[0]
    kern = functools.partial(_out_mlp_kernel, final_norm=final_norm, ff_chunk=1024)
    return pl.pallas_call(
        kern,
        grid=(m // bm,),
        in_specs=[pl.BlockSpec((bm, d), lambda i: (i, 0)), pl.BlockSpec((bm, d), lambda i: (i, 0)),
                  _resident((ms, d)), _resident((ms, d)),
                  _resident((d, d)), _layer_block(g_mlp, layer), _layer_block(w1, layer),
                  _layer_block(w2, layer), _resident((1, d))],
        out_specs=[pl.BlockSpec((bm, d), lambda i: (i, 0)), pl.BlockSpec((ms, d), lambda i: (0, 0))],
        out_shape=[jax.ShapeDtypeStruct((m, d), F32), jax.ShapeDtypeStruct((ms, d), F32)],
        compiler_params=pltpu.CompilerParams(dimension_semantics=("arbitrary",),
                                             vmem_limit_bytes=VMEM_LIMIT_V7X),
        name="out_mlp",
    )(mixed, x, mixed_s, x_s, w_out, g_mlp, w1, w2, g_final)


def _even_prompt_kernel(xn_ref, x0_ref, g_ref, win_ref, wpool_ref, pscale_ref, lbl_ref, gain_ref,
                        mixed_ref, pstate_ref, hstate_ref,
                        pu_ref, pq_ref, pf_ref, pi_ref, pg_ref, hn_ref, ext_ref, lvl_ref, st_ref, k_ref, lsplit_ref,
                        bcum_ref, dec_ref, gate_ref, v_ref, qd_ref, kd_ref, ke_ref, qb_ref, kb_ref, qr_ref, kc_ref,
                        *, layer, tb, tiles_per_seq):
    step = pl.program_id(0)
    d_pool = wpool_ref.shape[0]
    n_heads = st_ref.shape[0]
    d_hgrn = n_heads * HEAD
    nc = tb // HGRN_CHUNK
    nblk = tb // HGRN_BLOCK
    j = step % tiles_per_seq
    last = tiles_per_seq - 1
    o_q, o_f, o_i, o_g = d_pool, d_pool + d_hgrn, d_pool + 2 * d_hgrn, d_pool + 3 * d_hgrn
    sections = [(pu_ref, 0), (pq_ref, o_q), (pf_ref, o_f), (pi_ref, o_i), (pg_ref, o_g)]

    def chunks(dst_ref, col0):
        def make(c0, c1):
            def run():
                dst_ref[:, c0:c1] = jnp.dot(hn_ref[...], win_ref[:, col0 + c0:col0 + c1],
                                            preferred_element_type=F32)
            return run
        width = dst_ref.shape[1]
        return [make(c0, min(c0 + MXU_COLS_V7X, width)) for c0 in range(0, width, MXU_COLS_V7X)]

    @pl.when(step == 0)
    def _():
        hn_ref[...] = _rms(x0_ref[...], g_ref[...]).astype(BF16)
        for dst_ref, col0 in sections:
            for run in chunks(dst_ref, col0):
                run()

    r1_, r2_ = POOL_PAD + POOL_HIST, POOL_PAD + POOL_HIST + tb

    @pl.when(j == 0)
    def _():
        ext_ref[0:r1_, :] = jnp.zeros((r1_, d_pool), F32)
        lvl_ref[:, 0:POOL_PAD, :] = jnp.zeros((lvl_ref.shape[0], POOL_PAD, LANES), F32)
        st_ref[...] = jnp.zeros(st_ref.shape, F32)

    hn_ref[...] = _rms(xn_ref[...], g_ref[...]).astype(BF16)

    u = pu_ref[...]
    ext_ref[r1_:r2_, :] = u
    groups_per_tile = LANES // POOL_GROUP_DIM
    lane_grp = lax.broadcasted_iota(jnp.int32, (tb, LANES), 1) // POOL_GROUP_DIM
    pos = j * tb + lax.broadcasted_iota(jnp.int32, (tb, LANES), 0)
    means = []
    for lt in range(d_pool // LANES):
        wins = POOL_WINDOWS[lt * groups_per_tile:(lt + 1) * groups_per_tile]
        src, src_lanes, w, nbuf, got = ext_ref, slice(lt * LANES, (lt + 1) * LANES), 1, 0, {}
        while w < wins[-1]:
            new = src[POOL_PAD:r2_, src_lanes] + src[POOL_PAD - w:r2_ - w, src_lanes]
            w *= 2
            if w in wins:
                got[w] = new[POOL_HIST:]
            if w < wins[-1]:
                lvl_ref[nbuf, POOL_PAD:r2_, :] = new
                src, src_lanes, nbuf = lvl_ref.at[nbuf], slice(0, LANES), 1 - nbuf
        ssum, win = got[wins[-1]], jnp.full((tb, LANES), wins[-1], jnp.int32)
        for g_ in range(groups_per_tile - 2, -1, -1):
            ssum = jnp.where(lane_grp == g_, got[wins[g_]], ssum)
            win = jnp.where(lane_grp == g_, wins[g_], win)
        cnt = jnp.minimum(win, pos + 1).astype(F32)
        means.append(ssum / cnt)
    diff = (jnp.concatenate(means, axis=1) - u).astype(BF16)
    pool_out = jnp.dot(diff, wpool_ref[...], preferred_element_type=F32) * pscale_ref[...]
    mixed_ref[:, 0:d_pool] = pool_out.astype(BF16)

    @pl.when(j == last)
    def _():
        pstate_ref[...] = ext_ref[pl.ds(r2_ - POOL_BUF, POOL_BUF), :]

    ext_ref[POOL_PAD:r1_, :] = ext_ref[POOL_PAD + tb:r1_ + tb, :]
    for run in chunks(pu_ref, 0):
        run()

    lb = _lower_bound(lbl_ref[...], layer)
    for c in range(nc):
        rows = slice(c * HGRN_CHUNK, (c + 1) * HGRN_CHUNK)
        fz = pf_ref[rows, :]
        a = jnp.exp(-jnp.abs(fz))
        r = 1.0 / (1.0 + a)
        ar = a * r
        sig = jnp.where(fz >= 0, r, ar)
        sig_neg = jnp.where(fz >= 0, ar, r)
        logf = jnp.log(lb + (1.0 - lb) * sig)
        k_ref[rows, :] = (1.0 - lb) * sig_neg
        l_hi = logf.astype(BF16)
        r1 = logf - l_hi.astype(F32)
        l_mid = r1.astype(BF16)
        lsplit_ref[rows, 0:d_hgrn] = l_hi
        lsplit_ref[rows, d_hgrn:2 * d_hgrn] = l_mid
        lsplit_ref[rows, 2 * d_hgrn:3 * d_hgrn] = (r1 - l_mid.astype(F32)).astype(BF16)
        iz = pi_ref[rows, :]
        v_ref[rows, :] = (iz * _sigmoid(iz)).astype(BF16)
        gate_ref[rows, :] = gain_ref[...] * _sigmoid(pg_ref[rows, :])

    ri = lax.broadcasted_iota(jnp.int32, (tb, tb), 0)
    ci = lax.broadcasted_iota(jnp.int32, (tb, tb), 1)
    tri = ((ri // HGRN_CHUNK == ci // HGRN_CHUNK) & (ci <= ri)).astype(BF16)
    cs = jnp.dot(tri, lsplit_ref[...], preferred_element_type=F32)
    bcum_ref[...] = cs[:, 0:d_hgrn] + cs[:, d_hgrn:2 * d_hgrn] + cs[:, 2 * d_hgrn:3 * d_hgrn]
    for run in chunks(pf_ref, o_f) + chunks(pi_ref, o_i):
        run()

    nsub = HGRN_BLOCK // HGRN_CHUNK
    half = nsub // 2
    for b in range(nblk):
        blast = [bcum_ref[(b * nsub + i + 1) * HGRN_CHUNK - 1:(b * nsub + i + 1) * HGRN_CHUNK, :]
                 for i in range(nsub)]
        pre = [jnp.zeros_like(blast[0])]
        for i in range(nsub):
            pre.append(pre[-1] + blast[i])
        dec_ref[b:b + 1, :] = jnp.exp(pre[nsub])
        for i in range(nsub):
            c = b * nsub + i
            rows = slice(c * HGRN_CHUNK, (c + 1) * HGRN_CHUNK)
            bc = bcum_ref[rows, :]
            qd = pq_ref[rows, :] * jnp.exp(bc)
            kd = k_ref[rows, :] * jnp.exp(-bc)
            ke = kd * jnp.exp(blast[i])
            qd_ref[rows, :] = qd.astype(BF16)
            kd_ref[rows, :] = kd.astype(BF16)
            ke_ref[rows, :] = ke.astype(BF16)
            qb_ref[rows, :] = (qd * jnp.exp(pre[i])).astype(BF16)
            kb_ref[rows, :] = (ke * jnp.exp(pre[nsub] - pre[i + 1])).astype(BF16)
            if i >= half:
                qr_ref[rows, :] = (qd * jnp.exp(pre[i] - pre[half])).astype(BF16)
                kc_ref[rows, :] = jnp.zeros((HGRN_CHUNK, d_hgrn), BF16)
            else:
                kc_ref[rows, :] = (ke * jnp.exp(pre[half] - pre[i + 1])).astype(BF16)

    rb = lax.broadcasted_iota(jnp.int32, (HGRN_BLOCK, HGRN_BLOCK), 0)
    cb = lax.broadcasted_iota(jnp.int32, (HGRN_BLOCK, HGRN_BLOCK), 1)
    rsub, csub = rb // HGRN_CHUNK, cb // HGRN_CHUNK
    m_diag = (rsub == csub) & (cb <= rb)
    m_adj = (csub == rsub - 1) & (rsub != half)
    eye = rb == cb
    hb = HGRN_BLOCK // 2
    nt_dims = (((1,), (1,)), ((), ()))
    tn_dims = (((0,), (0,)), ((), ()))
    fillers = chunks(pq_ref, o_q) + chunks(pg_ref, o_g)

    def fill(n):
        for _ in range(min(n, len(fillers))):
            fillers.pop(0)()

    per_stage = -(-len(fillers) // (3 * nblk))
    for b in range(nblk):
        rows = slice(b * HGRN_BLOCK, (b + 1) * HGRN_BLOCK)
        far = slice(b * HGRN_BLOCK + hb, (b + 1) * HGRN_BLOCK)
        g12s, g3s = [], []
        for hd in range(n_heads):
            sl = slice(hd * HEAD, (hd + 1) * HEAD)
            kk = jnp.concatenate([kd_ref[rows, sl], ke_ref[rows, sl]], axis=0)
            g12s.append(lax.dot_general(qd_ref[rows, sl], kk, nt_dims, preferred_element_type=F32))
            g3s.append(lax.dot_general(qr_ref[far, sl], kc_ref[rows, sl], nt_dims, preferred_element_type=F32))
        fill(per_stage)
        outs = []
        for hd in range(n_heads):
            sl = slice(hd * HEAD, (hd + 1) * HEAD)
            g12 = g12s[hd]
            scores = jnp.where(m_diag, g12[:, 0:HGRN_BLOCK], jnp.where(m_adj, g12[:, HGRN_BLOCK:], 0.0))
            scores = jnp.concatenate([scores[0:hb], scores[hb:] + g3s[hd]], axis=0).astype(BF16)
            v_blk = v_ref[rows, sl]
            st = st_ref[hd]
            lhs = jnp.concatenate([scores, qb_ref[rows, sl]], axis=1)
            rhs = jnp.concatenate([v_blk, st.astype(BF16)], axis=0)
            outs.append(jnp.dot(lhs, rhs, preferred_element_type=F32))
            d_col = jnp.sum(jnp.where(eye, dec_ref[b:b + 1, sl], 0.0), axis=1, keepdims=True)
            st_ref[hd] = st * d_col + lax.dot_general(kb_ref[rows, sl], v_blk, tn_dims,
                                                      preferred_element_type=F32)
        fill(per_stage)
        for hd in range(n_heads):
            sl = slice(hd * HEAD, (hd + 1) * HEAD)
            o = outs[hd]
            o = o * lax.rsqrt(jnp.mean(o * o, axis=-1, keepdims=True) + EPS)
            mixed_ref[rows, d_pool + hd * HEAD:d_pool + (hd + 1) * HEAD] = (o * gate_ref[rows, sl]).astype(BF16)
        fill(per_stage)
    fill(len(fillers))

    @pl.when(j == last)
    def _():
        hstate_ref[...] = st_ref[...]


def even_mix_prompt(x, g, w_in, wpool_bd, pscale, lbl, gain, *, layer, tb):
    b, t, d = x.shape
    n_in = w_in.shape[1]
    d_pool = wpool_bd.shape[0]
    d_hgrn = gain.shape[1]
    n_heads = d_hgrn // HEAD
    tps = t // tb
    n_tiles = b * tps
    kern = functools.partial(_even_prompt_kernel, layer=layer, tb=tb, tiles_per_seq=tps)

    def next_tile(s):
        tile = jnp.minimum(s + 1, n_tiles - 1)
        return tile // tps, tile % tps

    assert d_pool % LANES == 0 and LANES % POOL_GROUP_DIM == 0 and tb % HGRN_BLOCK == 0
    return pl.pallas_call(
        kern,
        grid=(n_tiles,),
        in_specs=[pl.BlockSpec((None, tb, d), lambda s: (*next_tile(s), 0)),
                  pl.BlockSpec((None, tb, d), lambda s: (0, 0, 0)),
                  _resident((1, d)), _resident((d, n_in)), _resident((d_pool, d_pool)),
                  _resident((1, d_pool)), _resident(lbl.shape), _resident((1, d_hgrn))],
        out_specs=[pl.BlockSpec((None, tb, d), lambda s: (s // tps, s % tps, 0)),
                   pl.BlockSpec((None, POOL_BUF, d_pool), lambda s: (s // tps, 0, 0)),
                   pl.BlockSpec((None, n_heads, HEAD, HEAD), lambda s: (s // tps, 0, 0, 0))],
        out_shape=[jax.ShapeDtypeStruct((b, t, d), BF16),
                   jax.ShapeDtypeStruct((b, POOL_BUF, d_pool), F32),
                   jax.ShapeDtypeStruct((b, n_heads, HEAD, HEAD), F32)],
        scratch_shapes=[pltpu.VMEM((tb, d_pool), F32)]
                       + [pltpu.VMEM((tb, d_hgrn), F32)] * 4
                       + [pltpu.VMEM((tb, d), BF16),
                        pltpu.VMEM((POOL_PAD + POOL_HIST + tb, d_pool), F32),
                        pltpu.VMEM((2, POOL_PAD + POOL_HIST + tb, LANES), F32),
                        pltpu.VMEM((n_heads, HEAD, HEAD), F32),
                        pltpu.VMEM((tb, d_hgrn), F32),
                        pltpu.VMEM((tb, 3 * d_hgrn), BF16),
                        pltpu.VMEM((tb, d_hgrn), F32),
                        pltpu.VMEM((tb // HGRN_BLOCK, d_hgrn), F32),
                        pltpu.VMEM((tb, d_hgrn), F32)]
                       + [pltpu.VMEM((tb, d_hgrn), BF16)] * 8,
        compiler_params=pltpu.CompilerParams(dimension_semantics=("arbitrary",),
                                             vmem_limit_bytes=VMEM_LIMIT_V7X),
        name="even_mix_prompt",
    )(x, x, g, w_in, wpool_bd, pscale, lbl, gain)


def _even_sample_kernel(proj_ref, pool_ref, hst_ref, wpool_ref, pscale_ref, lbl_ref, gain_ref,
                        mixed_ref, npool_ref, nhst_ref, v_ref, o_ref, *, layer, pos0):
    bb = proj_ref.shape[0]
    d_pool = wpool_ref.shape[0]
    n_heads = hst_ref.shape[1]
    d_hgrn = n_heads * HEAD

    u = proj_ref[:, 0:d_pool]
    acc = u
    snaps = {}
    for s in range(1, POOL_WINDOWS[-1]):
        r0 = (POOL_BUF - s) * d_pool
        acc = acc + pool_ref[:, r0:r0 + d_pool]
        if s + 1 in POOL_WINDOWS:
            snaps[s + 1] = acc
    ssum, win = _pool_select(snaps, u.shape)
    cnt = jnp.minimum(win, pos0 + 1).astype(F32)
    diff = (ssum / cnt - u).astype(BF16)
    pool_out = jnp.dot(diff, wpool_ref[...], preferred_element_type=F32) * pscale_ref[...]
    mixed_ref[:, 0:d_pool] = pool_out.astype(BF16)
    npool_ref[:, 0:(POOL_BUF - 1) * d_pool] = pool_ref[:, d_pool:POOL_BUF * d_pool]
    npool_ref[:, (POOL_BUF - 1) * d_pool:POOL_BUF * d_pool] = u

    lb = _lower_bound(lbl_ref[...], layer)
    fz = proj_ref[:, d_pool + d_hgrn:d_pool + 2 * d_hgrn]
    iz = proj_ref[:, d_pool + 2 * d_hgrn:d_pool + 3 * d_hgrn]
    v_ref[...] = iz * _sigmoid(iz)
    fkq = (lb + (1.0 - lb) * _sigmoid(fz), (1.0 - lb) * _sigmoid(-fz), proj_ref[:, d_pool:d_pool + d_hgrn])

    def split3(x):
        hi = x.astype(BF16).astype(F32)
        r1 = x - hi
        mid = r1.astype(BF16).astype(F32)
        return jnp.concatenate([hi, mid, r1 - mid], axis=0)

    sel_r = lax.broadcasted_iota(jnp.int32, (3 * bb, bb * HEAD), 0) % bb
    sel_c = lax.broadcasted_iota(jnp.int32, (3 * bb, bb * HEAD), 1) // HEAD
    sel = (sel_r == sel_c).astype(F32)
    parts = [split3(x) for x in fkq]
    for hd in range(n_heads):
        sl = slice(hd * HEAD, (hd + 1) * HEAD)
        f_cols, k_cols, q_cols = [lax.dot_general(p[:, sl], sel, (((0,), (0,)), ((), ())),
                                                  preferred_element_type=F32) for p in parts]
        for i in range(bb):
            blk = slice(i * HEAD, (i + 1) * HEAD)
            s_new = f_cols[:, blk] * hst_ref[i, hd] + k_cols[:, blk] * v_ref[i:i + 1, sl]
            nhst_ref[i, hd] = s_new
            o_ref[i:i + 1, sl] = jnp.sum(q_cols[:, blk] * s_new, axis=0, keepdims=True)

    gz = proj_ref[:, d_pool + 3 * d_hgrn:d_pool + 4 * d_hgrn]
    gate = gain_ref[...] * _sigmoid(gz)
    for hd in range(n_heads):
        sl = slice(hd * HEAD, (hd + 1) * HEAD)
        o = o_ref[:, sl]
        o = o * lax.rsqrt(jnp.mean(o * o, axis=-1, keepdims=True) + EPS)
        mixed_ref[:, d_pool + hd * HEAD:d_pool + (hd + 1) * HEAD] = (o * gate[:, sl]).astype(BF16)


def even_mix_sample(proj, pool_flat, hstate, wpool_bd, pscale, lbl, gain, *, layer, pos0, bb):
    b, n_in = proj.shape
    d_pool = wpool_bd.shape[0]
    d_hgrn = gain.shape[1]
    n_heads = d_hgrn // HEAD
    d = d_pool + d_hgrn
    kern = functools.partial(_even_sample_kernel, layer=layer, pos0=pos0)
    return pl.pallas_call(
        kern,
        grid=(b // bb,),
        in_specs=[pl.BlockSpec((bb, n_in), lambda i: (i, 0)),
                  pl.BlockSpec((bb, POOL_BUF * d_pool), lambda i: (i, 0)),
                  pl.BlockSpec((bb, n_heads, HEAD, HEAD), lambda i: (i, 0, 0, 0)),
                  _resident((d_pool, d_pool)), _resident((1, d_pool)), _resident(lbl.shape),
                  _resident((1, d_hgrn))],
        out_specs=[pl.BlockSpec((bb, d), lambda i: (i, 0)),
                   pl.BlockSpec((bb, POOL_BUF * d_pool), lambda i: (i, 0)),
                   pl.BlockSpec((bb, n_heads, HEAD, HEAD), lambda i: (i, 0, 0, 0))],
        out_shape=[jax.ShapeDtypeStruct((b, d), BF16),
                   jax.ShapeDtypeStruct((b, POOL_BUF * d_pool), F32),
                   jax.ShapeDtypeStruct((b, n_heads, HEAD, HEAD), F32)],
        scratch_shapes=[pltpu.VMEM((bb, d_hgrn), F32)] * 2,
        compiler_params=pltpu.CompilerParams(dimension_semantics=("arbitrary",),
                                             vmem_limit_bytes=VMEM_LIMIT_V7X),
        name="even_mix_sample",
    )(proj, pool_flat, hstate, wpool_bd, pscale, lbl, gain)


def _odd_prompt_kernel(x_ref, g_ref, win_ref, cw_ref, mixed_ref, cstate_ref, zext_ref, *, tb):
    j = pl.program_id(1)
    last = pl.num_programs(1) - 1
    dc = cw_ref.shape[1]

    @pl.when(j == 0)
    def _():
        zext_ref[0:CONV_HIST, :] = jnp.zeros((CONV_HIST, dc), F32)

    h = _rms(x_ref[...], g_ref[...]).astype(BF16)
    cg = jnp.dot(h, win_ref[:, dc:2 * dc], preferred_element_type=F32)
    hv = jnp.dot(h, win_ref[:, 2 * dc:3 * dc], preferred_element_type=F32)
    z = cg * hv
    zext_ref[CONV_HIST:CONV_HIST + tb, :] = z
    conv = cw_ref[CONV_WIDTH - 1:CONV_WIDTH, :] * z
    for s in range(1, CONV_WIDTH):
        conv = conv + cw_ref[CONV_WIDTH - 1 - s:CONV_WIDTH - s, :] * zext_ref[pl.ds(CONV_HIST - s, tb), :]
    bg = jnp.dot(h, win_ref[:, 0:dc], preferred_element_type=F32)
    mixed_ref[...] = (bg * conv).astype(BF16)

    @pl.when(j == last)
    def _():
        cstate_ref[...] = zext_ref[pl.ds(CONV_HIST + tb - CONV_BUF, CONV_BUF), :]

    zext_ref[0:CONV_HIST, :] = zext_ref[tb:tb + CONV_HIST, :]


def odd_mix_prompt(x, g, w_in, conv_w, *, tb):
    b, t, d = x.shape
    dc = conv_w.shape[1]
    kern = functools.partial(_odd_prompt_kernel, tb=tb)
    return pl.pallas_call(
        kern,
        grid=(b, t // tb),
        in_specs=[pl.BlockSpec((None, tb, d), lambda i, j: (i, j, 0)),
                  _resident((1, d)), _resident((d, 3 * dc)), _resident((CONV_WIDTH, dc))],
        out_specs=[pl.BlockSpec((None, tb, dc), lambda i, j: (i, j, 0)),
                   pl.BlockSpec((None, CONV_BUF, dc), lambda i, j: (i, 0, 0))],
        out_shape=[jax.ShapeDtypeStruct((b, t, dc), BF16),
                   jax.ShapeDtypeStruct((b, CONV_BUF, dc), F32)],
        scratch_shapes=[pltpu.VMEM((CONV_HIST + tb, dc), F32)],
        compiler_params=pltpu.CompilerParams(dimension_semantics=("arbitrary", "arbitrary"),
                                             vmem_limit_bytes=VMEM_LIMIT_V7X),
        name="odd_mix_prompt",
    )(x, g, w_in, conv_w)


def _odd_sample_kernel(x_ref, g_ref, win_ref, cw_ref, cst_ref, mixed_ref, ncst_ref):
    dc = cw_ref.shape[1]
    h = _rms(x_ref[...], g_ref[...]).astype(BF16)
    cg = jnp.dot(h, win_ref[:, dc:2 * dc], preferred_element_type=F32)
    hv = jnp.dot(h, win_ref[:, 2 * dc:3 * dc], preferred_element_type=F32)
    z = cg * hv
    conv = cw_ref[CONV_WIDTH - 1:CONV_WIDTH, :] * z
    for s in range(1, CONV_WIDTH):
        r0 = (CONV_BUF - s) * dc
        conv = conv + cw_ref[CONV_WIDTH - 1 - s:CONV_WIDTH - s, :] * cst_ref[:, r0:r0 + dc]
    bg = jnp.dot(h, win_ref[:, 0:dc], preferred_element_type=F32)
    mixed_ref[...] = (bg * conv).astype(BF16)
    ncst_ref[:, 0:(CONV_BUF - 1) * dc] = cst_ref[:, dc:CONV_BUF * dc]
    ncst_ref[:, (CONV_BUF - 1) * dc:CONV_BUF * dc] = z


def odd_mix_sample(x, g, w_in, conv_w, cstate_flat):
    b, d = x.shape
    dc = conv_w.shape[1]
    return pl.pallas_call(
        _odd_sample_kernel,
        grid=(1,),
        in_specs=[_resident((b, d)), _resident((1, d)), _resident((d, 3 * dc)),
                  _resident((CONV_WIDTH, dc)), _resident((b, CONV_BUF * dc))],
        out_specs=[pl.BlockSpec((b, dc), lambda i: (0, 0)),
                   pl.BlockSpec((b, CONV_BUF * dc), lambda i: (0, 0))],
        out_shape=[jax.ShapeDtypeStruct((b, dc), BF16),
                   jax.ShapeDtypeStruct((b, CONV_BUF * dc), F32)],
        compiler_params=pltpu.CompilerParams(dimension_semantics=("arbitrary",),
                                             vmem_limit_bytes=VMEM_LIMIT_V7X),
        name="odd_mix_sample",
    )(x, g, w_in, conv_w, cstate_flat)


def _block_diag(w):
    g, c, _ = w.shape
    rows = [jnp.pad(w[i], ((0, 0), (i * c, (g - 1 - i) * c))) for i in range(g)]
    return jnp.concatenate(rows, axis=0)


def kernel(x_prompt, x_sample, state_pool, state_hgrn, state_conv, norm_mix, norm_mlp, norm_final, even_w_in, pool_w, pool_scale, hgrn_lb_logits, hgrn_gain, even_w_out, odd_w_in, conv_w, odd_w_out, ff_w1, ff_w2):
    depth = norm_mix.shape[0]
    b, t, d = x_prompt.shape
    db, ds, _ = x_sample.shape
    assert ds == 1, "the sample group carries one token per sequence"
    xp = x_prompt.reshape(b * t, d)
    xs = x_sample.reshape(db, d)
    g_mlp = norm_mlp[:, None, :]
    pool_p, hgrn_p, conv_p, pool_s, hgrn_s, conv_s = [], [], [], [], [], []
    for l in range(depth):
        g_mix = norm_mix[l][None]
        if l % 2 == 0:
            e = l // 2
            w_in = even_w_in[e].astype(BF16)
            pool_bd = _block_diag(pool_w[e]).astype(BF16)
            shared = (pool_bd, pool_scale[e][None], hgrn_lb_logits, hgrn_gain[e][None])
            mixed_p, p_new, s_new = even_mix_prompt(xp.reshape(b, t, d), g_mix, w_in, *shared, layer=l, tb=256)
            pool_p.append(p_new)
            hgrn_p.append(s_new)
            proj_s = norm_proj(xs, g_mix, w_in, bm=db)
            mixed_s, p_new, s_new = even_mix_sample(proj_s, state_pool[e].reshape(db, -1), state_hgrn[e], *shared,
                                                    layer=l, pos0=PAST_LEN, bb=8)
            pool_s.append(p_new.reshape(db, POOL_BUF, -1))
            hgrn_s.append(s_new)
            w_out = even_w_out[e]
        else:
            o = l // 2
            mixed_p, c_new = odd_mix_prompt(xp.reshape(b, t, d), g_mix, odd_w_in[o], conv_w[o], tb=512)
            conv_p.append(c_new)
            mixed_s, c_new = odd_mix_sample(xs, g_mix, odd_w_in[o], conv_w[o], state_conv[o].reshape(db, -1))
            conv_s.append(c_new.reshape(db, CONV_BUF, -1))
            w_out = odd_w_out[o]
        xp, xs = out_mlp(mixed_p.reshape(b * t, d), xp, mixed_s, xs, w_out, g_mlp, ff_w1, ff_w2, norm_final[None],
                         layer=l, final_norm=l == depth - 1, bm=512)
    return (xp.reshape(b, t, d), xs.reshape(db, ds, d), jnp.stack(pool_p), jnp.stack(hgrn_p), jnp.stack(conv_p),
            jnp.stack(pool_s), jnp.stack(hgrn_s), jnp.stack(conv_s))
```

```python
import functools

import jax
import jax.numpy as jnp
from jax import lax
from jax.experimental import pallas as pl
from jax.experimental.pallas import tpu as pltpu

F32 = jnp.float32
BF16 = jnp.bfloat16

EPS = 1e-6
PAST_LEN = 16384
POOL_WINDOWS = (2, 4, 8, 16)
POOL_GROUP_DIM = 64
POOL_BUF = max(POOL_WINDOWS) - 1
POOL_HIST = 16
POOL_PAD = 8
LANES = 128
HGRN_CHUNK = 32
HGRN_BLOCK = 128
HEAD = 128
CONV_WIDTH = 3
CONV_BUF = CONV_WIDTH - 1
CONV_HIST = 8

VMEM_LIMIT_V7X = 56 * 1024 * 1024
MXU_COLS_V7X = 256


def _resident(shape):
    nd = len(shape)
    return pl.BlockSpec(shape, lambda *_: (0,) * nd, pipeline_mode=pl.Buffered(1))


def _layer_block(stacked, layer):
    nd = stacked.ndim
    return pl.BlockSpec((None,) + stacked.shape[1:], lambda *_: (layer,) + (0,) * (nd - 1),
                        pipeline_mode=pl.Buffered(1))


def _rms(x, g):
    return x * lax.rsqrt(jnp.mean(x * x, axis=-1, keepdims=True) + EPS) * g


def _sigmoid(x):
    return 1.0 / (1.0 + jnp.exp(-x))


def _lower_bound(lbl, layer):
    e = jnp.exp(lbl - jnp.max(lbl, axis=0, keepdims=True))
    p = e / jnp.sum(e, axis=0, keepdims=True)
    return jnp.sum(p[0:layer + 1], axis=0, keepdims=True)


def _pool_select(snaps, u_shape):
    grp = lax.broadcasted_iota(jnp.int32, u_shape, 1) // POOL_GROUP_DIM
    s = snaps[POOL_WINDOWS[-1]]
    for g in range(len(POOL_WINDOWS) - 2, -1, -1):
        s = jnp.where(grp == g, snaps[POOL_WINDOWS[g]], s)
    win = jnp.left_shift(2, grp)
    return s, win


def _norm_proj_kernel(x_ref, g_ref, w_ref, o_ref):
    h = _rms(x_ref[...], g_ref[...]).astype(BF16)
    o_ref[...] = jnp.dot(h, w_ref[...], preferred_element_type=F32)


def norm_proj(x, g, w, *, bm):
    m, d = x.shape
    n = w.shape[1]
    return pl.pallas_call(
        _norm_proj_kernel,
        grid=(m // bm,),
        in_specs=[pl.BlockSpec((bm, d), lambda i: (i, 0)), _resident((1, d)), _resident((d, n))],
        out_specs=pl.BlockSpec((bm, n), lambda i: (i, 0)),
        out_shape=jax.ShapeDtypeStruct((m, n), F32),
        compiler_params=pltpu.CompilerParams(dimension_semantics=("arbitrary",),
                                             vmem_limit_bytes=VMEM_LIMIT_V7X),
        name="norm_proj",
    )(x, g, w)


def _out_mlp_kernel(m_ref, x_ref, ms_ref, xs_ref, wout_ref, g_ref, w1_ref, w2_ref, gf_ref, o_ref, os_ref,
                    *, final_norm, ff_chunk):
    d_ff = w1_ref.shape[1]

    def block(mixed, x):
        y = x + jnp.dot(mixed, wout_ref[...], preferred_element_type=F32)
        h = _rms(y, g_ref[...]).astype(BF16)
        acc = y
        for c in range(d_ff // ff_chunk):
            a = jnp.dot(h, w1_ref[:, c * ff_chunk:(c + 1) * ff_chunk], preferred_element_type=F32)
            a = jnp.square(jnp.maximum(a, 0.0)).astype(BF16)
            acc = acc + jnp.dot(a, w2_ref[c * ff_chunk:(c + 1) * ff_chunk, :], preferred_element_type=F32)
        return _rms(acc, gf_ref[...]) if final_norm else acc

    o_ref[...] = block(m_ref[...], x_ref[...])

    @pl.when(pl.program_id(0) == pl.num_programs(0) - 1)
    def _():
        os_ref[...] = block(ms_ref[...], xs_ref[...])


def out_mlp(mixed, x, mixed_s, x_s, w_out, g_mlp, w1, w2, g_final, *, layer, final_norm, bm):
    m, d = x.shape
    ms = x_s.shape[0]
    kern = functools.partial(_out_mlp_kernel, final_norm=final_norm, ff_chunk=1024)
    return pl.pallas_call(
        kern,
        grid=(m // bm,),
        in_specs=[pl.BlockSpec((bm, d), lambda i: (i, 0)), pl.BlockSpec((bm, d), lambda i: (i, 0)),
                  _resident((ms, d)), _resident((ms, d)),
                  _resident((d, d)), _layer_block(g_mlp, layer), _layer_block(w1, layer),
                  _layer_block(w2, layer), _resident((1, d))],
        out_specs=[pl.BlockSpec((bm, d), lambda i: (i, 0)), pl.BlockSpec((ms, d), lambda i: (0, 0))],
        out_shape=[jax.ShapeDtypeStruct((m, d), F32), jax.ShapeDtypeStruct((ms, d), F32)],
        compiler_params=pltpu.CompilerParams(dimension_semantics=("arbitrary",),
                                             vmem_limit_bytes=VMEM_LIMIT_V7X),
        name="out_mlp",
    )(mixed, x, mixed_s, x_s, w_out, g_mlp, w1, w2, g_final)


def _even_prompt_kernel(xn_ref, x0_ref, g_ref, win_ref, wpool_ref, pscale_ref, lbl_ref, gain_ref,
                        mixed_ref, pstate_ref, hstate_ref,
                        pu_ref, pq_ref, pf_ref, pi_ref, pg_ref, hn_ref, ext_ref, lvl_ref, st_ref, k_ref, lsplit_ref,
                        bcum_ref, dec_ref, gate_ref, v_ref, qd_ref, kd_ref, ke_ref, qb_ref, kb_ref, qr_ref, kc_ref,
                        *, layer, tb, tiles_per_seq):
    step = pl.program_id(0)
    d_pool = wpool_ref.shape[0]
    n_heads = st_ref.shape[0]
    d_hgrn = n_heads * HEAD
    nc = tb // HGRN_CHUNK
    nblk = tb // HGRN_BLOCK
    j = step % tiles_per_seq
    last = tiles_per_seq - 1
    o_q, o_f, o_i, o_g = d_pool, d_pool + d_hgrn, d_pool + 2 * d_hgrn, d_pool + 3 * d_hgrn
    sections = [(pu_ref, 0), (pq_ref, o_q), (pf_ref, o_f), (pi_ref, o_i), (pg_ref, o_g)]

    def chunks(dst_ref, col0):
        def make(c0, c1):
            def run():
                dst_ref[:, c0:c1] = jnp.dot(hn_ref[...], win_ref[:, col0 + c0:col0 + c1],
                                            preferred_element_type=F32)
            return run
        width = dst_ref.shape[1]
        return [make(c0, min(c0 + MXU_COLS_V7X, width)) for c0 in range(0, width, MXU_COLS_V7X)]

    @pl.when(step == 0)
    def _():
        hn_ref[...] = _rms(x0_ref[...], g_ref[...]).astype(BF16)
        for dst_ref, col0 in sections:
            for run in chunks(dst_ref, col0):
                run()

    r1_, r2_ = POOL_PAD + POOL_HIST, POOL_PAD + POOL_HIST + tb

    @pl.when(j == 0)
    def _():
        ext_ref[0:r1_, :] = jnp.zeros((r1_, d_pool), F32)
        lvl_ref[:, 0:POOL_PAD, :] = jnp.zeros((lvl_ref.shape[0], POOL_PAD, LANES), F32)
        st_ref[...] = jnp.zeros(st_ref.shape, F32)

    hn_ref[...] = _rms(xn_ref[...], g_ref[...]).astype(BF16)

    u = pu_ref[...]
    ext_ref[r1_:r2_, :] = u
    groups_per_tile = LANES // POOL_GROUP_DIM
    lane_grp = lax.broadcasted_iota(jnp.int32, (tb, LANES), 1) // POOL_GROUP_DIM
    pos = j * tb + lax.broadcasted_iota(jnp.int32, (tb, LANES), 0)
    means = []
    for lt in range(d_pool // LANES):
        wins = POOL_WINDOWS[lt * groups_per_tile:(lt + 1) * groups_per_tile]
        src, src_lanes, w, nbuf, got = ext_ref, slice(lt * LANES, (lt + 1) * LANES), 1, 0, {}
        while w < wins[-1]:
            new = src[POOL_PAD:r2_, src_lanes] + src[POOL_PAD - w:r2_ - w, src_lanes]
            w *= 2
            if w in wins:
                got[w] = new[POOL_HIST:]
            if w < wins[-1]:
                lvl_ref[nbuf, POOL_PAD:r2_, :] = new
                src, src_lanes, nbuf = lvl_ref.at[nbuf], slice(0, LANES), 1 - nbuf
        ssum, win = got[wins[-1]], jnp.full((tb, LANES), wins[-1], jnp.int32)
        for g_ in range(groups_per_tile - 2, -1, -1):
            ssum = jnp.where(lane_grp == g_, got[wins[g_]], ssum)
            win = jnp.where(lane_grp == g_, wins[g_], win)
        cnt = jnp.minimum(win, pos + 1).astype(F32)
        means.append(ssum / cnt)
    diff = (jnp.concatenate(means, axis=1) - u).astype(BF16)
    pool_out = jnp.dot(diff, wpool_ref[...], preferred_element_type=F32) * pscale_ref[...]
    mixed_ref[:, 0:d_pool] = pool_out.astype(BF16)

    @pl.when(j == last)
    def _():
        pstate_ref[...] = ext_ref[pl.ds(r2_ - POOL_BUF, POOL_BUF), :]

    ext_ref[POOL_PAD:r1_, :] = ext_ref[POOL_PAD + tb:r1_ + tb, :]
    for run in chunks(pu_ref, 0):
        run()

    lb = _lower_bound(lbl_ref[...], layer)
    for c in range(nc):
        rows = slice(c * HGRN_CHUNK, (c + 1) * HGRN_CHUNK)
        fz = pf_ref[rows, :]
        a = jnp.exp(-jnp.abs(fz))
        r = 1.0 / (1.0 + a)
        ar = a * r
        sig = jnp.where(fz >= 0, r, ar)
        sig_neg = jnp.where(fz >= 0, ar, r)
        logf = jnp.log(lb + (1.0 - lb) * sig)
        k_ref[rows, :] = (1.0 - lb) * sig_neg
        l_hi = logf.astype(BF16)
        r1 = logf - l_hi.astype(F32)
        l_mid = r1.astype(BF16)
        lsplit_ref[rows, 0:d_hgrn] = l_hi
        lsplit_ref[rows, d_hgrn:2 * d_hgrn] = l_mid
        lsplit_ref[rows, 2 * d_hgrn:3 * d_hgrn] = (r1 - l_mid.astype(F32)).astype(BF16)
        iz = pi_ref[rows, :]
        v_ref[rows, :] = (iz * _sigmoid(iz)).astype(BF16)
        gate_ref[rows, :] = gain_ref[...] * _sigmoid(pg_ref[rows, :])

    ri = lax.broadcasted_iota(jnp.int32, (tb, tb), 0)
    ci = lax.broadcasted_iota(jnp.int32, (tb, tb), 1)
    tri = ((ri // HGRN_CHUNK == ci // HGRN_CHUNK) & (ci <= ri)).astype(BF16)
    cs = jnp.dot(tri, lsplit_ref[...], preferred_element_type=F32)
    bcum_ref[...] = cs[:, 0:d_hgrn] + cs[:, d_hgrn:2 * d_hgrn] + cs[:, 2 * d_hgrn:3 * d_hgrn]
    for run in chunks(pf_ref, o_f) + chunks(pi_ref, o_i):
        run()

    nsub = HGRN_BLOCK // HGRN_CHUNK
    half = nsub // 2
    for b in range(nblk):
        blast = [bcum_ref[(b * nsub + i + 1) * HGRN_CHUNK - 1:(b * nsub + i + 1) * HGRN_CHUNK, :]
                 for i in range(nsub)]
        pre = [jnp.zeros_like(blast[0])]
        for i in range(nsub):
            pre.append(pre[-1] + blast[i])
        dec_ref[b:b + 1, :] = jnp.exp(pre[nsub])
        for i in range(nsub):
            c = b * nsub + i
            rows = slice(c * HGRN_CHUNK, (c + 1) * HGRN_CHUNK)
            bc = bcum_ref[rows, :]
            qd = pq_ref[rows, :] * jnp.exp(bc)
            kd = k_ref[rows, :] * jnp.exp(-bc)
            ke = kd * jnp.exp(blast[i])
            qd_ref[rows, :] = qd.astype(BF16)
            kd_ref[rows, :] = kd.astype(BF16)
            ke_ref[rows, :] = ke.astype(BF16)
            qb_ref[rows, :] = (qd * jnp.exp(pre[i])).astype(BF16)
            kb_ref[rows, :] = (ke * jnp.exp(pre[nsub] - pre[i + 1])).astype(BF16)
            if i >= half:
                qr_ref[rows, :] = (qd * jnp.exp(pre[i] - pre[half])).astype(BF16)
                kc_ref[rows, :] = jnp.zeros((HGRN_CHUNK, d_hgrn), BF16)
            else:
                kc_ref[rows, :] = (ke * jnp.exp(pre[half] - pre[i + 1])).astype(BF16)

    rb = lax.broadcasted_iota(jnp.int32, (HGRN_BLOCK, HGRN_BLOCK), 0)
    cb = lax.broadcasted_iota(jnp.int32, (HGRN_BLOCK, HGRN_BLOCK), 1)
    rsub, csub = rb // HGRN_CHUNK, cb // HGRN_CHUNK
    m_diag = (rsub == csub) & (cb <= rb)
    m_adj = (csub == rsub - 1) & (rsub != half)
    eye = rb == cb
    hb = HGRN_BLOCK // 2
    nt_dims = (((1,), (1,)), ((), ()))
    tn_dims = (((0,), (0,)), ((), ()))
    fillers = chunks(pq_ref, o_q) + chunks(pg_ref, o_g)

    def fill(n):
        for _ in range(min(n, len(fillers))):
            fillers.pop(0)()

    per_stage = -(-len(fillers) // (3 * nblk))
    for b in range(nblk):
        rows = slice(b * HGRN_BLOCK, (b + 1) * HGRN_BLOCK)
        far = slice(b * HGRN_BLOCK + hb, (b + 1) * HGRN_BLOCK)
        g12s, g3s = [], []
        for hd in range(n_heads):
            sl = slice(hd * HEAD, (hd + 1) * HEAD)
            kk = jnp.concatenate([kd_ref[rows, sl], ke_ref[rows, sl]], axis=0)
            g12s.append(lax.dot_general(qd_ref[rows, sl], kk, nt_dims, preferred_element_type=F32))
            g3s.append(lax.dot_general(qr_ref[far, sl], kc_ref[rows, sl], nt_dims, preferred_element_type=F32))
        fill(per_stage)
        outs = []
        for hd in range(n_heads):
            sl = slice(hd * HEAD, (hd + 1) * HEAD)
            g12 = g12s[hd]
            scores = jnp.where(m_diag, g12[:, 0:HGRN_BLOCK], jnp.where(m_adj, g12[:, HGRN_BLOCK:], 0.0))
            scores = jnp.concatenate([scores[0:hb], scores[hb:] + g3s[hd]], axis=0).astype(BF16)
            v_blk = v_ref[rows, sl]
            st = st_ref[hd]
            lhs = jnp.concatenate([scores, qb_ref[rows, sl]], axis=1)
            rhs = jnp.concatenate([v_blk, st.astype(BF16)], axis=0)
            outs.append(jnp.dot(lhs, rhs, preferred_element_type=F32))
            d_col = jnp.sum(jnp.where(eye, dec_ref[b:b + 1, sl], 0.0), axis=1, keepdims=True)
            st_ref[hd] = st * d_col + lax.dot_general(kb_ref[rows, sl], v_blk, tn_dims,
                                                      preferred_element_type=F32)
        fill(per_stage)
        for hd in range(n_heads):
            sl = slice(hd * HEAD, (hd + 1) * HEAD)
            o = outs[hd]
            o = o * lax.rsqrt(jnp.mean(o * o, axis=-1, keepdims=True) + EPS)
            mixed_ref[rows, d_pool + hd * HEAD:d_pool + (hd + 1) * HEAD] = (o * gate_ref[rows, sl]).astype(BF16)
        fill(per_stage)
    fill(len(fillers))

    @pl.when(j == last)
    def _():
        hstate_ref[...] = st_ref[...]


def even_mix_prompt(x, g, w_in, wpool_bd, pscale, lbl, gain, *, layer, tb):
    b, t, d = x.shape
    n_in = w_in.shape[1]
    d_pool = wpool_bd.shape[0]
    d_hgrn = gain.shape[1]
    n_heads = d_hgrn // HEAD
    tps = t // tb
    n_tiles = b * tps
    kern = functools.partial(_even_prompt_kernel, layer=layer, tb=tb, tiles_per_seq=tps)

    def next_tile(s):
        tile = jnp.minimum(s + 1, n_tiles - 1)
        return tile // tps, tile % tps

    assert d_pool % LANES == 0 and LANES % POOL_GROUP_DIM == 0 and tb % HGRN_BLOCK == 0
    return pl.pallas_call(
        kern,
        grid=(n_tiles,),
        in_specs=[pl.BlockSpec((None, tb, d), lambda s: (*next_tile(s), 0)),
                  pl.BlockSpec((None, tb, d), lambda s: (0, 0, 0)),
                  _resident((1, d)), _resident((d, n_in)), _resident((d_pool, d_pool)),
                  _resident((1, d_pool)), _resident(lbl.shape), _resident((1, d_hgrn))],
        out_specs=[pl.BlockSpec((None, tb, d), lambda s: (s // tps, s % tps, 0)),
                   pl.BlockSpec((None, POOL_BUF, d_pool), lambda s: (s // tps, 0, 0)),
                   pl.BlockSpec((None, n_heads, HEAD, HEAD), lambda s: (s // tps, 0, 0, 0))],
        out_shape=[jax.ShapeDtypeStruct((b, t, d), BF16),
                   jax.ShapeDtypeStruct((b, POOL_BUF, d_pool), F32),
                   jax.ShapeDtypeStruct((b, n_heads, HEAD, HEAD), F32)],
        scratch_shapes=[pltpu.VMEM((tb, d_pool), F32)]
                       + [pltpu.VMEM((tb, d_hgrn), F32)] * 4
                       + [pltpu.VMEM((tb, d), BF16),
                        pltpu.VMEM((POOL_PAD + POOL_HIST + tb, d_pool), F32),
                        pltpu.VMEM((2, POOL_PAD + POOL_HIST + tb, LANES), F32),
                        pltpu.VMEM((n_heads, HEAD, HEAD), F32),
                        pltpu.VMEM((tb, d_hgrn), F32),
                        pltpu.VMEM((tb, 3 * d_hgrn), BF16),
                        pltpu.VMEM((tb, d_hgrn), F32),
                        pltpu.VMEM((tb // HGRN_BLOCK, d_hgrn), F32),
                        pltpu.VMEM((tb, d_hgrn), F32)]
                       + [pltpu.VMEM((tb, d_hgrn), BF16)] * 8,
        compiler_params=pltpu.CompilerParams(dimension_semantics=("arbitrary",),
                                             vmem_limit_bytes=VMEM_LIMIT_V7X),
        name="even_mix_prompt",
    )(x, x, g, w_in, wpool_bd, pscale, lbl, gain)


def _even_sample_kernel(proj_ref, pool_ref, hst_ref, wpool_ref, pscale_ref, lbl_ref, gain_ref,
                        mixed_ref, npool_ref, nhst_ref, v_ref, o_ref, *, layer, pos0):
    bb = proj_ref.shape[0]
    d_pool = wpool_ref.shape[0]
    n_heads = hst_ref.shape[1]
    d_hgrn = n_heads * HEAD

    u = proj_ref[:, 0:d_pool]
    acc = u
    snaps = {}
    for s in range(1, POOL_WINDOWS[-1]):
        r0 = (POOL_BUF - s) * d_pool
        acc = acc + pool_ref[:, r0:r0 + d_pool]
        if s + 1 in POOL_WINDOWS:
            snaps[s + 1] = acc
    ssum, win = _pool_select(snaps, u.shape)
    cnt = jnp.minimum(win, pos0 + 1).astype(F32)
    diff = (ssum / cnt - u).astype(BF16)
    pool_out = jnp.dot(diff, wpool_ref[...], preferred_element_type=F32) * pscale_ref[...]
    mixed_ref[:, 0:d_pool] = pool_out.astype(BF16)
    npool_ref[:, 0:(POOL_BUF - 1) * d_pool] = pool_ref[:, d_pool:POOL_BUF * d_pool]
    npool_ref[:, (POOL_BUF - 1) * d_pool:POOL_BUF * d_pool] = u

    lb = _lower_bound(lbl_ref[...], layer)
    fz = proj_ref[:, d_pool + d_hgrn:d_pool + 2 * d_hgrn]
    iz = proj_ref[:, d_pool + 2 * d_hgrn:d_pool + 3 * d_hgrn]
    v_ref[...] = iz * _sigmoid(iz)
    fkq = (lb + (1.0 - lb) * _sigmoid(fz), (1.0 - lb) * _sigmoid(-fz), proj_ref[:, d_pool:d_pool + d_hgrn])

    def split3(x):
        hi = x.astype(BF16).astype(F32)
        r1 = x - hi
        mid = r1.astype(BF16).astype(F32)
        return jnp.concatenate([hi, mid, r1 - mid], axis=0)

    sel_r = lax.broadcasted_iota(jnp.int32, (3 * bb, bb * HEAD), 0) % bb
    sel_c = lax.broadcasted_iota(jnp.int32, (3 * bb, bb * HEAD), 1) // HEAD
    sel = (sel_r == sel_c).astype(F32)
    parts = [split3(x) for x in fkq]
    for hd in range(n_heads):
        sl = slice(hd * HEAD, (hd + 1) * HEAD)
        f_cols, k_cols, q_cols = [lax.dot_general(p[:, sl], sel, (((0,), (0,)), ((), ())),
                                                  preferred_element_type=F32) for p in parts]
        for i in range(bb):
            blk = slice(i * HEAD, (i + 1) * HEAD)
            s_new = f_cols[:, blk] * hst_ref[i, hd] + k_cols[:, blk] * v_ref[i:i + 1, sl]
            nhst_ref[i, hd] = s_new
            o_ref[i:i + 1, sl] = jnp.sum(q_cols[:, blk] * s_new, axis=0, keepdims=True)

    gz = proj_ref[:, d_pool + 3 * d_hgrn:d_pool + 4 * d_hgrn]
    gate = gain_ref[...] * _sigmoid(gz)
    for hd in range(n_heads):
        sl = slice(hd * HEAD, (hd + 1) * HEAD)
        o = o_ref[:, sl]
        o = o * lax.rsqrt(jnp.mean(o * o, axis=-1, keepdims=True) + EPS)
        mixed_ref[:, d_pool + hd * HEAD:d_pool + (hd + 1) * HEAD] = (o * gate[:, sl]).astype(BF16)


def even_mix_sample(proj, pool_flat, hstate, wpool_bd, pscale, lbl, gain, *, layer, pos0, bb):
    b, n_in = proj.shape
    d_pool = wpool_bd.shape[0]
    d_hgrn = gain.shape[1]
    n_heads = d_hgrn // HEAD
    d = d_pool + d_hgrn
    kern = functools.partial(_even_sample_kernel, layer=layer, pos0=pos0)
    return pl.pallas_call(
        kern,
        grid=(b // bb,),
        in_specs=[pl.BlockSpec((bb, n_in), lambda i: (i, 0)),
                  pl.BlockSpec((bb, POOL_BUF * d_pool), lambda i: (i, 0)),
                  pl.BlockSpec((bb, n_heads, HEAD, HEAD), lambda i: (i, 0, 0, 0)),
                  _resident((d_pool, d_pool)), _resident((1, d_pool)), _resident(lbl.shape),
                  _resident((1, d_hgrn))],
        out_specs=[pl.BlockSpec((bb, d), lambda i: (i, 0)),
                   pl.BlockSpec((bb, POOL_BUF * d_pool), lambda i: (i, 0)),
                   pl.BlockSpec((bb, n_heads, HEAD, HEAD), lambda i: (i, 0, 0, 0))],
        out_shape=[jax.ShapeDtypeStruct((b, d), BF16),
                   jax.ShapeDtypeStruct((b, POOL_BUF * d_pool), F32),
                   jax.ShapeDtypeStruct((b, n_heads, HEAD, HEAD), F32)],
        scratch_shapes=[pltpu.VMEM((bb, d_hgrn), F32)] * 2,
        compiler_params=pltpu.CompilerParams(dimension_semantics=("arbitrary",),
                                             vmem_limit_bytes=VMEM_LIMIT_V7X),
        name="even_mix_sample",
    )(proj, pool_flat, hstate, wpool_bd, pscale, lbl, gain)


def _odd_prompt_kernel(x_ref, g_ref, win_ref, cw_ref, mixed_ref, cstate_ref, zext_ref, *, tb):
    j = pl.program_id(1)
    last = pl.num_programs(1) - 1
    dc = cw_ref.shape[1]

    @pl.when(j == 0)
    def _():
        zext_ref[0:CONV_HIST, :] = jnp.zeros((CONV_HIST, dc), F32)

    h = _rms(x_ref[...], g_ref[...]).astype(BF16)
    cg = jnp.dot(h, win_ref[:, dc:2 * dc], preferred_element_type=F32)
    hv = jnp.dot(h, win_ref[:, 2 * dc:3 * dc], preferred_element_type=F32)
    z = cg * hv
    zext_ref[CONV_HIST:CONV_HIST + tb, :] = z
    conv = cw_ref[CONV_WIDTH - 1:CONV_WIDTH, :] * z
    for s in range(1, CONV_WIDTH):
        conv = conv + cw_ref[CONV_WIDTH - 1 - s:CONV_WIDTH - s, :] * zext_ref[pl.ds(CONV_HIST - s, tb), :]
    bg = jnp.dot(h, win_ref[:, 0:dc], preferred_element_type=F32)
    mixed_ref[...] = (bg * conv).astype(BF16)

    @pl.when(j == last)
    def _():
        cstate_ref[...] = zext_ref[pl.ds(CONV_HIST + tb - CONV_BUF, CONV_BUF), :]

    zext_ref[0:CONV_HIST, :] = zext_ref[tb:tb + CONV_HIST, :]


def odd_mix_prompt(x, g, w_in, conv_w, *, tb):
    b, t, d = x.shape
    dc = conv_w.shape[1]
    kern = functools.partial(_odd_prompt_kernel, tb=tb)
    return pl.pallas_call(
        kern,
        grid=(b, t // tb),
        in_specs=[pl.BlockSpec((None, tb, d), lambda i, j: (i, j, 0)),
                  _resident((1, d)), _resident((d, 3 * dc)), _resident((CONV_WIDTH, dc))],
        out_specs=[pl.BlockSpec((None, tb, dc), lambda i, j: (i, j, 0)),
                   pl.BlockSpec((None, CONV_BUF, dc), lambda i, j: (i, 0, 0))],
        out_shape=[jax.ShapeDtypeStruct((b, t, dc), BF16),
                   jax.ShapeDtypeStruct((b, CONV_BUF, dc), F32)],
        scratch_shapes=[pltpu.VMEM((CONV_HIST + tb, dc), F32)],
        compiler_params=pltpu.CompilerParams(dimension_semantics=("arbitrary", "arbitrary"),
                                             vmem_limit_bytes=VMEM_LIMIT_V7X),
        name="odd_mix_prompt",
    )(x, g, w_in, conv_w)


def _odd_sample_kernel(x_ref, g_ref, win_ref, cw_ref, cst_ref, mixed_ref, ncst_ref):
    dc = cw_ref.shape[1]
    h = _rms(x_ref[...], g_ref[...]).astype(BF16)
    cg = jnp.dot(h, win_ref[:, dc:2 * dc], preferred_element_type=F32)
    hv = jnp.dot(h, win_ref[:, 2 * dc:3 * dc], preferred_element_type=F32)
    z = cg * hv
    conv = cw_ref[CONV_WIDTH - 1:CONV_WIDTH, :] * z
    for s in range(1, CONV_WIDTH):
        r0 = (CONV_BUF - s) * dc
        conv = conv + cw_ref[CONV_WIDTH - 1 - s:CONV_WIDTH - s, :] * cst_ref[:, r0:r0 + dc]
    bg = jnp.dot(h, win_ref[:, 0:dc], preferred_element_type=F32)
    mixed_ref[...] = (bg * conv).astype(BF16)
    ncst_ref[:, 0:(CONV_BUF - 1) * dc] = cst_ref[:, dc:CONV_BUF * dc]
    ncst_ref[:, (CONV_BUF - 1) * dc:CONV_BUF * dc] = z


def odd_mix_sample(x, g, w_in, conv_w, cstate_flat):
    b, d = x.shape
    dc = conv_w.shape[1]
    return pl.pallas_call(
        _odd_sample_kernel,
        grid=(1,),
        in_specs=[_resident((b, d)), _resident((1, d)), _resident((d, 3 * dc)),
                  _resident((CONV_WIDTH, dc)), _resident((b, CONV_BUF * dc))],
        out_specs=[pl.BlockSpec((b, dc), lambda i: (0, 0)),
                   pl.BlockSpec((b, CONV_BUF * dc), lambda i: (0, 0))],
        out_shape=[jax.ShapeDtypeStruct((b, dc), BF16),
                   jax.ShapeDtypeStruct((b, CONV_BUF * dc), F32)],
        compiler_params=pltpu.CompilerParams(dimension_semantics=("arbitrary",),
                                             vmem_limit_bytes=VMEM_LIMIT_V7X),
        name="odd_mix_sample",
    )(x, g, w_in, conv_w, cstate_flat)


def _block_diag(w):
    g, c, _ = w.shape
    rows = [jnp.pad(w[i], ((0, 0), (i * c, (g - 1 - i) * c))) for i in range(g)]
    return jnp.concatenate(rows, axis=0)


def kernel(x_prompt, x_sample, state_pool, state_hgrn, state_conv, norm_mix, norm_mlp, norm_final, even_w_in, pool_w, pool_scale, hgrn_lb_logits, hgrn_gain, even_w_out, odd_w_in, conv_w, odd_w_out, ff_w1, ff_w2):
    depth = norm_mix.shape[0]
    b, t, d = x_prompt.shape
    db, ds, _ = x_sample.shape
    assert ds == 1, "the sample group carries one token per sequence"
    xp = x_prompt.reshape(b * t, d)
    xs = x_sample.reshape(db, d)
    g_mlp = norm_mlp[:, None, :]
    pool_p, hgrn_p, conv_p, pool_s, hgrn_s, conv_s = [], [], [], [], [], []
    for l in range(depth):
        g_mix = norm_mix[l][None]
        if l % 2 == 0:
            e = l // 2
            w_in = even_w_in[e].astype(BF16)
            pool_bd = _block_diag(pool_w[e]).astype(BF16)
            shared = (pool_bd, pool_scale[e][None], hgrn_lb_logits, hgrn_gain[e][None])
            mixed_p, p_new, s_new = even_mix_prompt(xp.reshape(b, t, d), g_mix, w_in, *shared, layer=l, tb=512)
            pool_p.append(p_new)
            hgrn_p.append(s_new)
            proj_s = norm_proj(xs, g_mix, w_in, bm=db)
            mixed_s, p_new, s_new = even_mix_sample(proj_s, state_pool[e].reshape(db, -1), state_hgrn[e], *shared,
                                                    layer=l, pos0=PAST_LEN, bb=8)
            pool_s.append(p_new.reshape(db, POOL_BUF, -1))
            hgrn_s.append(s_new)
            w_out = even_w_out[e]
        else:
            o = l // 2
            mixed_p, c_new = odd_mix_prompt(xp.reshape(b, t, d), g_mix, odd_w_in[o], conv_w[o], tb=512)
            conv_p.append(c_new)
            mixed_s, c_new = odd_mix_sample(xs, g_mix, odd_w_in[o], conv_w[o], state_conv[o].reshape(db, -1))
            conv_s.append(c_new.reshape(db, CONV_BUF, -1))
            w_out = odd_w_out[o]
        xp, xs = out_mlp(mixed_p.reshape(b * t, d), xp, mixed_s, xs, w_out, g_mlp, ff_w1, ff_w2, norm_final[None],
                         layer=l, final_norm=l == depth - 1, bm=512)
    return (xp.reshape(b, t, d), xs.reshape(db, ds, d), jnp.stack(pool_p), jnp.stack(hgrn_p), jnp.stack(conv_p),
            jnp.stack(pool_s), jnp.stack(hgrn_s), jnp.stack(conv_s))
```

```python
import functools

import jax
import jax.numpy as jnp
from jax import lax
from jax.experimental import pallas as pl
from jax.experimental.pallas import tpu as pltpu

F32 = jnp.float32
BF16 = jnp.bfloat16

EPS = 1e-6
PAST_LEN = 16384
POOL_WINDOWS = (2, 4, 8, 16)
POOL_GROUP_DIM = 64
POOL_BUF = max(POOL_WINDOWS) - 1
POOL_HIST = 16
POOL_PAD = 8
LANES = 128
HGRN_CHUNK = 32
HGRN_BLOCK = 128
HEAD = 128
CONV_WIDTH = 3
CONV_BUF = CONV_WIDTH - 1
CONV_HIST = 8

VMEM_LIMIT_V7X = 56 * 1024 * 1024
MXU_COLS_V7X = 256


def _resident(shape):
    nd = len(shape)
    return pl.BlockSpec(shape, lambda *_: (0,) * nd, pipeline_mode=pl.Buffered(1))


def _layer_block(stacked, layer):
    nd = stacked.ndim
    return pl.BlockSpec((None,) + stacked.shape[1:], lambda *_: (layer,) + (0,) * (nd - 1),
                        pipeline_mode=pl.Buffered(1))


def _cast_specs(weights, n_steps):
    in_specs, out_specs, out_shapes = [], [], []
    for w, layer in weights:
        r, c = w.shape[-2:]
        assert r % (n_steps * 16) == 0, "row slabs must be whole bf16 sublane tiles"
        if layer is None:
            in_specs.append(pl.BlockSpec((r // n_steps, c), lambda s: (s, 0)))
        else:
            in_specs.append(pl.BlockSpec((None, r // n_steps, c), lambda s, layer=layer: (layer, s, 0)))
        out_specs.append(pl.BlockSpec((r // n_steps, c), lambda s: (s, 0)))
        out_shapes.append(jax.ShapeDtypeStruct((r, c), BF16))
    return in_specs, out_specs, out_shapes


def _rms(x, g):
    return x * lax.rsqrt(jnp.mean(x * x, axis=-1, keepdims=True) + EPS) * g


def _sigmoid(x):
    return 1.0 / (1.0 + jnp.exp(-x))


def _lower_bound(lbl, layer):
    e = jnp.exp(lbl - jnp.max(lbl, axis=0, keepdims=True))
    p = e / jnp.sum(e, axis=0, keepdims=True)
    return jnp.sum(p[0:layer + 1], axis=0, keepdims=True)


def _pool_select(snaps, u_shape):
    grp = lax.broadcasted_iota(jnp.int32, u_shape, 1) // POOL_GROUP_DIM
    s = snaps[POOL_WINDOWS[-1]]
    for g in range(len(POOL_WINDOWS) - 2, -1, -1):
        s = jnp.where(grp == g, snaps[POOL_WINDOWS[g]], s)
    win = jnp.left_shift(2, grp)
    return s, win


def _norm_proj_kernel(x_ref, g_ref, w_ref, o_ref):
    h = _rms(x_ref[...], g_ref[...]).astype(BF16)
    o_ref[...] = jnp.dot(h, w_ref[...], preferred_element_type=F32)


def norm_proj(x, g, w, *, bm):
    m, d = x.shape
    n = w.shape[1]
    return pl.pallas_call(
        _norm_proj_kernel,
        grid=(m // bm,),
        in_specs=[pl.BlockSpec((bm, d), lambda i: (i, 0)), _resident((1, d)), _resident((d, n))],
        out_specs=pl.BlockSpec((bm, n), lambda i: (i, 0)),
        out_shape=jax.ShapeDtypeStruct((m, n), F32),
        compiler_params=pltpu.CompilerParams(dimension_semantics=("arbitrary",),
                                             vmem_limit_bytes=VMEM_LIMIT_V7X),
        name="norm_proj",
    )(x, g, w)


def _out_mlp_kernel(*refs, n_cast, final_norm, ff_chunk):
    m_ref, x_ref, ms_ref, xs_ref, wout_ref, g_ref, w1_ref, w2_ref, gf_ref = refs[:9]
    cast_in, (o_ref, os_ref), cast_out = refs[9:9 + n_cast], refs[9 + n_cast:11 + n_cast], refs[11 + n_cast:]
    d_ff = w1_ref.shape[1]
    for src_ref, dst_ref in zip(cast_in, cast_out):
        dst_ref[...] = src_ref[...].astype(BF16)

    def block(mixed, x):
        y = x + jnp.dot(mixed, wout_ref[...], preferred_element_type=F32)
        h = _rms(y, g_ref[...]).astype(BF16)
        acc = y
        for c in range(d_ff // ff_chunk):
            a = jnp.dot(h, w1_ref[:, c * ff_chunk:(c + 1) * ff_chunk], preferred_element_type=F32)
            a = jnp.square(jnp.maximum(a, 0.0)).astype(BF16)
            acc = acc + jnp.dot(a, w2_ref[c * ff_chunk:(c + 1) * ff_chunk, :], preferred_element_type=F32)
        return _rms(acc, gf_ref[...]) if final_norm else acc

    o_ref[...] = block(m_ref[...], x_ref[...])

    @pl.when(pl.program_id(0) == pl.num_programs(0) - 1)
    def _():
        os_ref[...] = block(ms_ref[...], xs_ref[...])


def out_mlp(mixed, x, mixed_s, x_s, w_out, g_mlp, w1, w2, g_final, to_cast, *, layer, final_norm, bm):
    m, d = x.shape
    ms = x_s.shape[0]
    d_ff = w1.shape[1]
    steps = m // bm
    cast_in, cast_out, cast_shapes = _cast_specs(to_cast, steps)
    kern = functools.partial(_out_mlp_kernel, n_cast=len(to_cast), final_norm=final_norm, ff_chunk=1024)
    return pl.pallas_call(
        kern,
        grid=(steps,),
        in_specs=[pl.BlockSpec((bm, d), lambda i: (i, 0)), pl.BlockSpec((bm, d), lambda i: (i, 0)),
                  _resident((ms, d)), _resident((ms, d)),
                  _resident((d, d)), _layer_block(g_mlp, layer), _resident((d, d_ff)),
                  _resident((d_ff, d)), _resident((1, d))] + cast_in,
        out_specs=[pl.BlockSpec((bm, d), lambda i: (i, 0)), pl.BlockSpec((ms, d), lambda i: (0, 0))] + cast_out,
        out_shape=[jax.ShapeDtypeStruct((m, d), F32), jax.ShapeDtypeStruct((ms, d), F32)] + cast_shapes,
        compiler_params=pltpu.CompilerParams(dimension_semantics=("arbitrary",),
                                             vmem_limit_bytes=VMEM_LIMIT_V7X),
        name="out_mlp",
    )(mixed, x, mixed_s, x_s, w_out, g_mlp, w1, w2, g_final, *[w for w, _ in to_cast])


def _even_prompt_kernel(*refs, n_cast, layer, tb, tiles_per_seq):
    xn_ref, x0_ref, g_ref, win_ref, wpool_ref, pscale_ref, lbl_ref, gain_ref = refs[:8]
    cast_in, (mixed_ref, pstate_ref, hstate_ref) = refs[8:8 + n_cast], refs[8 + n_cast:11 + n_cast]
    cast_out = refs[11 + n_cast:11 + 2 * n_cast]
    (pu_ref, pq_ref, pf_ref, pi_ref, pg_ref, hn_ref, ext_ref, lvl_ref, st_ref, k_ref, lsplit_ref,
     bcum_ref, dec_ref, gate_ref, v_ref, qd_ref, kd_ref, ke_ref, qb_ref, kb_ref, qr_ref, kc_ref) = refs[11 + 2 * n_cast:]
    for src_ref, dst_ref in zip(cast_in, cast_out):
        dst_ref[...] = src_ref[...].astype(BF16)

    step = pl.program_id(0)
    d_pool = wpool_ref.shape[0]
    n_heads = st_ref.shape[0]
    d_hgrn = n_heads * HEAD
    nc = tb // HGRN_CHUNK
    nblk = tb // HGRN_BLOCK
    j = step % tiles_per_seq
    last = tiles_per_seq - 1
    o_q, o_f, o_i, o_g = d_pool, d_pool + d_hgrn, d_pool + 2 * d_hgrn, d_pool + 3 * d_hgrn
    sections = [(pu_ref, 0), (pq_ref, o_q), (pf_ref, o_f), (pi_ref, o_i), (pg_ref, o_g)]

    def chunks(dst_ref, col0):
        def make(c0, c1):
            def run():
                dst_ref[:, c0:c1] = jnp.dot(hn_ref[...], win_ref[:, col0 + c0:col0 + c1],
                                            preferred_element_type=F32)
            return run
        width = dst_ref.shape[1]
        return [make(c0, min(c0 + MXU_COLS_V7X, width)) for c0 in range(0, width, MXU_COLS_V7X)]

    @pl.when(step == 0)
    def _():
        hn_ref[...] = _rms(x0_ref[...], g_ref[...]).astype(BF16)
        for dst_ref, col0 in sections:
            for run in chunks(dst_ref, col0):
                run()

    r1_, r2_ = POOL_PAD + POOL_HIST, POOL_PAD + POOL_HIST + tb

    @pl.when(j == 0)
    def _():
        ext_ref[0:r1_, :] = jnp.zeros((r1_, d_pool), F32)
        lvl_ref[:, 0:POOL_PAD, :] = jnp.zeros((lvl_ref.shape[0], POOL_PAD, LANES), F32)
        st_ref[...] = jnp.zeros(st_ref.shape, F32)

    hn_ref[...] = _rms(xn_ref[...], g_ref[...]).astype(BF16)

    u = pu_ref[...]
    ext_ref[r1_:r2_, :] = u
    groups_per_tile = LANES // POOL_GROUP_DIM
    lane_grp = lax.broadcasted_iota(jnp.int32, (tb, LANES), 1) // POOL_GROUP_DIM
    pos = j * tb + lax.broadcasted_iota(jnp.int32, (tb, LANES), 0)
    means = []
    for lt in range(d_pool // LANES):
        wins = POOL_WINDOWS[lt * groups_per_tile:(lt + 1) * groups_per_tile]
        src, src_lanes, w, nbuf, got = ext_ref, slice(lt * LANES, (lt + 1) * LANES), 1, 0, {}
        while w < wins[-1]:
            new = src[POOL_PAD:r2_, src_lanes] + src[POOL_PAD - w:r2_ - w, src_lanes]
            w *= 2
            if w in wins:
                got[w] = new[POOL_HIST:]
            if w < wins[-1]:
                lvl_ref[nbuf, POOL_PAD:r2_, :] = new
                src, src_lanes, nbuf = lvl_ref.at[nbuf], slice(0, LANES), 1 - nbuf
        ssum, win = got[wins[-1]], jnp.full((tb, LANES), wins[-1], jnp.int32)
        for g_ in range(groups_per_tile - 2, -1, -1):
            ssum = jnp.where(lane_grp == g_, got[wins[g_]], ssum)
            win = jnp.where(lane_grp == g_, wins[g_], win)
        cnt = jnp.minimum(win, pos + 1).astype(F32)
        means.append(ssum / cnt)
    diff = (jnp.concatenate(means, axis=1) - u).astype(BF16)
    pool_out = jnp.dot(diff, wpool_ref[...], preferred_element_type=F32) * pscale_ref[...]
    mixed_ref[:, 0:d_pool] = pool_out.astype(BF16)

    @pl.when(j == last)
    def _():
        pstate_ref[...] = ext_ref[pl.ds(r2_ - POOL_BUF, POOL_BUF), :]

    ext_ref[POOL_PAD:r1_, :] = ext_ref[POOL_PAD + tb:r1_ + tb, :]
    for run in chunks(pu_ref, 0):
        run()

    lb = _lower_bound(lbl_ref[...], layer)
    for c in range(nc):
        rows = slice(c * HGRN_CHUNK, (c + 1) * HGRN_CHUNK)
        fz = pf_ref[rows, :]
        a = jnp.exp(-jnp.abs(fz))
        r = 1.0 / (1.0 + a)
        ar = a * r
        sig = jnp.where(fz >= 0, r, ar)
        sig_neg = jnp.where(fz >= 0, ar, r)
        logf = jnp.log(lb + (1.0 - lb) * sig)
        k_ref[rows, :] = (1.0 - lb) * sig_neg
        l_hi = logf.astype(BF16)
        r1 = logf - l_hi.astype(F32)
        l_mid = r1.astype(BF16)
        lsplit_ref[rows, 0:d_hgrn] = l_hi
        lsplit_ref[rows, d_hgrn:2 * d_hgrn] = l_mid
        lsplit_ref[rows, 2 * d_hgrn:3 * d_hgrn] = (r1 - l_mid.astype(F32)).astype(BF16)
        iz = pi_ref[rows, :]
        v_ref[rows, :] = (iz * _sigmoid(iz)).astype(BF16)
        gate_ref[rows, :] = gain_ref[...] * _sigmoid(pg_ref[rows, :])

    ri = lax.broadcasted_iota(jnp.int32, (tb, tb), 0)
    ci = lax.broadcasted_iota(jnp.int32, (tb, tb), 1)
    tri = ((ri // HGRN_CHUNK == ci // HGRN_CHUNK) & (ci <= ri)).astype(BF16)
    cs = jnp.dot(tri, lsplit_ref[...], preferred_element_type=F32)
    bcum_ref[...] = cs[:, 0:d_hgrn] + cs[:, d_hgrn:2 * d_hgrn] + cs[:, 2 * d_hgrn:3 * d_hgrn]
    for run in chunks(pf_ref, o_f) + chunks(pi_ref, o_i):
        run()

    nsub = HGRN_BLOCK // HGRN_CHUNK
    half = nsub // 2
    for b in range(nblk):
        blast = [bcum_ref[(b * nsub + i + 1) * HGRN_CHUNK - 1:(b * nsub + i + 1) * HGRN_CHUNK, :]
                 for i in range(nsub)]
        pre = [jnp.zeros_like(blast[0])]
        for i in range(nsub):
            pre.append(pre[-1] + blast[i])
        dec_ref[b:b + 1, :] = jnp.exp(pre[nsub])
        for i in range(nsub):
            c = b * nsub + i
            rows = slice(c * HGRN_CHUNK, (c + 1) * HGRN_CHUNK)
            bc = bcum_ref[rows, :]
            qd = pq_ref[rows, :] * jnp.exp(bc)
            kd = k_ref[rows, :] * jnp.exp(-bc)
            ke = kd * jnp.exp(blast[i])
            qd_ref[rows, :] = qd.astype(BF16)
            kd_ref[rows, :] = kd.astype(BF16)
            ke_ref[rows, :] = ke.astype(BF16)
            qb_ref[rows, :] = (qd * jnp.exp(pre[i])).astype(BF16)
            kb_ref[rows, :] = (ke * jnp.exp(pre[nsub] - pre[i + 1])).astype(BF16)
            if i >= half:
                qr_ref[rows, :] = (qd * jnp.exp(pre[i] - pre[half])).astype(BF16)
                kc_ref[rows, :] = jnp.zeros((HGRN_CHUNK, d_hgrn), BF16)
            else:
                kc_ref[rows, :] = (ke * jnp.exp(pre[half] - pre[i + 1])).astype(BF16)

    rb = lax.broadcasted_iota(jnp.int32, (HGRN_BLOCK, HGRN_BLOCK), 0)
    cb = lax.broadcasted_iota(jnp.int32, (HGRN_BLOCK, HGRN_BLOCK), 1)
    rsub, csub = rb // HGRN_CHUNK, cb // HGRN_CHUNK
    m_diag = (rsub == csub) & (cb <= rb)
    m_adj = (csub == rsub - 1) & (rsub != half)
    eye = rb == cb
    hb = HGRN_BLOCK // 2
    nt_dims = (((1,), (1,)), ((), ()))
    tn_dims = (((0,), (0,)), ((), ()))
    fillers = chunks(pq_ref, o_q) + chunks(pg_ref, o_g)

    def fill(n):
        for _ in range(min(n, len(fillers))):
            fillers.pop(0)()

    per_stage = -(-len(fillers) // (3 * nblk))
    for b in range(nblk):
        rows = slice(b * HGRN_BLOCK, (b + 1) * HGRN_BLOCK)
        far = slice(b * HGRN_BLOCK + hb, (b + 1) * HGRN_BLOCK)
        g12s, g3s = [], []
        for hd in range(n_heads):
            sl = slice(hd * HEAD, (hd + 1) * HEAD)
            kk = jnp.concatenate([kd_ref[rows, sl], ke_ref[rows, sl]], axis=0)
            g12s.append(lax.dot_general(qd_ref[rows, sl], kk, nt_dims, preferred_element_type=F32))
            g3s.append(lax.dot_general(qr_ref[far, sl], kc_ref[rows, sl], nt_dims, preferred_element_type=F32))
        fill(per_stage)
        outs = []
        for hd in range(n_heads):
            sl = slice(hd * HEAD, (hd + 1) * HEAD)
            g12 = g12s[hd]
            scores = jnp.where(m_diag, g12[:, 0:HGRN_BLOCK], jnp.where(m_adj, g12[:, HGRN_BLOCK:], 0.0))
            scores = jnp.concatenate([scores[0:hb], scores[hb:] + g3s[hd]], axis=0).astype(BF16)
            v_blk = v_ref[rows, sl]
            st = st_ref[hd]
            lhs = jnp.concatenate([scores, qb_ref[rows, sl]], axis=1)
            rhs = jnp.concatenate([v_blk, st.astype(BF16)], axis=0)
            outs.append(jnp.dot(lhs, rhs, preferred_element_type=F32))
            d_col = jnp.sum(jnp.where(eye, dec_ref[b:b + 1, sl], 0.0), axis=1, keepdims=True)
            st_ref[hd] = st * d_col + lax.dot_general(kb_ref[rows, sl], v_blk, tn_dims,
                                                      preferred_element_type=F32)
        fill(per_stage)
        for hd in range(n_heads):
            sl = slice(hd * HEAD, (hd + 1) * HEAD)
            o = outs[hd]
            o = o * lax.rsqrt(jnp.mean(o * o, axis=-1, keepdims=True) + EPS)
            mixed_ref[rows, d_pool + hd * HEAD:d_pool + (hd + 1) * HEAD] = (o * gate_ref[rows, sl]).astype(BF16)
        fill(per_stage)
    fill(len(fillers))

    @pl.when(j == last)
    def _():
        hstate_ref[...] = st_ref[...]


def even_mix_prompt(x, g, w_in, wpool_bd, pscale, lbl, gain, to_cast, *, layer, tb):
    b, t, d = x.shape
    n_in = w_in.shape[1]
    d_pool = wpool_bd.shape[0]
    d_hgrn = gain.shape[1]
    n_heads = d_hgrn // HEAD
    tps = t // tb
    n_tiles = b * tps
    cast_in, cast_out, cast_shapes = _cast_specs(to_cast, n_tiles)
    kern = functools.partial(_even_prompt_kernel, n_cast=len(to_cast), layer=layer, tb=tb, tiles_per_seq=tps)

    def next_tile(s):
        tile = jnp.minimum(s + 1, n_tiles - 1)
        return tile // tps, tile % tps

    assert d_pool % LANES == 0 and LANES % POOL_GROUP_DIM == 0 and tb % HGRN_BLOCK == 0
    return pl.pallas_call(
        kern,
        grid=(n_tiles,),
        in_specs=[pl.BlockSpec((None, tb, d), lambda s: (*next_tile(s), 0)),
                  pl.BlockSpec((None, tb, d), lambda s: (0, 0, 0)),
                  _resident((1, d)), _resident((d, n_in)), _resident((d_pool, d_pool)),
                  _resident((1, d_pool)), _resident(lbl.shape), _resident((1, d_hgrn))] + cast_in,
        out_specs=[pl.BlockSpec((None, tb, d), lambda s: (s // tps, s % tps, 0)),
                   pl.BlockSpec((None, POOL_BUF, d_pool), lambda s: (s // tps, 0, 0)),
                   pl.BlockSpec((None, n_heads, HEAD, HEAD), lambda s: (s // tps, 0, 0, 0))] + cast_out,
        out_shape=[jax.ShapeDtypeStruct((b, t, d), BF16),
                   jax.ShapeDtypeStruct((b, POOL_BUF, d_pool), F32),
                   jax.ShapeDtypeStruct((b, n_heads, HEAD, HEAD), F32)] + cast_shapes,
        scratch_shapes=[pltpu.VMEM((tb, d_pool), F32)]
                       + [pltpu.VMEM((tb, d_hgrn), F32)] * 4
                       + [pltpu.VMEM((tb, d), BF16),
                        pltpu.VMEM((POOL_PAD + POOL_HIST + tb, d_pool), F32),
                        pltpu.VMEM((2, POOL_PAD + POOL_HIST + tb, LANES), F32),
                        pltpu.VMEM((n_heads, HEAD, HEAD), F32),
                        pltpu.VMEM((tb, d_hgrn), F32),
                        pltpu.VMEM((tb, 3 * d_hgrn), BF16),
                        pltpu.VMEM((tb, d_hgrn), F32),
                        pltpu.VMEM((tb // HGRN_BLOCK, d_hgrn), F32),
                        pltpu.VMEM((tb, d_hgrn), F32)]
                       + [pltpu.VMEM((tb, d_hgrn), BF16)] * 8,
        compiler_params=pltpu.CompilerParams(dimension_semantics=("arbitrary",),
                                             vmem_limit_bytes=VMEM_LIMIT_V7X),
        name="even_mix_prompt",
    )(x, x, g, w_in, wpool_bd, pscale, lbl, gain, *[w for w, _ in to_cast])


def _even_sample_kernel(proj_ref, pool_ref, hst_ref, wpool_ref, pscale_ref, lbl_ref, gain_ref,
                        mixed_ref, npool_ref, nhst_ref, v_ref, o_ref, *, layer, pos0):
    bb = proj_ref.shape[0]
    d_pool = wpool_ref.shape[0]
    n_heads = hst_ref.shape[1]
    d_hgrn = n_heads * HEAD

    u = proj_ref[:, 0:d_pool]
    acc = u
    snaps = {}
    for s in range(1, POOL_WINDOWS[-1]):
        r0 = (POOL_BUF - s) * d_pool
        acc = acc + pool_ref[:, r0:r0 + d_pool]
        if s + 1 in POOL_WINDOWS:
            snaps[s + 1] = acc
    ssum, win = _pool_select(snaps, u.shape)
    cnt = jnp.minimum(win, pos0 + 1).astype(F32)
    diff = (ssum / cnt - u).astype(BF16)
    pool_out = jnp.dot(diff, wpool_ref[...], preferred_element_type=F32) * pscale_ref[...]
    mixed_ref[:, 0:d_pool] = pool_out.astype(BF16)
    npool_ref[:, 0:(POOL_BUF - 1) * d_pool] = pool_ref[:, d_pool:POOL_BUF * d_pool]
    npool_ref[:, (POOL_BUF - 1) * d_pool:POOL_BUF * d_pool] = u

    lb = _lower_bound(lbl_ref[...], layer)
    fz = proj_ref[:, d_pool + d_hgrn:d_pool + 2 * d_hgrn]
    iz = proj_ref[:, d_pool + 2 * d_hgrn:d_pool + 3 * d_hgrn]
    v_ref[...] = iz * _sigmoid(iz)
    fkq = (lb + (1.0 - lb) * _sigmoid(fz), (1.0 - lb) * _sigmoid(-fz), proj_ref[:, d_pool:d_pool + d_hgrn])

    def split3(x):
        hi = x.astype(BF16).astype(F32)
        r1 = x - hi
        mid = r1.astype(BF16).astype(F32)
        return jnp.concatenate([hi, mid, r1 - mid], axis=0)

    sel_r = lax.broadcasted_iota(jnp.int32, (3 * bb, bb * HEAD), 0) % bb
    sel_c = lax.broadcasted_iota(jnp.int32, (3 * bb, bb * HEAD), 1) // HEAD
    sel = (sel_r == sel_c).astype(F32)
    parts = [split3(x) for x in fkq]
    for hd in range(n_heads):
        sl = slice(hd * HEAD, (hd + 1) * HEAD)
        f_cols, k_cols, q_cols = [lax.dot_general(p[:, sl], sel, (((0,), (0,)), ((), ())),
                                                  preferred_element_type=F32) for p in parts]
        for i in range(bb):
            blk = slice(i * HEAD, (i + 1) * HEAD)
            s_new = f_cols[:, blk] * hst_ref[i, hd] + k_cols[:, blk] * v_ref[i:i + 1, sl]
            nhst_ref[i, hd] = s_new
            o_ref[i:i + 1, sl] = jnp.sum(q_cols[:, blk] * s_new, axis=0, keepdims=True)

    gz = proj_ref[:, d_pool + 3 * d_hgrn:d_pool + 4 * d_hgrn]
    gate = gain_ref[...] * _sigmoid(gz)
    for hd in range(n_heads):
        sl = slice(hd * HEAD, (hd + 1) * HEAD)
        o = o_ref[:, sl]
        o = o * lax.rsqrt(jnp.mean(o * o, axis=-1, keepdims=True) + EPS)
        mixed_ref[:, d_pool + hd * HEAD:d_pool + (hd + 1) * HEAD] = (o * gate[:, sl]).astype(BF16)


def even_mix_sample(proj, pool_flat, hstate, wpool_bd, pscale, lbl, gain, *, layer, pos0, bb):
    b, n_in = proj.shape
    d_pool = wpool_bd.shape[0]
    d_hgrn = gain.shape[1]
    n_heads = d_hgrn // HEAD
    d = d_pool + d_hgrn
    kern = functools.partial(_even_sample_kernel, layer=layer, pos0=pos0)
    return pl.pallas_call(
        kern,
        grid=(b // bb,),
        in_specs=[pl.BlockSpec((bb, n_in), lambda i: (i, 0)),
                  pl.BlockSpec((bb, POOL_BUF * d_pool), lambda i: (i, 0)),
                  pl.BlockSpec((bb, n_heads, HEAD, HEAD), lambda i: (i, 0, 0, 0)),
                  _resident((d_pool, d_pool)), _resident((1, d_pool)), _resident(lbl.shape),
                  _resident((1, d_hgrn))],
        out_specs=[pl.BlockSpec((bb, d), lambda i: (i, 0)),
                   pl.BlockSpec((bb, POOL_BUF * d_pool), lambda i: (i, 0)),
                   pl.BlockSpec((bb, n_heads, HEAD, HEAD), lambda i: (i, 0, 0, 0))],
        out_shape=[jax.ShapeDtypeStruct((b, d), BF16),
                   jax.ShapeDtypeStruct((b, POOL_BUF * d_pool), F32),
                   jax.ShapeDtypeStruct((b, n_heads, HEAD, HEAD), F32)],
        scratch_shapes=[pltpu.VMEM((bb, d_hgrn), F32)] * 2,
        compiler_params=pltpu.CompilerParams(dimension_semantics=("arbitrary",),
                                             vmem_limit_bytes=VMEM_LIMIT_V7X),
        name="even_mix_sample",
    )(proj, pool_flat, hstate, wpool_bd, pscale, lbl, gain)


def _odd_prompt_kernel(x_ref, g_ref, win_ref, cw_ref, mixed_ref, cstate_ref, zext_ref, *, tb):
    j = pl.program_id(1)
    last = pl.num_programs(1) - 1
    dc = cw_ref.shape[1]

    @pl.when(j == 0)
    def _():
        zext_ref[0:CONV_HIST, :] = jnp.zeros((CONV_HIST, dc), F32)

    h = _rms(x_ref[...], g_ref[...]).astype(BF16)
    cg = jnp.dot(h, win_ref[:, dc:2 * dc], preferred_element_type=F32)
    hv = jnp.dot(h, win_ref[:, 2 * dc:3 * dc], preferred_element_type=F32)
    z = cg * hv
    zext_ref[CONV_HIST:CONV_HIST + tb, :] = z
    conv = cw_ref[CONV_WIDTH - 1:CONV_WIDTH, :] * z
    for s in range(1, CONV_WIDTH):
        conv = conv + cw_ref[CONV_WIDTH - 1 - s:CONV_WIDTH - s, :] * zext_ref[pl.ds(CONV_HIST - s, tb), :]
    bg = jnp.dot(h, win_ref[:, 0:dc], preferred_element_type=F32)
    mixed_ref[...] = (bg * conv).astype(BF16)

    @pl.when(j == last)
    def _():
        cstate_ref[...] = zext_ref[pl.ds(CONV_HIST + tb - CONV_BUF, CONV_BUF), :]

    zext_ref[0:CONV_HIST, :] = zext_ref[tb:tb + CONV_HIST, :]


def odd_mix_prompt(x, g, w_in, conv_w, *, tb):
    b, t, d = x.shape
    dc = conv_w.shape[1]
    kern = functools.partial(_odd_prompt_kernel, tb=tb)
    return pl.pallas_call(
        kern,
        grid=(b, t // tb),
        in_specs=[pl.BlockSpec((None, tb, d), lambda i, j: (i, j, 0)),
                  _resident((1, d)), _resident((d, 3 * dc)), _resident((CONV_WIDTH, dc))],
        out_specs=[pl.BlockSpec((None, tb, dc), lambda i, j: (i, j, 0)),
                   pl.BlockSpec((None, CONV_BUF, dc), lambda i, j: (i, 0, 0))],
        out_shape=[jax.ShapeDtypeStruct((b, t, dc), BF16),
                   jax.ShapeDtypeStruct((b, CONV_BUF, dc), F32)],
        scratch_shapes=[pltpu.VMEM((CONV_HIST + tb, dc), F32)],
        compiler_params=pltpu.CompilerParams(dimension_semantics=("arbitrary", "arbitrary"),
                                             vmem_limit_bytes=VMEM_LIMIT_V7X),
        name="odd_mix_prompt",
    )(x, g, w_in, conv_w)


def _odd_sample_kernel(x_ref, g_ref, win_ref, cw_ref, cst_ref, mixed_ref, ncst_ref):
    dc = cw_ref.shape[1]
    h = _rms(x_ref[...], g_ref[...]).astype(BF16)
    cg = jnp.dot(h, win_ref[:, dc:2 * dc], preferred_element_type=F32)
    hv = jnp.dot(h, win_ref[:, 2 * dc:3 * dc], preferred_element_type=F32)
    z = cg * hv
    conv = cw_ref[CONV_WIDTH - 1:CONV_WIDTH, :] * z
    for s in range(1, CONV_WIDTH):
        r0 = (CONV_BUF - s) * dc
        conv = conv + cw_ref[CONV_WIDTH - 1 - s:CONV_WIDTH - s, :] * cst_ref[:, r0:r0 + dc]
    bg = jnp.dot(h, win_ref[:, 0:dc], preferred_element_type=F32)
    mixed_ref[...] = (bg * conv).astype(BF16)
    ncst_ref[:, 0:(CONV_BUF - 1) * dc] = cst_ref[:, dc:CONV_BUF * dc]
    ncst_ref[:, (CONV_BUF - 1) * dc:CONV_BUF * dc] = z


def odd_mix_sample(x, g, w_in, conv_w, cstate_flat):
    b, d = x.shape
    dc = conv_w.shape[1]
    return pl.pallas_call(
        _odd_sample_kernel,
        grid=(1,),
        in_specs=[_resident((b, d)), _resident((1, d)), _resident((d, 3 * dc)),
                  _resident((CONV_WIDTH, dc)), _resident((b, CONV_BUF * dc))],
        out_specs=[pl.BlockSpec((b, dc), lambda i: (0, 0)),
                   pl.BlockSpec((b, CONV_BUF * dc), lambda i: (0, 0))],
        out_shape=[jax.ShapeDtypeStruct((b, dc), BF16),
                   jax.ShapeDtypeStruct((b, CONV_BUF * dc), F32)],
        compiler_params=pltpu.CompilerParams(dimension_semantics=("arbitrary",),
                                             vmem_limit_bytes=VMEM_LIMIT_V7X),
        name="odd_mix_sample",
    )(x, g, w_in, conv_w, cstate_flat)


def _block_diag(w):
    g, c, _ = w.shape
    rows = [jnp.pad(w[i], ((0, 0), (i * c, (g - 1 - i) * c))) for i in range(g)]
    return jnp.concatenate(rows, axis=0)


def kernel(x_prompt, x_sample, state_pool, state_hgrn, state_conv, norm_mix, norm_mlp, norm_final, even_w_in, pool_w, pool_scale, hgrn_lb_logits, hgrn_gain, even_w_out, odd_w_in, conv_w, odd_w_out, ff_w1, ff_w2):
    depth = norm_mix.shape[0]
    b, t, d = x_prompt.shape
    db, ds, _ = x_sample.shape
    assert ds == 1, "the sample group carries one token per sequence"
    xp = x_prompt.reshape(b * t, d)
    xs = x_sample.reshape(db, d)
    g_mlp = norm_mlp[:, None, :]
    pool_p, hgrn_p, conv_p, pool_s, hgrn_s, conv_s = [], [], [], [], [], []

    def layer_weights(l):
        w_in, w_out = (even_w_in, even_w_out) if l % 2 == 0 else (odd_w_in, odd_w_out)
        return dict(w_in=(w_in[l // 2], None), w_out=(w_out[l // 2], None), w1=(ff_w1, l), w2=(ff_w2, l))

    def cast_now(w, layer):
        return (w if layer is None else w[layer]).astype(BF16)

    have = {}
    for l in range(depth):
        g_mix = norm_mix[l][None]
        wl = layer_weights(l)
        if "w_in" not in have:
            have["w_in"] = cast_now(*wl["w_in"])
        missing = [k for k in ("w_out", "w1", "w2") if k not in have]
        if l % 2 == 0:
            e = l // 2
            pool_bd = _block_diag(pool_w[e]).astype(BF16)
            shared = (pool_bd, pool_scale[e][None], hgrn_lb_logits, hgrn_gain[e][None])
            mixed_p, p_new, s_new, *cast = even_mix_prompt(xp.reshape(b, t, d), g_mix, have["w_in"], *shared,
                                                           [wl[k] for k in missing], layer=l, tb=256)
            have.update(zip(missing, cast))
            pool_p.append(p_new)
            hgrn_p.append(s_new)
            proj_s = norm_proj(xs, g_mix, have["w_in"], bm=db)
            mixed_s, p_new, s_new = even_mix_sample(proj_s, state_pool[e].reshape(db, -1), state_hgrn[e], *shared,
                                                    layer=l, pos0=PAST_LEN, bb=8)
            pool_s.append(p_new.reshape(db, POOL_BUF, -1))
            hgrn_s.append(s_new)
        else:
            o = l // 2
            have.update({k: cast_now(*wl[k]) for k in missing})
            mixed_p, c_new = odd_mix_prompt(xp.reshape(b, t, d), g_mix, have["w_in"], conv_w[o], tb=512)
            conv_p.append(c_new)
            mixed_s, c_new = odd_mix_sample(xs, g_mix, have["w_in"], conv_w[o], state_conv[o].reshape(db, -1))
            conv_s.append(c_new.reshape(db, CONV_BUF, -1))
        nxt = layer_weights(l + 1) if l + 1 < depth else {}
        xp, xs, *cast = out_mlp(mixed_p.reshape(b * t, d), xp, mixed_s, xs, have["w_out"], g_mlp, have["w1"],
                                have["w2"], norm_final[None], list(nxt.values()),
                                layer=l, final_norm=l == depth - 1, bm=512)
        have = dict(zip(nxt, cast))
    return (xp.reshape(b, t, d), xs.reshape(db, ds, d), jnp.stack(pool_p), jnp.stack(hgrn_p), jnp.stack(conv_p),
            jnp.stack(pool_s), jnp.stack(hgrn_s), jnp.stack(conv_s))
```

```python
import functools

import jax
import jax.numpy as jnp
from jax import lax
from jax.experimental import pallas as pl
from jax.experimental.pallas import tpu as pltpu

F32 = jnp.float32
BF16 = jnp.bfloat16

EPS = 1e-6
PAST_LEN = 16384
POOL_WINDOWS = (2, 4, 8, 16)
POOL_GROUP_DIM = 64
POOL_BUF = max(POOL_WINDOWS) - 1
POOL_HIST = 16
POOL_PAD = 8
LANES = 128
HGRN_CHUNK = 32
HGRN_BLOCK = 128
HEAD = 128
CONV_WIDTH = 3
CONV_BUF = CONV_WIDTH - 1
CONV_HIST = 8

VMEM_LIMIT_V7X = 56 * 1024 * 1024
MXU_COLS_V7X = 256


def _resident(shape):
    nd = len(shape)
    return pl.BlockSpec(shape, lambda *_: (0,) * nd, pipeline_mode=pl.Buffered(1))


def _layer_block(stacked, layer):
    nd = stacked.ndim
    return pl.BlockSpec((None,) + stacked.shape[1:], lambda *_: (layer,) + (0,) * (nd - 1),
                        pipeline_mode=pl.Buffered(1))


def _cast_specs(weights, n_steps):
    in_specs, out_specs, out_shapes = [], [], []
    for w, layer in weights:
        r, c = w.shape[-2:]
        assert r % (n_steps * 16) == 0, "row slabs must be whole bf16 sublane tiles"
        if layer is None:
            in_specs.append(pl.BlockSpec((r // n_steps, c), lambda s: (s, 0)))
        else:
            in_specs.append(pl.BlockSpec((None, r // n_steps, c), lambda s, layer=layer: (layer, s, 0)))
        out_specs.append(pl.BlockSpec((r // n_steps, c), lambda s: (s, 0)))
        out_shapes.append(jax.ShapeDtypeStruct((r, c), BF16))
    return in_specs, out_specs, out_shapes


def _rms(x, g):
    return x * lax.rsqrt(jnp.mean(x * x, axis=-1, keepdims=True) + EPS) * g


def _sigmoid(x):
    return 1.0 / (1.0 + jnp.exp(-x))


def _lower_bound(lbl, layer):
    e = jnp.exp(lbl - jnp.max(lbl, axis=0, keepdims=True))
    p = e / jnp.sum(e, axis=0, keepdims=True)
    return jnp.sum(p[0:layer + 1], axis=0, keepdims=True)


def _pool_select(snaps, u_shape):
    grp = lax.broadcasted_iota(jnp.int32, u_shape, 1) // POOL_GROUP_DIM
    s = snaps[POOL_WINDOWS[-1]]
    for g in range(len(POOL_WINDOWS) - 2, -1, -1):
        s = jnp.where(grp == g, snaps[POOL_WINDOWS[g]], s)
    win = jnp.left_shift(2, grp)
    return s, win


def _norm_proj_kernel(x_ref, g_ref, w_ref, o_ref):
    h = _rms(x_ref[...], g_ref[...]).astype(BF16)
    o_ref[...] = jnp.dot(h, w_ref[...], preferred_element_type=F32)


def norm_proj(x, g, w, *, bm):
    m, d = x.shape
    n = w.shape[1]
    return pl.pallas_call(
        _norm_proj_kernel,
        grid=(m // bm,),
        in_specs=[pl.BlockSpec((bm, d), lambda i: (i, 0)), _resident((1, d)), _resident((d, n))],
        out_specs=pl.BlockSpec((bm, n), lambda i: (i, 0)),
        out_shape=jax.ShapeDtypeStruct((m, n), F32),
        compiler_params=pltpu.CompilerParams(dimension_semantics=("arbitrary",),
                                             vmem_limit_bytes=VMEM_LIMIT_V7X),
        name="norm_proj",
    )(x, g, w)


def _out_mlp_kernel(*refs, n_cast, layer, final_norm, ff_chunk):
    m_ref, x_ref, ms_ref, xs_ref, wout_ref, g_ref, w1_ref, w2_ref, gf_ref = refs[:9]
    cast_in, (o_ref, os_ref), cast_out = refs[9:9 + n_cast], refs[9 + n_cast:11 + n_cast], refs[11 + n_cast:]
    d_ff = w1_ref.shape[1]
    for src_ref, dst_ref in zip(cast_in, cast_out):
        dst_ref[...] = src_ref[...].astype(BF16)

    def block(mixed, x):
        y = x + jnp.dot(mixed, wout_ref[...], preferred_element_type=F32)
        h = _rms(y, g_ref[layer:layer + 1, :]).astype(BF16)
        acc = y
        for c in range(d_ff // ff_chunk):
            a = jnp.dot(h, w1_ref[:, c * ff_chunk:(c + 1) * ff_chunk], preferred_element_type=F32)
            a = jnp.square(jnp.maximum(a, 0.0)).astype(BF16)
            acc = acc + jnp.dot(a, w2_ref[c * ff_chunk:(c + 1) * ff_chunk, :], preferred_element_type=F32)
        return _rms(acc, gf_ref[...]) if final_norm else acc

    o_ref[...] = block(m_ref[...], x_ref[...])

    @pl.when(pl.program_id(0) == pl.num_programs(0) - 1)
    def _():
        os_ref[...] = block(ms_ref[...], xs_ref[...])


def out_mlp(mixed, x, mixed_s, x_s, w_out, g_mlp, w1, w2, g_final, to_cast, *, layer, final_norm, bm):
    m, d = x.shape
    ms = x_s.shape[0]
    d_ff = w1.shape[1]
    steps = m // bm
    cast_in, cast_out, cast_shapes = _cast_specs(to_cast, steps)
    kern = functools.partial(_out_mlp_kernel, n_cast=len(to_cast), layer=layer, final_norm=final_norm,
                             ff_chunk=1024)
    return pl.pallas_call(
        kern,
        grid=(steps,),
        in_specs=[pl.BlockSpec((bm, d), lambda i: (i, 0)), pl.BlockSpec((bm, d), lambda i: (i, 0)),
                  _resident((ms, d)), _resident((ms, d)),
                  _resident((d, d)), _resident(g_mlp.shape), _resident((d, d_ff)),
                  _resident((d_ff, d)), _resident((1, d))] + cast_in,
        out_specs=[pl.BlockSpec((bm, d), lambda i: (i, 0)), pl.BlockSpec((ms, d), lambda i: (0, 0))] + cast_out,
        out_shape=[jax.ShapeDtypeStruct((m, d), F32), jax.ShapeDtypeStruct((ms, d), F32)] + cast_shapes,
        compiler_params=pltpu.CompilerParams(dimension_semantics=("arbitrary",),
                                             vmem_limit_bytes=VMEM_LIMIT_V7X),
        name="out_mlp",
    )(mixed, x, mixed_s, x_s, w_out, g_mlp, w1, w2, g_final, *[w for w, _ in to_cast])


def _even_prompt_kernel(*refs, n_cast, layer, tb, tiles_per_seq):
    xn_ref, x0_ref, g_ref, win_ref, wpool_ref, pscale_ref, lbl_ref, gain_ref = refs[:8]
    cast_in, (mixed_ref, pstate_ref, hstate_ref) = refs[8:8 + n_cast], refs[8 + n_cast:11 + n_cast]
    cast_out = refs[11 + n_cast:11 + 2 * n_cast]
    (pu_ref, pq_ref, pf_ref, pi_ref, pg_ref, hn_ref, ext_ref, lvl_ref, st_ref, k_ref, lsplit_ref,
     bcum_ref, dec_ref, gate_ref, v_ref, qd_ref, kd_ref, ke_ref, qb_ref, kb_ref, qr_ref, kc_ref) = refs[11 + 2 * n_cast:]
    for src_ref, dst_ref in zip(cast_in, cast_out):
        dst_ref[...] = src_ref[...].astype(BF16)

    step = pl.program_id(0)
    d_pool = wpool_ref.shape[0]
    n_heads = st_ref.shape[0]
    d_hgrn = n_heads * HEAD
    nc = tb // HGRN_CHUNK
    nblk = tb // HGRN_BLOCK
    j = step % tiles_per_seq
    last = tiles_per_seq - 1
    o_q, o_f, o_i, o_g = d_pool, d_pool + d_hgrn, d_pool + 2 * d_hgrn, d_pool + 3 * d_hgrn
    sections = [(pu_ref, 0), (pq_ref, o_q), (pf_ref, o_f), (pi_ref, o_i), (pg_ref, o_g)]

    def chunks(dst_ref, col0):
        def make(c0, c1):
            def run():
                dst_ref[:, c0:c1] = jnp.dot(hn_ref[...], win_ref[:, col0 + c0:col0 + c1],
                                            preferred_element_type=F32)
            return run
        width = dst_ref.shape[1]
        return [make(c0, min(c0 + MXU_COLS_V7X, width)) for c0 in range(0, width, MXU_COLS_V7X)]

    @pl.when(step == 0)
    def _():
        hn_ref[...] = _rms(x0_ref[...], g_ref[...]).astype(BF16)
        for dst_ref, col0 in sections:
            for run in chunks(dst_ref, col0):
                run()

    r1_, r2_ = POOL_PAD + POOL_HIST, POOL_PAD + POOL_HIST + tb

    @pl.when(j == 0)
    def _():
        ext_ref[0:r1_, :] = jnp.zeros((r1_, d_pool), F32)
        lvl_ref[:, 0:POOL_PAD, :] = jnp.zeros((lvl_ref.shape[0], POOL_PAD, LANES), F32)
        st_ref[...] = jnp.zeros(st_ref.shape, F32)

    hn_ref[...] = _rms(xn_ref[...], g_ref[...]).astype(BF16)

    u = pu_ref[...]
    ext_ref[r1_:r2_, :] = u
    groups_per_tile = LANES // POOL_GROUP_DIM
    lane_grp = lax.broadcasted_iota(jnp.int32, (tb, LANES), 1) // POOL_GROUP_DIM
    pos = j * tb + lax.broadcasted_iota(jnp.int32, (tb, LANES), 0)
    means = []
    for lt in range(d_pool // LANES):
        wins = POOL_WINDOWS[lt * groups_per_tile:(lt + 1) * groups_per_tile]
        src, src_lanes, w, nbuf, got = ext_ref, slice(lt * LANES, (lt + 1) * LANES), 1, 0, {}
        while w < wins[-1]:
            new = src[POOL_PAD:r2_, src_lanes] + src[POOL_PAD - w:r2_ - w, src_lanes]
            w *= 2
            if w in wins:
                got[w] = new[POOL_HIST:]
            if w < wins[-1]:
                lvl_ref[nbuf, POOL_PAD:r2_, :] = new
                src, src_lanes, nbuf = lvl_ref.at[nbuf], slice(0, LANES), 1 - nbuf
        ssum, win = got[wins[-1]], jnp.full((tb, LANES), wins[-1], jnp.int32)
        for g_ in range(groups_per_tile - 2, -1, -1):
            ssum = jnp.where(lane_grp == g_, got[wins[g_]], ssum)
            win = jnp.where(lane_grp == g_, wins[g_], win)
        cnt = jnp.minimum(win, pos + 1).astype(F32)
        means.append(ssum / cnt)
    diff = (jnp.concatenate(means, axis=1) - u).astype(BF16)
    pool_out = jnp.dot(diff, wpool_ref[...], preferred_element_type=F32) * pscale_ref[...]
    mixed_ref[:, 0:d_pool] = pool_out.astype(BF16)

    @pl.when(j == last)
    def _():
        pstate_ref[...] = ext_ref[pl.ds(r2_ - POOL_BUF, POOL_BUF), :]

    ext_ref[POOL_PAD:r1_, :] = ext_ref[POOL_PAD + tb:r1_ + tb, :]
    for run in chunks(pu_ref, 0):
        run()

    lb = _lower_bound(lbl_ref[...], layer)
    for c in range(nc):
        rows = slice(c * HGRN_CHUNK, (c + 1) * HGRN_CHUNK)
        fz = pf_ref[rows, :]
        a = jnp.exp(-jnp.abs(fz))
        r = 1.0 / (1.0 + a)
        ar = a * r
        sig = jnp.where(fz >= 0, r, ar)
        sig_neg = jnp.where(fz >= 0, ar, r)
        logf = jnp.log(lb + (1.0 - lb) * sig)
        k_ref[rows, :] = (1.0 - lb) * sig_neg
        l_hi = logf.astype(BF16)
        r1 = logf - l_hi.astype(F32)
        l_mid = r1.astype(BF16)
        lsplit_ref[rows, 0:d_hgrn] = l_hi
        lsplit_ref[rows, d_hgrn:2 * d_hgrn] = l_mid
        lsplit_ref[rows, 2 * d_hgrn:3 * d_hgrn] = (r1 - l_mid.astype(F32)).astype(BF16)
        iz = pi_ref[rows, :]
        v_ref[rows, :] = (iz * _sigmoid(iz)).astype(BF16)
        gate_ref[rows, :] = gain_ref[...] * _sigmoid(pg_ref[rows, :])

    ri = lax.broadcasted_iota(jnp.int32, (tb, tb), 0)
    ci = lax.broadcasted_iota(jnp.int32, (tb, tb), 1)
    tri = ((ri // HGRN_CHUNK == ci // HGRN_CHUNK) & (ci <= ri)).astype(BF16)
    cs = jnp.dot(tri, lsplit_ref[...], preferred_element_type=F32)
    bcum_ref[...] = cs[:, 0:d_hgrn] + cs[:, d_hgrn:2 * d_hgrn] + cs[:, 2 * d_hgrn:3 * d_hgrn]
    for run in chunks(pf_ref, o_f) + chunks(pi_ref, o_i):
        run()

    nsub = HGRN_BLOCK // HGRN_CHUNK
    half = nsub // 2
    for b in range(nblk):
        blast = [bcum_ref[(b * nsub + i + 1) * HGRN_CHUNK - 1:(b * nsub + i + 1) * HGRN_CHUNK, :]
                 for i in range(nsub)]
        pre = [jnp.zeros_like(blast[0])]
        for i in range(nsub):
            pre.append(pre[-1] + blast[i])
        dec_ref[b:b + 1, :] = jnp.exp(pre[nsub])
        for i in range(nsub):
            c = b * nsub + i
            rows = slice(c * HGRN_CHUNK, (c + 1) * HGRN_CHUNK)
            bc = bcum_ref[rows, :]
            qd = pq_ref[rows, :] * jnp.exp(bc)
            kd = k_ref[rows, :] * jnp.exp(-bc)
            ke = kd * jnp.exp(blast[i])
            qd_ref[rows, :] = qd.astype(BF16)
            kd_ref[rows, :] = kd.astype(BF16)
            ke_ref[rows, :] = ke.astype(BF16)
            qb_ref[rows, :] = (qd * jnp.exp(pre[i])).astype(BF16)
            kb_ref[rows, :] = (ke * jnp.exp(pre[nsub] - pre[i + 1])).astype(BF16)
            if i >= half:
                qr_ref[rows, :] = (qd * jnp.exp(pre[i] - pre[half])).astype(BF16)
                kc_ref[rows, :] = jnp.zeros((HGRN_CHUNK, d_hgrn), BF16)
            else:
                kc_ref[rows, :] = (ke * jnp.exp(pre[half] - pre[i + 1])).astype(BF16)

    rb = lax.broadcasted_iota(jnp.int32, (HGRN_BLOCK, HGRN_BLOCK), 0)
    cb = lax.broadcasted_iota(jnp.int32, (HGRN_BLOCK, HGRN_BLOCK), 1)
    rsub, csub = rb // HGRN_CHUNK, cb // HGRN_CHUNK
    m_diag = (rsub == csub) & (cb <= rb)
    m_adj = (csub == rsub - 1) & (rsub != half)
    eye = rb == cb
    hb = HGRN_BLOCK // 2
    nt_dims = (((1,), (1,)), ((), ()))
    tn_dims = (((0,), (0,)), ((), ()))
    fillers = chunks(pq_ref, o_q) + chunks(pg_ref, o_g)

    def fill(n):
        for _ in range(min(n, len(fillers))):
            fillers.pop(0)()

    per_stage = -(-len(fillers) // (3 * nblk))
    for b in range(nblk):
        rows = slice(b * HGRN_BLOCK, (b + 1) * HGRN_BLOCK)
        far = slice(b * HGRN_BLOCK + hb, (b + 1) * HGRN_BLOCK)
        g12s, g3s = [], []
        for hd in range(n_heads):
            sl = slice(hd * HEAD, (hd + 1) * HEAD)
            kk = jnp.concatenate([kd_ref[rows, sl], ke_ref[rows, sl]], axis=0)
            g12s.append(lax.dot_general(qd_ref[rows, sl], kk, nt_dims, preferred_element_type=F32))
            g3s.append(lax.dot_general(qr_ref[far, sl], kc_ref[rows, sl], nt_dims, preferred_element_type=F32))
        fill(per_stage)
        outs = []
        for hd in range(n_heads):
            sl = slice(hd * HEAD, (hd + 1) * HEAD)
            g12 = g12s[hd]
            scores = jnp.where(m_diag, g12[:, 0:HGRN_BLOCK], jnp.where(m_adj, g12[:, HGRN_BLOCK:], 0.0))
            scores = jnp.concatenate([scores[0:hb], scores[hb:] + g3s[hd]], axis=0).astype(BF16)
            v_blk = v_ref[rows, sl]
            st = st_ref[hd]
            lhs = jnp.concatenate([scores, qb_ref[rows, sl]], axis=1)
            rhs = jnp.concatenate([v_blk, st.astype(BF16)], axis=0)
            outs.append(jnp.dot(lhs, rhs, preferred_element_type=F32))
            d_col = jnp.sum(jnp.where(eye, dec_ref[b:b + 1, sl], 0.0), axis=1, keepdims=True)
            st_ref[hd] = st * d_col + lax.dot_general(kb_ref[rows, sl], v_blk, tn_dims,
                                                      preferred_element_type=F32)
        fill(per_stage)
        for hd in range(n_heads):
            sl = slice(hd * HEAD, (hd + 1) * HEAD)
            o = outs[hd]
            o = o * lax.rsqrt(jnp.mean(o * o, axis=-1, keepdims=True) + EPS)
            mixed_ref[rows, d_pool + hd * HEAD:d_pool + (hd + 1) * HEAD] = (o * gate_ref[rows, sl]).astype(BF16)
        fill(per_stage)
    fill(len(fillers))

    @pl.when(j == last)
    def _():
        hstate_ref[...] = st_ref[...]


def even_mix_prompt(x, g, w_in, wpool_bd, pscale, lbl, gain, to_cast, *, layer, tb):
    b, t, d = x.shape
    n_in = w_in.shape[1]
    d_pool = wpool_bd.shape[0]
    d_hgrn = gain.shape[1]
    n_heads = d_hgrn // HEAD
    tps = t // tb
    n_tiles = b * tps
    cast_in, cast_out, cast_shapes = _cast_specs(to_cast, n_tiles)
    kern = functools.partial(_even_prompt_kernel, n_cast=len(to_cast), layer=layer, tb=tb, tiles_per_seq=tps)

    def next_tile(s):
        tile = jnp.minimum(s + 1, n_tiles - 1)
        return tile // tps, tile % tps

    assert d_pool % LANES == 0 and LANES % POOL_GROUP_DIM == 0 and tb % HGRN_BLOCK == 0
    return pl.pallas_call(
        kern,
        grid=(n_tiles,),
        in_specs=[pl.BlockSpec((None, tb, d), lambda s: (*next_tile(s), 0)),
                  pl.BlockSpec((None, tb, d), lambda s: (0, 0, 0)),
                  _resident((1, d)), _resident((d, n_in)), _resident((d_pool, d_pool)),
                  _resident((1, d_pool)), _resident(lbl.shape), _resident((1, d_hgrn))] + cast_in,
        out_specs=[pl.BlockSpec((None, tb, d), lambda s: (s // tps, s % tps, 0)),
                   pl.BlockSpec((None, POOL_BUF, d_pool), lambda s: (s // tps, 0, 0)),
                   pl.BlockSpec((None, n_heads, HEAD, HEAD), lambda s: (s // tps, 0, 0, 0))] + cast_out,
        out_shape=[jax.ShapeDtypeStruct((b, t, d), BF16),
                   jax.ShapeDtypeStruct((b, POOL_BUF, d_pool), F32),
                   jax.ShapeDtypeStruct((b, n_heads, HEAD, HEAD), F32)] + cast_shapes,
        scratch_shapes=[pltpu.VMEM((tb, d_pool), F32)]
                       + [pltpu.VMEM((tb, d_hgrn), F32)] * 4
                       + [pltpu.VMEM((tb, d), BF16),
                        pltpu.VMEM((POOL_PAD + POOL_HIST + tb, d_pool), F32),
                        pltpu.VMEM((2, POOL_PAD + POOL_HIST + tb, LANES), F32),
                        pltpu.VMEM((n_heads, HEAD, HEAD), F32),
                        pltpu.VMEM((tb, d_hgrn), F32),
                        pltpu.VMEM((tb, 3 * d_hgrn), BF16),
                        pltpu.VMEM((tb, d_hgrn), F32),
                        pltpu.VMEM((tb // HGRN_BLOCK, d_hgrn), F32),
                        pltpu.VMEM((tb, d_hgrn), F32)]
                       + [pltpu.VMEM((tb, d_hgrn), BF16)] * 8,
        compiler_params=pltpu.CompilerParams(dimension_semantics=("arbitrary",),
                                             vmem_limit_bytes=VMEM_LIMIT_V7X),
        name="even_mix_prompt",
    )(x, x, g, w_in, wpool_bd, pscale, lbl, gain, *[w for w, _ in to_cast])


def _even_sample_kernel(proj_ref, pool_ref, hst_ref, wpool_ref, pscale_ref, lbl_ref, gain_ref,
                        mixed_ref, npool_ref, nhst_ref, v_ref, o_ref, *, layer, pos0):
    bb = proj_ref.shape[0]
    d_pool = wpool_ref.shape[0]
    n_heads = hst_ref.shape[1]
    d_hgrn = n_heads * HEAD

    u = proj_ref[:, 0:d_pool]
    acc = u
    snaps = {}
    for s in range(1, POOL_WINDOWS[-1]):
        acc = acc + pool_ref[POOL_BUF - s]
        if s + 1 in POOL_WINDOWS:
            snaps[s + 1] = acc
    ssum, win = _pool_select(snaps, u.shape)
    cnt = jnp.minimum(win, pos0 + 1).astype(F32)
    diff = (ssum / cnt - u).astype(BF16)
    pool_out = jnp.dot(diff, wpool_ref[...], preferred_element_type=F32) * pscale_ref[...]
    mixed_ref[:, 0:d_pool] = pool_out.astype(BF16)
    npool_ref[0:POOL_BUF - 1] = pool_ref[1:POOL_BUF]
    npool_ref[POOL_BUF - 1] = u

    lb = _lower_bound(lbl_ref[...], layer)
    fz = proj_ref[:, d_pool + d_hgrn:d_pool + 2 * d_hgrn]
    iz = proj_ref[:, d_pool + 2 * d_hgrn:d_pool + 3 * d_hgrn]
    v_ref[...] = iz * _sigmoid(iz)
    fkq = (lb + (1.0 - lb) * _sigmoid(fz), (1.0 - lb) * _sigmoid(-fz), proj_ref[:, d_pool:d_pool + d_hgrn])

    def split3(x):
        hi = x.astype(BF16).astype(F32)
        r1 = x - hi
        mid = r1.astype(BF16).astype(F32)
        return jnp.concatenate([hi, mid, r1 - mid], axis=0)

    sel_r = lax.broadcasted_iota(jnp.int32, (3 * bb, bb * HEAD), 0) % bb
    sel_c = lax.broadcasted_iota(jnp.int32, (3 * bb, bb * HEAD), 1) // HEAD
    sel = (sel_r == sel_c).astype(F32)
    parts = [split3(x) for x in fkq]
    for hd in range(n_heads):
        sl = slice(hd * HEAD, (hd + 1) * HEAD)
        f_cols, k_cols, q_cols = [lax.dot_general(p[:, sl], sel, (((0,), (0,)), ((), ())),
                                                  preferred_element_type=F32) for p in parts]
        for i in range(bb):
            blk = slice(i * HEAD, (i + 1) * HEAD)
            s_new = f_cols[:, blk] * hst_ref[i, hd] + k_cols[:, blk] * v_ref[i:i + 1, sl]
            nhst_ref[i, hd] = s_new
            o_ref[i:i + 1, sl] = jnp.sum(q_cols[:, blk] * s_new, axis=0, keepdims=True)

    gz = proj_ref[:, d_pool + 3 * d_hgrn:d_pool + 4 * d_hgrn]
    gate = gain_ref[...] * _sigmoid(gz)
    for hd in range(n_heads):
        sl = slice(hd * HEAD, (hd + 1) * HEAD)
        o = o_ref[:, sl]
        o = o * lax.rsqrt(jnp.mean(o * o, axis=-1, keepdims=True) + EPS)
        mixed_ref[:, d_pool + hd * HEAD:d_pool + (hd + 1) * HEAD] = (o * gate[:, sl]).astype(BF16)


def even_mix_sample(proj, pool_rows, hstate, wpool_bd, pscale, lbl, gain, *, layer, pos0, bb):
    b, n_in = proj.shape
    d_pool = wpool_bd.shape[0]
    d_hgrn = gain.shape[1]
    n_heads = d_hgrn // HEAD
    d = d_pool + d_hgrn
    kern = functools.partial(_even_sample_kernel, layer=layer, pos0=pos0)
    return pl.pallas_call(
        kern,
        grid=(b // bb,),
        in_specs=[pl.BlockSpec((bb, n_in), lambda i: (i, 0)),
                  pl.BlockSpec((POOL_BUF, bb, d_pool), lambda i: (0, i, 0)),
                  pl.BlockSpec((bb, n_heads, HEAD, HEAD), lambda i: (i, 0, 0, 0)),
                  _resident((d_pool, d_pool)), _resident((1, d_pool)), _resident(lbl.shape),
                  _resident((1, d_hgrn))],
        out_specs=[pl.BlockSpec((bb, d), lambda i: (i, 0)),
                   pl.BlockSpec((POOL_BUF, bb, d_pool), lambda i: (0, i, 0)),
                   pl.BlockSpec((bb, n_heads, HEAD, HEAD), lambda i: (i, 0, 0, 0))],
        out_shape=[jax.ShapeDtypeStruct((b, d), BF16),
                   jax.ShapeDtypeStruct((POOL_BUF, b, d_pool), F32),
                   jax.ShapeDtypeStruct((b, n_heads, HEAD, HEAD), F32)],
        scratch_shapes=[pltpu.VMEM((bb, d_hgrn), F32)] * 2,
        compiler_params=pltpu.CompilerParams(dimension_semantics=("arbitrary",),
                                             vmem_limit_bytes=VMEM_LIMIT_V7X),
        name="even_mix_sample",
    )(proj, pool_rows, hstate, wpool_bd, pscale, lbl, gain)


def _odd_prompt_kernel(x_ref, g_ref, win_ref, cw_ref, mixed_ref, cstate_ref, zext_ref, *, tb):
    j = pl.program_id(1)
    last = pl.num_programs(1) - 1
    dc = cw_ref.shape[1]

    @pl.when(j == 0)
    def _():
        zext_ref[0:CONV_HIST, :] = jnp.zeros((CONV_HIST, dc), F32)

    h = _rms(x_ref[...], g_ref[...]).astype(BF16)
    cg = jnp.dot(h, win_ref[:, dc:2 * dc], preferred_element_type=F32)
    hv = jnp.dot(h, win_ref[:, 2 * dc:3 * dc], preferred_element_type=F32)
    z = cg * hv
    zext_ref[CONV_HIST:CONV_HIST + tb, :] = z
    conv = cw_ref[CONV_WIDTH - 1:CONV_WIDTH, :] * z
    for s in range(1, CONV_WIDTH):
        conv = conv + cw_ref[CONV_WIDTH - 1 - s:CONV_WIDTH - s, :] * zext_ref[pl.ds(CONV_HIST - s, tb), :]
    bg = jnp.dot(h, win_ref[:, 0:dc], preferred_element_type=F32)
    mixed_ref[...] = (bg * conv).astype(BF16)

    @pl.when(j == last)
    def _():
        cstate_ref[...] = zext_ref[pl.ds(CONV_HIST + tb - CONV_BUF, CONV_BUF), :]

    zext_ref[0:CONV_HIST, :] = zext_ref[tb:tb + CONV_HIST, :]


def odd_mix_prompt(x, g, w_in, conv_w, *, tb):
    b, t, d = x.shape
    dc = conv_w.shape[1]
    kern = functools.partial(_odd_prompt_kernel, tb=tb)
    return pl.pallas_call(
        kern,
        grid=(b, t // tb),
        in_specs=[pl.BlockSpec((None, tb, d), lambda i, j: (i, j, 0)),
                  _resident((1, d)), _resident((d, 3 * dc)), _resident((CONV_WIDTH, dc))],
        out_specs=[pl.BlockSpec((None, tb, dc), lambda i, j: (i, j, 0)),
                   pl.BlockSpec((None, CONV_BUF, dc), lambda i, j: (i, 0, 0))],
        out_shape=[jax.ShapeDtypeStruct((b, t, dc), BF16),
                   jax.ShapeDtypeStruct((b, CONV_BUF, dc), F32)],
        scratch_shapes=[pltpu.VMEM((CONV_HIST + tb, dc), F32)],
        compiler_params=pltpu.CompilerParams(dimension_semantics=("arbitrary", "arbitrary"),
                                             vmem_limit_bytes=VMEM_LIMIT_V7X),
        name="odd_mix_prompt",
    )(x, g, w_in, conv_w)


def _odd_sample_kernel(x_ref, g_ref, win_ref, cw_ref, cst_ref, mixed_ref, ncst_ref):
    dc = cw_ref.shape[1]
    h = _rms(x_ref[...], g_ref[...]).astype(BF16)
    cg = jnp.dot(h, win_ref[:, dc:2 * dc], preferred_element_type=F32)
    hv = jnp.dot(h, win_ref[:, 2 * dc:3 * dc], preferred_element_type=F32)
    z = cg * hv
    conv = cw_ref[CONV_WIDTH - 1:CONV_WIDTH, :] * z
    for s in range(1, CONV_WIDTH):
        r0 = (CONV_BUF - s) * dc
        conv = conv + cw_ref[CONV_WIDTH - 1 - s:CONV_WIDTH - s, :] * cst_ref[:, r0:r0 + dc]
    bg = jnp.dot(h, win_ref[:, 0:dc], preferred_element_type=F32)
    mixed_ref[...] = (bg * conv).astype(BF16)
    ncst_ref[:, 0:(CONV_BUF - 1) * dc] = cst_ref[:, dc:CONV_BUF * dc]
    ncst_ref[:, (CONV_BUF - 1) * dc:CONV_BUF * dc] = z


def odd_mix_sample(x, g, w_in, conv_w, cstate_flat):
    b, d = x.shape
    dc = conv_w.shape[1]
    return pl.pallas_call(
        _odd_sample_kernel,
        grid=(1,),
        in_specs=[_resident((b, d)), _resident((1, d)), _resident((d, 3 * dc)),
                  _resident((CONV_WIDTH, dc)), _resident((b, CONV_BUF * dc))],
        out_specs=[pl.BlockSpec((b, dc), lambda i: (0, 0)),
                   pl.BlockSpec((b, CONV_BUF * dc), lambda i: (0, 0))],
        out_shape=[jax.ShapeDtypeStruct((b, dc), BF16),
                   jax.ShapeDtypeStruct((b, CONV_BUF * dc), F32)],
        compiler_params=pltpu.CompilerParams(dimension_semantics=("arbitrary",),
                                             vmem_limit_bytes=VMEM_LIMIT_V7X),
        name="odd_mix_sample",
    )(x, g, w_in, conv_w, cstate_flat)


def _block_diag(w):
    g, c, _ = w.shape
    rows = [jnp.pad(w[i], ((0, 0), (i * c, (g - 1 - i) * c))) for i in range(g)]
    return jnp.concatenate(rows, axis=0)


def kernel(x_prompt, x_sample, state_pool, state_hgrn, state_conv, norm_mix, norm_mlp, norm_final, even_w_in, pool_w, pool_scale, hgrn_lb_logits, hgrn_gain, even_w_out, odd_w_in, conv_w, odd_w_out, ff_w1, ff_w2):
    depth = norm_mix.shape[0]
    b, t, d = x_prompt.shape
    db, ds, _ = x_sample.shape
    assert ds == 1, "the sample group carries one token per sequence"
    xp = x_prompt.reshape(b * t, d)
    xs = x_sample.reshape(db, d)
    pool_p, hgrn_p, conv_p, pool_s, hgrn_s, conv_s = [], [], [], [], [], []

    def layer_weights(l):
        w_in, w_out = (even_w_in, even_w_out) if l % 2 == 0 else (odd_w_in, odd_w_out)
        return dict(w_in=(w_in[l // 2], None), w_out=(w_out[l // 2], None), w1=(ff_w1, l), w2=(ff_w2, l))

    def cast_now(w, layer):
        return (w if layer is None else w[layer]).astype(BF16)

    have = {}
    for l in range(depth):
        g_mix = norm_mix[l][None]
        wl = layer_weights(l)
        if "w_in" not in have:
            have["w_in"] = cast_now(*wl["w_in"])
        missing = [k for k in ("w_out", "w1", "w2") if k not in have]
        if l % 2 == 0:
            e = l // 2
            pool_bd = _block_diag(pool_w[e]).astype(BF16)
            shared = (pool_bd, pool_scale[e][None], hgrn_lb_logits, hgrn_gain[e][None])
            mixed_p, p_new, s_new, *cast = even_mix_prompt(xp.reshape(b, t, d), g_mix, have["w_in"], *shared,
                                                           [wl[k] for k in missing], layer=l, tb=256)
            have.update(zip(missing, cast))
            pool_p.append(p_new)
            hgrn_p.append(s_new)
            proj_s = norm_proj(xs, g_mix, have["w_in"], bm=db)
            mixed_s, p_new, s_new = even_mix_sample(proj_s, jnp.swapaxes(state_pool[e], 0, 1), state_hgrn[e],
                                                    *shared, layer=l, pos0=PAST_LEN, bb=8)
            pool_s.append(jnp.swapaxes(p_new, 0, 1))
            hgrn_s.append(s_new)
        else:
            o = l // 2
            have.update({k: cast_now(*wl[k]) for k in missing})
            mixed_p, c_new = odd_mix_prompt(xp.reshape(b, t, d), g_mix, have["w_in"], conv_w[o], tb=512)
            conv_p.append(c_new)
            mixed_s, c_new = odd_mix_sample(xs, g_mix, have["w_in"], conv_w[o], state_conv[o].reshape(db, -1))
            conv_s.append(c_new.reshape(db, CONV_BUF, -1))
        nxt = layer_weights(l + 1) if l + 1 < depth else {}
        xp, xs, *cast = out_mlp(mixed_p.reshape(b * t, d), xp, mixed_s, xs, have["w_out"], norm_mlp, have["w1"],
                                have["w2"], norm_final[None], list(nxt.values()),
                                layer=l, final_norm=l == depth - 1, bm=512)
        have = dict(zip(nxt, cast))
    return (xp.reshape(b, t, d), xs.reshape(db, ds, d), jnp.stack(pool_p), jnp.stack(hgrn_p), jnp.stack(conv_p),
            jnp.stack(pool_s), jnp.stack(hgrn_s), jnp.stack(conv_s))
```

```python
import functools

import jax
import jax.numpy as jnp
from jax import lax
from jax.experimental import pallas as pl
from jax.experimental.pallas import tpu as pltpu

F32 = jnp.float32
BF16 = jnp.bfloat16

EPS = 1e-6
PAST_LEN = 16384
POOL_WINDOWS = (2, 4, 8, 16)
POOL_GROUP_DIM = 64
POOL_BUF = max(POOL_WINDOWS) - 1
POOL_HIST = 16
POOL_PAD = 8
LANES = 128
HGRN_CHUNK = 32
HGRN_BLOCK = 128
HEAD = 128
CONV_WIDTH = 3
CONV_BUF = CONV_WIDTH - 1
CONV_HIST = 8

VMEM_LIMIT_V7X = 56 * 1024 * 1024
MXU_COLS_V7X = 256


def _resident(shape):
    nd = len(shape)
    return pl.BlockSpec(shape, lambda *_: (0,) * nd, pipeline_mode=pl.Buffered(1))


def _layer_block(stacked, layer):
    nd = stacked.ndim
    return pl.BlockSpec((None,) + stacked.shape[1:], lambda *_: (layer,) + (0,) * (nd - 1),
                        pipeline_mode=pl.Buffered(1))


def _cast_specs(weights, n_steps):
    in_specs, out_specs, out_shapes = [], [], []
    for w, layer in weights:
        r, c = w.shape[-2:]
        assert r % (n_steps * 16) == 0, "row slabs must be whole bf16 sublane tiles"
        if layer is None:
            in_specs.append(pl.BlockSpec((r // n_steps, c), lambda s: (s, 0)))
        else:
            in_specs.append(pl.BlockSpec((None, r // n_steps, c), lambda s, layer=layer: (layer, s, 0)))
        out_specs.append(pl.BlockSpec((r // n_steps, c), lambda s: (s, 0)))
        out_shapes.append(jax.ShapeDtypeStruct((r, c), BF16))
    return in_specs, out_specs, out_shapes


def _rms(x, g):
    return x * lax.rsqrt(jnp.mean(x * x, axis=-1, keepdims=True) + EPS) * g


def _sigmoid(x):
    return 1.0 / (1.0 + jnp.exp(-x))


def _lower_bound(lbl, layer):
    e = jnp.exp(lbl - jnp.max(lbl, axis=0, keepdims=True))
    p = e / jnp.sum(e, axis=0, keepdims=True)
    return jnp.sum(p[0:layer + 1], axis=0, keepdims=True)


def _pool_select(snaps, u_shape):
    grp = lax.broadcasted_iota(jnp.int32, u_shape, 1) // POOL_GROUP_DIM
    s = snaps[POOL_WINDOWS[-1]]
    for g in range(len(POOL_WINDOWS) - 2, -1, -1):
        s = jnp.where(grp == g, snaps[POOL_WINDOWS[g]], s)
    win = jnp.left_shift(2, grp)
    return s, win


def _norm_proj_kernel(x_ref, g_ref, w_ref, o_ref, *wb_ref):
    h = _rms(x_ref[...], g_ref[...]).astype(BF16)
    w = w_ref[...].astype(BF16)
    for ref in wb_ref:
        ref[...] = w
    o_ref[...] = jnp.dot(h, w, preferred_element_type=F32)


def norm_proj(x, g, w, *, col_blocks):
    m, d = x.shape
    n = w.shape[1]
    bn = n // col_blocks
    assert n % col_blocks == 0 and bn % LANES == 0
    cast = w.dtype != BF16
    outs = pl.pallas_call(
        _norm_proj_kernel,
        grid=(col_blocks,),
        in_specs=[_resident((m, d)), _resident((1, d)), pl.BlockSpec((d, bn), lambda i: (0, i))],
        out_specs=[pl.BlockSpec((m, bn), lambda i: (0, i))] + [pl.BlockSpec((d, bn), lambda i: (0, i))] * cast,
        out_shape=[jax.ShapeDtypeStruct((m, n), F32)] + [jax.ShapeDtypeStruct((d, n), BF16)] * cast,
        compiler_params=pltpu.CompilerParams(dimension_semantics=("arbitrary",),
                                             vmem_limit_bytes=VMEM_LIMIT_V7X),
        name="norm_proj",
    )(x, g, w)
    return outs if cast else (outs[0], w)


def _out_mlp_kernel(*refs, n_cast, layer, final_norm, ff_chunk):
    m_ref, x_ref, ms_ref, xs_ref, wout_ref, g_ref, w1_ref, w2_ref, gf_ref = refs[:9]
    cast_in, (o_ref, os_ref), cast_out = refs[9:9 + n_cast], refs[9 + n_cast:11 + n_cast], refs[11 + n_cast:]
    d_ff = w1_ref.shape[1]
    for src_ref, dst_ref in zip(cast_in, cast_out):
        dst_ref[...] = src_ref[...].astype(BF16)

    def block(mixed, x):
        y = x + jnp.dot(mixed, wout_ref[...], preferred_element_type=F32)
        h = _rms(y, g_ref[layer:layer + 1, :]).astype(BF16)
        acc = y
        for c in range(d_ff // ff_chunk):
            a = jnp.dot(h, w1_ref[:, c * ff_chunk:(c + 1) * ff_chunk], preferred_element_type=F32)
            a = jnp.square(jnp.maximum(a, 0.0)).astype(BF16)
            acc = acc + jnp.dot(a, w2_ref[c * ff_chunk:(c + 1) * ff_chunk, :], preferred_element_type=F32)
        return _rms(acc, gf_ref[...]) if final_norm else acc

    o_ref[...] = block(m_ref[...], x_ref[...])

    @pl.when(pl.program_id(0) == pl.num_programs(0) - 1)
    def _():
        os_ref[...] = block(ms_ref[...], xs_ref[...])


def out_mlp(mixed, x, mixed_s, x_s, w_out, g_mlp, w1, w2, g_final, to_cast, *, layer, final_norm, bm):
    m, d = x.shape
    ms = x_s.shape[0]
    d_ff = w1.shape[1]
    steps = m // bm
    cast_in, cast_out, cast_shapes = _cast_specs(to_cast, steps)
    kern = functools.partial(_out_mlp_kernel, n_cast=len(to_cast), layer=layer, final_norm=final_norm,
                             ff_chunk=1024)
    return pl.pallas_call(
        kern,
        grid=(steps,),
        in_specs=[pl.BlockSpec((bm, d), lambda i: (i, 0)), pl.BlockSpec((bm, d), lambda i: (i, 0)),
                  _resident((ms, d)), _resident((ms, d)),
                  _resident((d, d)), _resident(g_mlp.shape), _resident((d, d_ff)),
                  _resident((d_ff, d)), _resident((1, d))] + cast_in,
        out_specs=[pl.BlockSpec((bm, d), lambda i: (i, 0)), pl.BlockSpec((ms, d), lambda i: (0, 0))] + cast_out,
        out_shape=[jax.ShapeDtypeStruct((m, d), F32), jax.ShapeDtypeStruct((ms, d), F32)] + cast_shapes,
        compiler_params=pltpu.CompilerParams(dimension_semantics=("arbitrary",),
                                             vmem_limit_bytes=VMEM_LIMIT_V7X),
        name="out_mlp",
    )(mixed, x, mixed_s, x_s, w_out, g_mlp, w1, w2, g_final, *[w for w, _ in to_cast])


def _even_prompt_kernel(*refs, n_cast, layer, tb, tiles_per_seq, n_tiles):
    xn_ref, x0_ref, x1_ref, g_ref, win_ref, wpool_ref, pscale_ref, lbl_ref, gain_ref = refs[:9]
    cast_in, (mixed_ref, pstate_ref, hstate_ref) = refs[9:9 + n_cast], refs[9 + n_cast:12 + n_cast]
    cast_out = refs[12 + n_cast:12 + 2 * n_cast]
    (pu_ref, pq_ref, pf_ref, pi_ref, pg_ref, hn_ref, hn2_ref, poolout_ref, ext_ref, lvl_ref, st_ref, k_ref,
     lsplit_ref, bcum_ref, dec_ref, gate_ref, v_ref, qd_ref, kd_ref, ke_ref, qb_ref, kb_ref, qr_ref,
     kc_ref) = refs[12 + 2 * n_cast:]
    for src_ref, dst_ref in zip(cast_in, cast_out):
        dst_ref[...] = src_ref[...].astype(BF16)

    step = pl.program_id(0)
    d_pool = wpool_ref.shape[0]
    n_heads = st_ref.shape[0]
    d_hgrn = n_heads * HEAD
    nc = tb // HGRN_CHUNK
    nblk = tb // HGRN_BLOCK
    j = step % tiles_per_seq
    j_next = jnp.minimum(step + 1, n_tiles - 1) % tiles_per_seq
    last = tiles_per_seq - 1
    o_q, o_f, o_i, o_g = d_pool, d_pool + d_hgrn, d_pool + 2 * d_hgrn, d_pool + 3 * d_hgrn
    sections = [(pu_ref, 0), (pq_ref, o_q), (pf_ref, o_f), (pi_ref, o_i), (pg_ref, o_g)]
    r1_, r2_ = POOL_PAD + POOL_HIST, POOL_PAD + POOL_HIST + tb

    def chunks(dst_ref, col0, src_ref=hn_ref):
        def make(c0, c1):
            def run():
                dst_ref[:, c0:c1] = jnp.dot(src_ref[...], win_ref[:, col0 + c0:col0 + c1],
                                            preferred_element_type=F32)
            return run
        width = dst_ref.shape[1]
        return [make(c0, min(c0 + MXU_COLS_V7X, width)) for c0 in range(0, width, MXU_COLS_V7X)]

    def pool(jt):
        u = pu_ref[...]
        ext_ref[r1_:r2_, :] = u
        ext_ref[0:r1_, :] = jnp.where(jt == 0, 0.0, ext_ref[0:r1_, :])
        groups_per_tile = LANES // POOL_GROUP_DIM
        lane_grp = lax.broadcasted_iota(jnp.int32, (tb, LANES), 1) // POOL_GROUP_DIM
        pos = jt * tb + lax.broadcasted_iota(jnp.int32, (tb, LANES), 0)
        means = []
        for lt in range(d_pool // LANES):
            wins = POOL_WINDOWS[lt * groups_per_tile:(lt + 1) * groups_per_tile]
            src, src_lanes, w, nbuf, got = ext_ref, slice(lt * LANES, (lt + 1) * LANES), 1, 0, {}
            while w < wins[-1]:
                new = src[POOL_PAD:r2_, src_lanes] + src[POOL_PAD - w:r2_ - w, src_lanes]
                w *= 2
                if w in wins:
                    got[w] = new[POOL_HIST:]
                if w < wins[-1]:
                    lvl_ref[nbuf, POOL_PAD:r2_, :] = new
                    src, src_lanes, nbuf = lvl_ref.at[nbuf], slice(0, LANES), 1 - nbuf
            ssum, win = got[wins[-1]], jnp.full((tb, LANES), wins[-1], jnp.int32)
            for g_ in range(groups_per_tile - 2, -1, -1):
                ssum = jnp.where(lane_grp == g_, got[wins[g_]], ssum)
                win = jnp.where(lane_grp == g_, wins[g_], win)
            cnt = jnp.minimum(win, pos + 1).astype(F32)
            means.append(ssum / cnt)
        diff = (jnp.concatenate(means, axis=1) - u).astype(BF16)
        pool_out = jnp.dot(diff, wpool_ref[...], preferred_element_type=F32) * pscale_ref[...]
        poolout_ref[...] = pool_out.astype(BF16)
        pstate_ref[...] = ext_ref[pl.ds(r2_ - POOL_BUF, POOL_BUF), :]
        ext_ref[POOL_PAD:r1_, :] = ext_ref[POOL_PAD + tb:r1_ + tb, :]

    @pl.when(step == 0)
    def _():
        ext_ref[0:r1_, :] = jnp.zeros((r1_, d_pool), F32)
        lvl_ref[:, 0:POOL_PAD, :] = jnp.zeros((lvl_ref.shape[0], POOL_PAD, LANES), F32)
        hn_ref[...] = _rms(x0_ref[...], g_ref[...]).astype(BF16)
        for dst_ref, col0 in sections:
            for run in chunks(dst_ref, col0):
                run()
        pool(0)
        hn_ref[...] = _rms(x1_ref[...], g_ref[...]).astype(BF16)
        for run in chunks(pu_ref, 0):
            run()

    @pl.when(j == 0)
    def _():
        st_ref[...] = jnp.zeros(st_ref.shape, F32)

    mixed_ref[:, 0:d_pool] = poolout_ref[...]

    def prepare_next_tile():
        hn2_ref[...] = _rms(xn_ref[...], g_ref[...]).astype(BF16)
        pool(j_next)
        for run in chunks(pu_ref, 0, hn2_ref):
            run()

    lb = _lower_bound(lbl_ref[...], layer)
    for c in range(nc):
        rows = slice(c * HGRN_CHUNK, (c + 1) * HGRN_CHUNK)
        fz = pf_ref[rows, :]
        a = jnp.exp(-jnp.abs(fz))
        r = 1.0 / (1.0 + a)
        ar = a * r
        sig = jnp.where(fz >= 0, r, ar)
        sig_neg = jnp.where(fz >= 0, ar, r)
        logf = jnp.log(lb + (1.0 - lb) * sig)
        k_ref[rows, :] = (1.0 - lb) * sig_neg
        l_hi = logf.astype(BF16)
        r1 = logf - l_hi.astype(F32)
        l_mid = r1.astype(BF16)
        lsplit_ref[rows, 0:d_hgrn] = l_hi
        lsplit_ref[rows, d_hgrn:2 * d_hgrn] = l_mid
        lsplit_ref[rows, 2 * d_hgrn:3 * d_hgrn] = (r1 - l_mid.astype(F32)).astype(BF16)
        iz = pi_ref[rows, :]
        v_ref[rows, :] = (iz * _sigmoid(iz)).astype(BF16)
        gate_ref[rows, :] = gain_ref[...] * _sigmoid(pg_ref[rows, :])

    ri = lax.broadcasted_iota(jnp.int32, (tb, tb), 0)
    ci = lax.broadcasted_iota(jnp.int32, (tb, tb), 1)
    tri = ((ri // HGRN_CHUNK == ci // HGRN_CHUNK) & (ci <= ri)).astype(BF16)
    cs = jnp.dot(tri, lsplit_ref[...], preferred_element_type=F32)
    bcum_ref[...] = cs[:, 0:d_hgrn] + cs[:, d_hgrn:2 * d_hgrn] + cs[:, 2 * d_hgrn:3 * d_hgrn]
    for run in chunks(pf_ref, o_f) + chunks(pi_ref, o_i):
        run()
    prepare_next_tile()

    nsub = HGRN_BLOCK // HGRN_CHUNK
    half = nsub // 2
    for b in range(nblk):
        blast = [bcum_ref[(b * nsub + i + 1) * HGRN_CHUNK - 1:(b * nsub + i + 1) * HGRN_CHUNK, :]
                 for i in range(nsub)]
        pre = [jnp.zeros_like(blast[0])]
        for i in range(nsub):
            pre.append(pre[-1] + blast[i])
        dec_ref[b:b + 1, :] = jnp.exp(pre[nsub])
        for i in range(nsub):
            c = b * nsub + i
            rows = slice(c * HGRN_CHUNK, (c + 1) * HGRN_CHUNK)
            bc = bcum_ref[rows, :]
            qd = pq_ref[rows, :] * jnp.exp(bc)
            kd = k_ref[rows, :] * jnp.exp(-bc)
            ke = kd * jnp.exp(blast[i])
            qd_ref[rows, :] = qd.astype(BF16)
            kd_ref[rows, :] = kd.astype(BF16)
            ke_ref[rows, :] = ke.astype(BF16)
            qb_ref[rows, :] = (qd * jnp.exp(pre[i])).astype(BF16)
            kb_ref[rows, :] = (ke * jnp.exp(pre[nsub] - pre[i + 1])).astype(BF16)
            if i >= half:
                qr_ref[rows, :] = (qd * jnp.exp(pre[i] - pre[half])).astype(BF16)
                kc_ref[rows, :] = jnp.zeros((HGRN_CHUNK, d_hgrn), BF16)
            else:
                kc_ref[rows, :] = (ke * jnp.exp(pre[half] - pre[i + 1])).astype(BF16)

    rb = lax.broadcasted_iota(jnp.int32, (HGRN_BLOCK, HGRN_BLOCK), 0)
    cb = lax.broadcasted_iota(jnp.int32, (HGRN_BLOCK, HGRN_BLOCK), 1)
    rsub, csub = rb // HGRN_CHUNK, cb // HGRN_CHUNK
    m_diag = (rsub == csub) & (cb <= rb)
    m_adj = (csub == rsub - 1) & (rsub != half)
    eye = rb == cb
    hb = HGRN_BLOCK // 2
    nt_dims = (((1,), (1,)), ((), ()))
    tn_dims = (((0,), (0,)), ((), ()))
    fillers = chunks(pq_ref, o_q) + chunks(pg_ref, o_g)

    def fill(n):
        for _ in range(min(n, len(fillers))):
            fillers.pop(0)()

    per_stage = -(-len(fillers) // (3 * nblk))
    for b in range(nblk):
        rows = slice(b * HGRN_BLOCK, (b + 1) * HGRN_BLOCK)
        far = slice(b * HGRN_BLOCK + hb, (b + 1) * HGRN_BLOCK)
        g12s, g3s = [], []
        for hd in range(n_heads):
            sl = slice(hd * HEAD, (hd + 1) * HEAD)
            kk = jnp.concatenate([kd_ref[rows, sl], ke_ref[rows, sl]], axis=0)
            g12s.append(lax.dot_general(qd_ref[rows, sl], kk, nt_dims, preferred_element_type=F32))
            g3s.append(lax.dot_general(qr_ref[far, sl], kc_ref[rows, sl], nt_dims, preferred_element_type=F32))
        fill(per_stage)
        outs = []
        for hd in range(n_heads):
            sl = slice(hd * HEAD, (hd + 1) * HEAD)
            g12 = g12s[hd]
            scores = jnp.where(m_diag, g12[:, 0:HGRN_BLOCK], jnp.where(m_adj, g12[:, HGRN_BLOCK:], 0.0))
            scores = jnp.concatenate([scores[0:hb], scores[hb:] + g3s[hd]], axis=0).astype(BF16)
            v_blk = v_ref[rows, sl]
            st = st_ref[hd]
            lhs = jnp.concatenate([scores, qb_ref[rows, sl]], axis=1)
            rhs = jnp.concatenate([v_blk, st.astype(BF16)], axis=0)
            outs.append(jnp.dot(lhs, rhs, preferred_element_type=F32))
            d_col = jnp.sum(jnp.where(eye, dec_ref[b:b + 1, sl], 0.0), axis=1, keepdims=True)
            st_ref[hd] = st * d_col + lax.dot_general(kb_ref[rows, sl], v_blk, tn_dims,
                                                      preferred_element_type=F32)
        fill(per_stage)
        for hd in range(n_heads):
            sl = slice(hd * HEAD, (hd + 1) * HEAD)
            o = outs[hd]
            o = o * lax.rsqrt(jnp.mean(o * o, axis=-1, keepdims=True) + EPS)
            mixed_ref[rows, d_pool + hd * HEAD:d_pool + (hd + 1) * HEAD] = (o * gate_ref[rows, sl]).astype(BF16)
        fill(per_stage)
    fill(len(fillers))
    hn_ref[...] = hn2_ref[...]

    @pl.when(j == last)
    def _():
        hstate_ref[...] = st_ref[...]


def even_mix_prompt(x, g, w_in, wpool_bd, pscale, lbl, gain, to_cast, *, layer, tb):
    b, t, d = x.shape
    n_in = w_in.shape[1]
    d_pool = wpool_bd.shape[0]
    d_hgrn = gain.shape[1]
    n_heads = d_hgrn // HEAD
    tps = t // tb
    n_tiles = b * tps
    cast_in, cast_out, cast_shapes = _cast_specs(to_cast, n_tiles)
    kern = functools.partial(_even_prompt_kernel, n_cast=len(to_cast), layer=layer, tb=tb, tiles_per_seq=tps,
                             n_tiles=n_tiles)

    def tile(n):
        return n // tps, n % tps

    def ahead(s, k):
        return tile(jnp.minimum(s + k, n_tiles - 1))

    assert tps >= 2, "a tile and its successor share the pooling-state output block"

    assert d_pool % LANES == 0 and LANES % POOL_GROUP_DIM == 0 and tb % HGRN_BLOCK == 0
    return pl.pallas_call(
        kern,
        grid=(n_tiles,),
        in_specs=[pl.BlockSpec((None, tb, d), lambda s: (*ahead(s, 2), 0)),
                  pl.BlockSpec((None, tb, d), lambda s: (0, 0, 0)),
                  pl.BlockSpec((None, tb, d), lambda s: (0, 1, 0)),
                  _resident((1, d)), _resident((d, n_in)), _resident((d_pool, d_pool)),
                  _resident((1, d_pool)), _resident(lbl.shape), _resident((1, d_hgrn))] + cast_in,
        out_specs=[pl.BlockSpec((None, tb, d), lambda s: (s // tps, s % tps, 0)),
                   pl.BlockSpec((None, POOL_BUF, d_pool), lambda s: (ahead(s, 1)[0], 0, 0)),
                   pl.BlockSpec((None, n_heads, HEAD, HEAD), lambda s: (s // tps, 0, 0, 0))] + cast_out,
        out_shape=[jax.ShapeDtypeStruct((b, t, d), BF16),
                   jax.ShapeDtypeStruct((b, POOL_BUF, d_pool), F32),
                   jax.ShapeDtypeStruct((b, n_heads, HEAD, HEAD), F32)] + cast_shapes,
        scratch_shapes=[pltpu.VMEM((tb, d_pool), F32)]
                       + [pltpu.VMEM((tb, d_hgrn), F32)] * 4
                       + [pltpu.VMEM((tb, d), BF16),
                        pltpu.VMEM((tb, d), BF16),
                        pltpu.VMEM((tb, d_pool), BF16),
                        pltpu.VMEM((POOL_PAD + POOL_HIST + tb, d_pool), F32),
                        pltpu.VMEM((2, POOL_PAD + POOL_HIST + tb, LANES), F32),
                        pltpu.VMEM((n_heads, HEAD, HEAD), F32),
                        pltpu.VMEM((tb, d_hgrn), F32),
                        pltpu.VMEM((tb, 3 * d_hgrn), BF16),
                        pltpu.VMEM((tb, d_hgrn), F32),
                        pltpu.VMEM((tb // HGRN_BLOCK, d_hgrn), F32),
                        pltpu.VMEM((tb, d_hgrn), F32)]
                       + [pltpu.VMEM((tb, d_hgrn), BF16)] * 8,
        compiler_params=pltpu.CompilerParams(dimension_semantics=("arbitrary",),
                                             vmem_limit_bytes=VMEM_LIMIT_V7X),
        name="even_mix_prompt",
    )(x, x, x, g, w_in, wpool_bd, pscale, lbl, gain, *[w for w, _ in to_cast])


def _even_sample_kernel(proj_ref, pool_ref, hst_ref, wpool_ref, pscale_ref, lbl_ref, gain_ref,
                        mixed_ref, npool_ref, nhst_ref, v_ref, o_ref, *, layer, pos0):
    bb = proj_ref.shape[0]
    d_pool = wpool_ref.shape[0]
    n_heads = hst_ref.shape[1]
    d_hgrn = n_heads * HEAD

    u = proj_ref[:, 0:d_pool]
    acc = u
    snaps = {}
    for s in range(1, POOL_WINDOWS[-1]):
        acc = acc + pool_ref[POOL_BUF - s]
        if s + 1 in POOL_WINDOWS:
            snaps[s + 1] = acc
    ssum, win = _pool_select(snaps, u.shape)
    cnt = jnp.minimum(win, pos0 + 1).astype(F32)
    diff = (ssum / cnt - u).astype(BF16)
    pool_out = jnp.dot(diff, wpool_ref[...], preferred_element_type=F32) * pscale_ref[...]
    mixed_ref[:, 0:d_pool] = pool_out.astype(BF16)
    npool_ref[0:POOL_BUF - 1] = pool_ref[1:POOL_BUF]
    npool_ref[POOL_BUF - 1] = u

    lb = _lower_bound(lbl_ref[...], layer)
    fz = proj_ref[:, d_pool + d_hgrn:d_pool + 2 * d_hgrn]
    iz = proj_ref[:, d_pool + 2 * d_hgrn:d_pool + 3 * d_hgrn]
    v_ref[...] = iz * _sigmoid(iz)
    fkq = (lb + (1.0 - lb) * _sigmoid(fz), (1.0 - lb) * _sigmoid(-fz), proj_ref[:, d_pool:d_pool + d_hgrn])

    def split3(x):
        hi = x.astype(BF16).astype(F32)
        r1 = x - hi
        mid = r1.astype(BF16).astype(F32)
        return jnp.concatenate([hi, mid, r1 - mid], axis=0)

    sel_r = lax.broadcasted_iota(jnp.int32, (3 * bb, bb * HEAD), 0) % bb
    sel_c = lax.broadcasted_iota(jnp.int32, (3 * bb, bb * HEAD), 1) // HEAD
    sel = (sel_r == sel_c).astype(F32)
    parts = [split3(x) for x in fkq]
    for hd in range(n_heads):
        sl = slice(hd * HEAD, (hd + 1) * HEAD)
        f_cols, k_cols, q_cols = [lax.dot_general(p[:, sl], sel, (((0,), (0,)), ((), ())),
                                                  preferred_element_type=F32) for p in parts]
        for i in range(bb):
            blk = slice(i * HEAD, (i + 1) * HEAD)
            s_new = f_cols[:, blk] * hst_ref[i, hd] + k_cols[:, blk] * v_ref[i:i + 1, sl]
            nhst_ref[i, hd] = s_new
            o_ref[i:i + 1, sl] = jnp.sum(q_cols[:, blk] * s_new, axis=0, keepdims=True)

    gz = proj_ref[:, d_pool + 3 * d_hgrn:d_pool + 4 * d_hgrn]
    gate = gain_ref[...] * _sigmoid(gz)
    for hd in range(n_heads):
        sl = slice(hd * HEAD, (hd + 1) * HEAD)
        o = o_ref[:, sl]
        o = o * lax.rsqrt(jnp.mean(o * o, axis=-1, keepdims=True) + EPS)
        mixed_ref[:, d_pool + hd * HEAD:d_pool + (hd + 1) * HEAD] = (o * gate[:, sl]).astype(BF16)


def even_mix_sample(proj, pool_rows, hstate, wpool_bd, pscale, lbl, gain, *, layer, pos0, bb):
    b, n_in = proj.shape
    d_pool = wpool_bd.shape[0]
    d_hgrn = gain.shape[1]
    n_heads = d_hgrn // HEAD
    d = d_pool + d_hgrn
    kern = functools.partial(_even_sample_kernel, layer=layer, pos0=pos0)
    return pl.pallas_call(
        kern,
        grid=(b // bb,),
        in_specs=[pl.BlockSpec((bb, n_in), lambda i: (i, 0)),
                  pl.BlockSpec((POOL_BUF, bb, d_pool), lambda i: (0, i, 0)),
                  pl.BlockSpec((bb, n_heads, HEAD, HEAD), lambda i: (i, 0, 0, 0)),
                  _resident((d_pool, d_pool)), _resident((1, d_pool)), _resident(lbl.shape),
                  _resident((1, d_hgrn))],
        out_specs=[pl.BlockSpec((bb, d), lambda i: (i, 0)),
                   pl.BlockSpec((POOL_BUF, bb, d_pool), lambda i: (0, i, 0)),
                   pl.BlockSpec((bb, n_heads, HEAD, HEAD), lambda i: (i, 0, 0, 0))],
        out_shape=[jax.ShapeDtypeStruct((b, d), BF16),
                   jax.ShapeDtypeStruct((POOL_BUF, b, d_pool), F32),
                   jax.ShapeDtypeStruct((b, n_heads, HEAD, HEAD), F32)],
        scratch_shapes=[pltpu.VMEM((bb, d_hgrn), F32)] * 2,
        compiler_params=pltpu.CompilerParams(dimension_semantics=("arbitrary",),
                                             vmem_limit_bytes=VMEM_LIMIT_V7X),
        name="even_mix_sample",
    )(proj, pool_rows, hstate, wpool_bd, pscale, lbl, gain)


def _odd_kernel(x_ref, xs_ref, cst_ref, g_ref, win_ref, cw_ref, mixed_ref, cstate_ref, ms_ref, ncst_ref, zext_ref,
                *, tb):
    i, j = pl.program_id(0), pl.program_id(1)
    last = pl.num_programs(1) - 1
    dc = cw_ref.shape[1]

    def gated_taps(x):
        h = _rms(x, g_ref[...]).astype(BF16)
        cg = jnp.dot(h, win_ref[:, dc:2 * dc], preferred_element_type=F32)
        hv = jnp.dot(h, win_ref[:, 2 * dc:3 * dc], preferred_element_type=F32)
        z = cg * hv
        bg = jnp.dot(h, win_ref[:, 0:dc], preferred_element_type=F32)
        return bg, z, cw_ref[CONV_WIDTH - 1:CONV_WIDTH, :] * z

    @pl.when(j == 0)
    def _():
        zext_ref[0:CONV_HIST, :] = jnp.zeros((CONV_HIST, dc), F32)

    bg, z, conv = gated_taps(x_ref[...])
    zext_ref[CONV_HIST:CONV_HIST + tb, :] = z
    for s in range(1, CONV_WIDTH):
        conv = conv + cw_ref[CONV_WIDTH - 1 - s:CONV_WIDTH - s, :] * zext_ref[pl.ds(CONV_HIST - s, tb), :]
    mixed_ref[...] = (bg * conv).astype(BF16)

    @pl.when(j == last)
    def _():
        cstate_ref[...] = zext_ref[pl.ds(CONV_HIST + tb - CONV_BUF, CONV_BUF), :]

    zext_ref[0:CONV_HIST, :] = zext_ref[tb:tb + CONV_HIST, :]

    @pl.when((i == pl.num_programs(0) - 1) & (j == last))
    def _():
        bg_s, z_s, conv_s = gated_taps(xs_ref[...])
        for s in range(1, CONV_WIDTH):
            r0 = (CONV_BUF - s) * dc
            conv_s = conv_s + cw_ref[CONV_WIDTH - 1 - s:CONV_WIDTH - s, :] * cst_ref[:, r0:r0 + dc]
        ms_ref[...] = (bg_s * conv_s).astype(BF16)
        ncst_ref[:, 0:(CONV_BUF - 1) * dc] = cst_ref[:, dc:CONV_BUF * dc]
        ncst_ref[:, (CONV_BUF - 1) * dc:CONV_BUF * dc] = z_s


def odd_mix(x, x_s, cstate_s, g, w_in, conv_w, *, tb):
    b, t, d = x.shape
    ms = x_s.shape[0]
    dc = conv_w.shape[1]
    kern = functools.partial(_odd_kernel, tb=tb)
    return pl.pallas_call(
        kern,
        grid=(b, t // tb),
        in_specs=[pl.BlockSpec((None, tb, d), lambda i, j: (i, j, 0)),
                  _resident((ms, d)), _resident((ms, CONV_BUF * dc)),
                  _resident((1, d)), _resident((d, 3 * dc)), _resident((CONV_WIDTH, dc))],
        out_specs=[pl.BlockSpec((None, tb, dc), lambda i, j: (i, j, 0)),
                   pl.BlockSpec((None, CONV_BUF, dc), lambda i, j: (i, 0, 0)),
                   pl.BlockSpec((ms, dc), lambda i, j: (0, 0)),
                   pl.BlockSpec((ms, CONV_BUF * dc), lambda i, j: (0, 0))],
        out_shape=[jax.ShapeDtypeStruct((b, t, dc), BF16),
                   jax.ShapeDtypeStruct((b, CONV_BUF, dc), F32),
                   jax.ShapeDtypeStruct((ms, dc), BF16),
                   jax.ShapeDtypeStruct((ms, CONV_BUF * dc), F32)],
        scratch_shapes=[pltpu.VMEM((CONV_HIST + tb, dc), F32)],
        compiler_params=pltpu.CompilerParams(dimension_semantics=("arbitrary", "arbitrary"),
                                             vmem_limit_bytes=VMEM_LIMIT_V7X),
        name="odd_mix",
    )(x, x_s, cstate_s, g, w_in, conv_w)


def _block_diag(w):
    g, c, _ = w.shape
    rows = [jnp.pad(w[i], ((0, 0), (i * c, (g - 1 - i) * c))) for i in range(g)]
    return jnp.concatenate(rows, axis=0)


def kernel(x_prompt, x_sample, state_pool, state_hgrn, state_conv, norm_mix, norm_mlp, norm_final, even_w_in, pool_w, pool_scale, hgrn_lb_logits, hgrn_gain, even_w_out, odd_w_in, conv_w, odd_w_out, ff_w1, ff_w2):
    depth = norm_mix.shape[0]
    b, t, d = x_prompt.shape
    db, ds, _ = x_sample.shape
    assert ds == 1, "the sample group carries one token per sequence"
    xp = x_prompt.reshape(b * t, d)
    xs = x_sample.reshape(db, d)
    pool_p, hgrn_p, conv_p, pool_s, hgrn_s, conv_s = [], [], [], [], [], []

    def layer_weights(l):
        w_in, w_out = (even_w_in, even_w_out) if l % 2 == 0 else (odd_w_in, odd_w_out)
        return dict(w_in=(w_in[l // 2], None), w_out=(w_out[l // 2], None), w1=(ff_w1, l), w2=(ff_w2, l))

    def cast_now(w, layer):
        return (w if layer is None else w[layer]).astype(BF16)

    have = {}
    for l in range(depth):
        g_mix = norm_mix[l][None]
        wl = layer_weights(l)
        missing = [k for k in ("w_out", "w1", "w2") if k not in have]
        if l % 2 == 0:
            e = l // 2
            pool_bd = _block_diag(pool_w[e]).astype(BF16)
            shared = (pool_bd, pool_scale[e][None], hgrn_lb_logits, hgrn_gain[e][None])
            proj_s, have["w_in"] = norm_proj(xs, g_mix, have.get("w_in", wl["w_in"][0]), col_blocks=2)
            mixed_s, p_new, s_new = even_mix_sample(proj_s, jnp.swapaxes(state_pool[e], 0, 1), state_hgrn[e],
                                                    *shared, layer=l, pos0=PAST_LEN, bb=8)
            pool_s.append(jnp.swapaxes(p_new, 0, 1))
            hgrn_s.append(s_new)
            mixed_p, p_new, s_new, *cast = even_mix_prompt(xp.reshape(b, t, d), g_mix, have["w_in"], *shared,
                                                           [wl[k] for k in missing], layer=l, tb=256)
            have.update(zip(missing, cast))
            pool_p.append(p_new)
            hgrn_p.append(s_new)
        else:
            o = l // 2
            have.update({k: cast_now(*wl[k]) for k in ["w_in"] + missing if k not in have})
            mixed_p, c_new, mixed_s, cs_new = odd_mix(xp.reshape(b, t, d), xs, state_conv[o].reshape(db, -1), g_mix,
                                                      have["w_in"], conv_w[o], tb=512)
            conv_p.append(c_new)
            conv_s.append(cs_new.reshape(db, CONV_BUF, -1))
        nxt = layer_weights(l + 1) if l + 1 < depth else {}
        xp, xs, *cast = out_mlp(mixed_p.reshape(b * t, d), xp, mixed_s, xs, have["w_out"], norm_mlp, have["w1"],
                                have["w2"], norm_final[None], list(nxt.values()),
                                layer=l, final_norm=l == depth - 1, bm=512)
        have = dict(zip(nxt, cast))
    return (xp.reshape(b, t, d), xs.reshape(db, ds, d), jnp.stack(pool_p), jnp.stack(hgrn_p), jnp.stack(conv_p),
            jnp.stack(pool_s), jnp.stack(hgrn_s), jnp.stack(conv_s))
```

```python
import functools

import jax
import jax.numpy as jnp
from jax import lax
from jax.experimental import pallas as pl
from jax.experimental.pallas import tpu as pltpu

F32 = jnp.float32
BF16 = jnp.bfloat16

EPS = 1e-6
PAST_LEN = 16384
POOL_WINDOWS = (2, 4, 8, 16)
POOL_GROUP_DIM = 64
POOL_BUF = max(POOL_WINDOWS) - 1
POOL_HIST = 16
POOL_PAD = 8
LANES = 128
HGRN_CHUNK = 32
HGRN_BLOCK = 128
HEAD = 128
CONV_WIDTH = 3
CONV_BUF = CONV_WIDTH - 1
CONV_HIST = 8

VMEM_LIMIT_V7X = 56 * 1024 * 1024
MXU_COLS_V7X = 256


def _resident(shape):
    nd = len(shape)
    return pl.BlockSpec(shape, lambda *_: (0,) * nd, pipeline_mode=pl.Buffered(1))


def _layer_block(stacked, layer):
    nd = stacked.ndim
    return pl.BlockSpec((None,) + stacked.shape[1:], lambda *_: (layer,) + (0,) * (nd - 1),
                        pipeline_mode=pl.Buffered(1))


def _cast_specs(weights, n_steps):
    in_specs, out_specs, out_shapes = [], [], []
    for w, layer in weights:
        r, c = w.shape[-2:]
        assert r % (n_steps * 16) == 0, "row slabs must be whole bf16 sublane tiles"
        if layer is None:
            in_specs.append(pl.BlockSpec((r // n_steps, c), lambda s: (s, 0)))
        else:
            in_specs.append(pl.BlockSpec((None, r // n_steps, c), lambda s, layer=layer: (layer, s, 0)))
        out_specs.append(pl.BlockSpec((r // n_steps, c), lambda s: (s, 0)))
        out_shapes.append(jax.ShapeDtypeStruct((r, c), BF16))
    return in_specs, out_specs, out_shapes


def _rms(x, g):
    return x * lax.rsqrt(jnp.mean(x * x, axis=-1, keepdims=True) + EPS) * g


def _sigmoid(x):
    return 1.0 / (1.0 + jnp.exp(-x))


def _lower_bound(lbl, layer):
    e = jnp.exp(lbl - jnp.max(lbl, axis=0, keepdims=True))
    p = e / jnp.sum(e, axis=0, keepdims=True)
    return jnp.sum(p[0:layer + 1], axis=0, keepdims=True)


def _pool_select(snaps, u_shape):
    grp = lax.broadcasted_iota(jnp.int32, u_shape, 1) // POOL_GROUP_DIM
    s = snaps[POOL_WINDOWS[-1]]
    for g in range(len(POOL_WINDOWS) - 2, -1, -1):
        s = jnp.where(grp == g, snaps[POOL_WINDOWS[g]], s)
    win = jnp.left_shift(2, grp)
    return s, win


def _norm_proj_kernel(x_ref, g_ref, w_ref, o_ref, *wb_ref):
    h = _rms(x_ref[...], g_ref[...]).astype(BF16)
    w = w_ref[...].astype(BF16)
    for ref in wb_ref:
        ref[...] = w
    o_ref[...] = jnp.dot(h, w, preferred_element_type=F32)


def norm_proj(x, g, w, *, col_blocks):
    m, d = x.shape
    n = w.shape[1]
    bn = n // col_blocks
    assert n % col_blocks == 0 and bn % LANES == 0
    cast = w.dtype != BF16
    outs = pl.pallas_call(
        _norm_proj_kernel,
        grid=(col_blocks,),
        in_specs=[_resident((m, d)), _resident((1, d)), pl.BlockSpec((d, bn), lambda i: (0, i))],
        out_specs=[pl.BlockSpec((m, bn), lambda i: (0, i))] + [pl.BlockSpec((d, bn), lambda i: (0, i))] * cast,
        out_shape=[jax.ShapeDtypeStruct((m, n), F32)] + [jax.ShapeDtypeStruct((d, n), BF16)] * cast,
        compiler_params=pltpu.CompilerParams(dimension_semantics=("arbitrary",),
                                             vmem_limit_bytes=VMEM_LIMIT_V7X),
        name="norm_proj",
    )(x, g, w)
    return outs if cast else (outs[0], w)


def _mlp_block(mixed, x, wout_ref, g_ref, w1_ref, w2_ref, gf_ref, *, layer, final_norm, ff_chunk, between=None):
    d_ff = w1_ref.shape[1]
    y = x + jnp.dot(mixed, wout_ref[...], preferred_element_type=F32)
    h = _rms(y, g_ref[layer:layer + 1, :]).astype(BF16)
    acc = y
    for c in range(d_ff // ff_chunk):
        a = jnp.dot(h, w1_ref[:, c * ff_chunk:(c + 1) * ff_chunk], preferred_element_type=F32)
        a = jnp.square(jnp.maximum(a, 0.0)).astype(BF16)
        acc = acc + jnp.dot(a, w2_ref[c * ff_chunk:(c + 1) * ff_chunk, :], preferred_element_type=F32)
        if c == 0 and between is not None:
            between()
    return _rms(acc, gf_ref[...]) if final_norm else acc


def _hgrn_sample_step(pj_ref, hst_ref, lbl_ref, nhst_ref, oraw_ref, *, layer, d_pool):
    sb, n_heads = hst_ref.shape[0], hst_ref.shape[1]
    d_hgrn = n_heads * HEAD
    lb = _lower_bound(lbl_ref[...], layer)
    fz = pj_ref[:, d_pool + d_hgrn:d_pool + 2 * d_hgrn]
    iz = pj_ref[:, d_pool + 2 * d_hgrn:d_pool + 3 * d_hgrn]
    v = iz * _sigmoid(iz)
    fkq = (lb + (1.0 - lb) * _sigmoid(fz), (1.0 - lb) * _sigmoid(-fz), pj_ref[:, d_pool:d_pool + d_hgrn])

    eye = (lax.broadcasted_iota(jnp.int32, (HEAD, HEAD), 0) == lax.broadcasted_iota(jnp.int32, (HEAD, HEAD), 1))

    def to_column(row):
        return jnp.sum(jnp.where(eye, row, 0.0), axis=1, keepdims=True)

    for i in range(sb):
        for hd in range(n_heads):
            sl = slice(hd * HEAD, (hd + 1) * HEAD)
            f_col, k_col, q_col = [to_column(x[i:i + 1, sl]) for x in fkq]
            s_new = f_col * hst_ref[i, hd] + k_col * v[i:i + 1, sl]
            nhst_ref[i, hd] = s_new
            oraw_ref[i:i + 1, sl] = jnp.sum(q_col * s_new, axis=0, keepdims=True)


def _out_mlp_kernel(*refs, n_cast, layer, final_norm, ff_chunk, sample_rows, hgrn):
    refs = list(refs)
    m_ref, x_ref = refs[:2]
    del refs[:2]
    if sample_rows:
        ms_ref, xs_ref = refs[:2]
        del refs[:2]
    weights = refs[:5]
    del refs[:5]
    if hgrn is not None:
        pj_ref, hst_ref, lbl_ref = refs[:3]
        del refs[:3]
    cast_in = refs[:n_cast]
    del refs[:n_cast]
    o_ref = refs.pop(0)
    if sample_rows:
        os_ref = refs.pop(0)
    if hgrn is not None:
        nhst_ref, oraw_ref = refs[:2]
        del refs[:2]
    cast_out = refs
    for src_ref, dst_ref in zip(cast_in, cast_out):
        dst_ref[...] = src_ref[...].astype(BF16)

    between = None
    if hgrn is not None:
        between = functools.partial(_hgrn_sample_step, pj_ref, hst_ref, lbl_ref, nhst_ref, oraw_ref, **hgrn)
    o_ref[...] = _mlp_block(m_ref[...], x_ref[...], *weights, layer=layer, final_norm=final_norm,
                            ff_chunk=ff_chunk, between=between)

    if sample_rows:
        @pl.when(pl.program_id(0) == pl.num_programs(0) - 1)
        def _():
            os_ref[...] = _mlp_block(ms_ref[...], xs_ref[...], *weights, layer=layer, final_norm=final_norm,
                                     ff_chunk=ff_chunk)


def out_mlp(mixed, x, sample, w_out, g_mlp, w1, w2, g_final, to_cast, hgrn_sample=None, *, layer, final_norm, bm):
    m, d = x.shape
    d_ff = w1.shape[1]
    steps = m // bm
    cast_in, cast_out, cast_shapes = _cast_specs(to_cast, steps)
    args = [mixed, x]
    in_specs = [pl.BlockSpec((bm, d), lambda i: (i, 0)), pl.BlockSpec((bm, d), lambda i: (i, 0))]
    out_specs = [pl.BlockSpec((bm, d), lambda i: (i, 0))]
    out_shapes = [jax.ShapeDtypeStruct((m, d), F32)]
    if sample is not None:
        ms = sample[1].shape[0]
        args += list(sample)
        in_specs += [_resident((ms, d)), _resident((ms, d))]
        out_specs.append(pl.BlockSpec((ms, d), lambda i: (0, 0)))
        out_shapes.append(jax.ShapeDtypeStruct((ms, d), F32))
    args += [w_out, g_mlp, w1, w2, g_final]
    in_specs += [_resident((d, d)), _resident(g_mlp.shape), _resident((d, d_ff)), _resident((d_ff, d)),
                 _resident((1, d))]
    hgrn = None
    if hgrn_sample is not None:
        proj_s, state, lbl, d_pool, hgrn_layer = hgrn_sample
        nseq, n_in = proj_s.shape
        assert nseq % steps == 0
        sb = nseq // steps
        n_heads = state.shape[1]
        args += [proj_s.reshape(steps, sb, n_in), state, lbl]
        in_specs += [pl.BlockSpec((None, sb, n_in), lambda i: (i, 0, 0)),
                     pl.BlockSpec((sb, n_heads, HEAD, HEAD), lambda i: (i, 0, 0, 0)), _resident(lbl.shape)]
        out_specs += [pl.BlockSpec((sb, n_heads, HEAD, HEAD), lambda i: (i, 0, 0, 0)),
                      pl.BlockSpec((None, sb, n_heads * HEAD), lambda i: (i, 0, 0))]
        out_shapes += [jax.ShapeDtypeStruct(state.shape, F32),
                       jax.ShapeDtypeStruct((steps, sb, n_heads * HEAD), F32)]
        hgrn = dict(layer=hgrn_layer, d_pool=d_pool)
    kern = functools.partial(_out_mlp_kernel, n_cast=len(to_cast), layer=layer, final_norm=final_norm,
                             ff_chunk=1024, sample_rows=sample is not None, hgrn=hgrn)
    return pl.pallas_call(
        kern,
        grid=(steps,),
        in_specs=in_specs + cast_in,
        out_specs=out_specs + cast_out,
        out_shape=out_shapes + cast_shapes,
        compiler_params=pltpu.CompilerParams(dimension_semantics=("arbitrary",),
                                             vmem_limit_bytes=VMEM_LIMIT_V7X),
        name="out_mlp",
    )(*args, *[w for w, _ in to_cast])


def _even_prompt_kernel(*refs, n_cast, layer, tb, tiles_per_seq):
    xn_ref, x0_ref, g_ref, win_ref, wpool_ref, pscale_ref, lbl_ref, gain_ref = refs[:8]
    cast_in, (mixed_ref, pstate_ref, hstate_ref) = refs[8:8 + n_cast], refs[8 + n_cast:11 + n_cast]
    cast_out = refs[11 + n_cast:11 + 2 * n_cast]
    (pu_ref, pq_ref, pf_ref, pi_ref, pg_ref, hn_ref, ext_ref, lvl_ref, st_ref, k_ref, lsplit_ref,
     bcum_ref, dec_ref, gate_ref, v_ref, qd_ref, kd_ref, ke_ref, qb_ref, kb_ref, qr_ref, kc_ref) = refs[11 + 2 * n_cast:]
    for src_ref, dst_ref in zip(cast_in, cast_out):
        dst_ref[...] = src_ref[...].astype(BF16)

    step = pl.program_id(0)
    d_pool = wpool_ref.shape[0]
    n_heads = st_ref.shape[0]
    d_hgrn = n_heads * HEAD
    nc = tb // HGRN_CHUNK
    nblk = tb // HGRN_BLOCK
    j = step % tiles_per_seq
    last = tiles_per_seq - 1
    o_q, o_f, o_i, o_g = d_pool, d_pool + d_hgrn, d_pool + 2 * d_hgrn, d_pool + 3 * d_hgrn
    sections = [(pu_ref, 0), (pq_ref, o_q), (pf_ref, o_f), (pi_ref, o_i), (pg_ref, o_g)]

    def chunks(dst_ref, col0):
        def make(c0, c1):
            def run():
                dst_ref[:, c0:c1] = jnp.dot(hn_ref[...], win_ref[:, col0 + c0:col0 + c1],
                                            preferred_element_type=F32)
            return run
        width = dst_ref.shape[1]
        return [make(c0, min(c0 + MXU_COLS_V7X, width)) for c0 in range(0, width, MXU_COLS_V7X)]

    @pl.when(step == 0)
    def _():
        hn_ref[...] = _rms(x0_ref[...], g_ref[...]).astype(BF16)
        for dst_ref, col0 in sections:
            for run in chunks(dst_ref, col0):
                run()

    r1_, r2_ = POOL_PAD + POOL_HIST, POOL_PAD + POOL_HIST + tb

    @pl.when(j == 0)
    def _():
        ext_ref[0:r1_, :] = jnp.zeros((r1_, d_pool), F32)
        lvl_ref[:, 0:POOL_PAD, :] = jnp.zeros((lvl_ref.shape[0], POOL_PAD, LANES), F32)
        st_ref[...] = jnp.zeros(st_ref.shape, F32)

    hn_ref[...] = _rms(xn_ref[...], g_ref[...]).astype(BF16)

    u = pu_ref[...]
    ext_ref[r1_:r2_, :] = u
    groups_per_tile = LANES // POOL_GROUP_DIM
    lane_grp = lax.broadcasted_iota(jnp.int32, (tb, LANES), 1) // POOL_GROUP_DIM
    pos = j * tb + lax.broadcasted_iota(jnp.int32, (tb, LANES), 0)
    means = []
    for lt in range(d_pool // LANES):
        wins = POOL_WINDOWS[lt * groups_per_tile:(lt + 1) * groups_per_tile]
        src, src_lanes, w, nbuf, got = ext_ref, slice(lt * LANES, (lt + 1) * LANES), 1, 0, {}
        while w < wins[-1]:
            new = src[POOL_PAD:r2_, src_lanes] + src[POOL_PAD - w:r2_ - w, src_lanes]
            w *= 2
            if w in wins:
                got[w] = new[POOL_HIST:]
            if w < wins[-1]:
                lvl_ref[nbuf, POOL_PAD:r2_, :] = new
                src, src_lanes, nbuf = lvl_ref.at[nbuf], slice(0, LANES), 1 - nbuf
        ssum, win = got[wins[-1]], jnp.full((tb, LANES), wins[-1], jnp.int32)
        for g_ in range(groups_per_tile - 2, -1, -1):
            ssum = jnp.where(lane_grp == g_, got[wins[g_]], ssum)
            win = jnp.where(lane_grp == g_, wins[g_], win)
        cnt = jnp.minimum(win, pos + 1).astype(F32)
        means.append(ssum / cnt)
    diff = (jnp.concatenate(means, axis=1) - u).astype(BF16)
    pool_out = jnp.dot(diff, wpool_ref[...], preferred_element_type=F32) * pscale_ref[...]
    mixed_ref[:, 0:d_pool] = pool_out.astype(BF16)

    @pl.when(j == last)
    def _():
        pstate_ref[...] = ext_ref[pl.ds(r2_ - POOL_BUF, POOL_BUF), :]

    ext_ref[POOL_PAD:r1_, :] = ext_ref[POOL_PAD + tb:r1_ + tb, :]
    for run in chunks(pu_ref, 0):
        run()

    lb = _lower_bound(lbl_ref[...], layer)
    for c in range(nc):
        rows = slice(c * HGRN_CHUNK, (c + 1) * HGRN_CHUNK)
        fz = pf_ref[rows, :]
        a = jnp.exp(-jnp.abs(fz))
        r = 1.0 / (1.0 + a)
        ar = a * r
        sig = jnp.where(fz >= 0, r, ar)
        sig_neg = jnp.where(fz >= 0, ar, r)
        logf = jnp.log(lb + (1.0 - lb) * sig)
        k_ref[rows, :] = (1.0 - lb) * sig_neg
        l_hi = logf.astype(BF16)
        r1 = logf - l_hi.astype(F32)
        l_mid = r1.astype(BF16)
        lsplit_ref[rows, 0:d_hgrn] = l_hi
        lsplit_ref[rows, d_hgrn:2 * d_hgrn] = l_mid
        lsplit_ref[rows, 2 * d_hgrn:3 * d_hgrn] = (r1 - l_mid.astype(F32)).astype(BF16)
        iz = pi_ref[rows, :]
        v_ref[rows, :] = (iz * _sigmoid(iz)).astype(BF16)
        gate_ref[rows, :] = gain_ref[...] * _sigmoid(pg_ref[rows, :])

    ri = lax.broadcasted_iota(jnp.int32, (tb, tb), 0)
    ci = lax.broadcasted_iota(jnp.int32, (tb, tb), 1)
    tri = ((ri // HGRN_CHUNK == ci // HGRN_CHUNK) & (ci <= ri)).astype(BF16)
    cs = jnp.dot(tri, lsplit_ref[...], preferred_element_type=F32)
    bcum_ref[...] = cs[:, 0:d_hgrn] + cs[:, d_hgrn:2 * d_hgrn] + cs[:, 2 * d_hgrn:3 * d_hgrn]
    for run in chunks(pf_ref, o_f) + chunks(pi_ref, o_i):
        run()

    nsub = HGRN_BLOCK // HGRN_CHUNK
    half = nsub // 2
    for b in range(nblk):
        blast = [bcum_ref[(b * nsub + i + 1) * HGRN_CHUNK - 1:(b * nsub + i + 1) * HGRN_CHUNK, :]
                 for i in range(nsub)]
        pre = [jnp.zeros_like(blast[0])]
        for i in range(nsub):
            pre.append(pre[-1] + blast[i])
        dec_ref[b:b + 1, :] = jnp.exp(pre[nsub])
        for i in range(nsub):
            c = b * nsub + i
            rows = slice(c * HGRN_CHUNK, (c + 1) * HGRN_CHUNK)
            bc = bcum_ref[rows, :]
            qd = pq_ref[rows, :] * jnp.exp(bc)
            kd = k_ref[rows, :] * jnp.exp(-bc)
            ke = kd * jnp.exp(blast[i])
            qd_ref[rows, :] = qd.astype(BF16)
            kd_ref[rows, :] = kd.astype(BF16)
            ke_ref[rows, :] = ke.astype(BF16)
            qb_ref[rows, :] = (qd * jnp.exp(pre[i])).astype(BF16)
            kb_ref[rows, :] = (ke * jnp.exp(pre[nsub] - pre[i + 1])).astype(BF16)
            if i >= half:
                qr_ref[rows, :] = (qd * jnp.exp(pre[i] - pre[half])).astype(BF16)
                kc_ref[rows, :] = jnp.zeros((HGRN_CHUNK, d_hgrn), BF16)
            else:
                kc_ref[rows, :] = (ke * jnp.exp(pre[half] - pre[i + 1])).astype(BF16)

    rb = lax.broadcasted_iota(jnp.int32, (HGRN_BLOCK, HGRN_BLOCK), 0)
    cb = lax.broadcasted_iota(jnp.int32, (HGRN_BLOCK, HGRN_BLOCK), 1)
    rsub, csub = rb // HGRN_CHUNK, cb // HGRN_CHUNK
    m_diag = (rsub == csub) & (cb <= rb)
    m_adj = (csub == rsub - 1) & (rsub != half)
    eye = rb == cb
    hb = HGRN_BLOCK // 2
    nt_dims = (((1,), (1,)), ((), ()))
    tn_dims = (((0,), (0,)), ((), ()))
    fillers = chunks(pq_ref, o_q) + chunks(pg_ref, o_g)

    def fill(n):
        for _ in range(min(n, len(fillers))):
            fillers.pop(0)()

    per_stage = -(-len(fillers) // (3 * nblk))
    for b in range(nblk):
        rows = slice(b * HGRN_BLOCK, (b + 1) * HGRN_BLOCK)
        far = slice(b * HGRN_BLOCK + hb, (b + 1) * HGRN_BLOCK)
        g12s, g3s = [], []
        for hd in range(n_heads):
            sl = slice(hd * HEAD, (hd + 1) * HEAD)
            kk = jnp.concatenate([kd_ref[rows, sl], ke_ref[rows, sl]], axis=0)
            g12s.append(lax.dot_general(qd_ref[rows, sl], kk, nt_dims, preferred_element_type=F32))
            g3s.append(lax.dot_general(qr_ref[far, sl], kc_ref[rows, sl], nt_dims, preferred_element_type=F32))
        fill(per_stage)
        outs = []
        for hd in range(n_heads):
            sl = slice(hd * HEAD, (hd + 1) * HEAD)
            g12 = g12s[hd]
            scores = jnp.where(m_diag, g12[:, 0:HGRN_BLOCK], jnp.where(m_adj, g12[:, HGRN_BLOCK:], 0.0))
            scores = jnp.concatenate([scores[0:hb], scores[hb:] + g3s[hd]], axis=0).astype(BF16)
            v_blk = v_ref[rows, sl]
            st = st_ref[hd]
            lhs = jnp.concatenate([scores, qb_ref[rows, sl]], axis=1)
            rhs = jnp.concatenate([v_blk, st.astype(BF16)], axis=0)
            outs.append(jnp.dot(lhs, rhs, preferred_element_type=F32))
            d_col = jnp.sum(jnp.where(eye, dec_ref[b:b + 1, sl], 0.0), axis=1, keepdims=True)
            st_ref[hd] = st * d_col + lax.dot_general(kb_ref[rows, sl], v_blk, tn_dims,
                                                      preferred_element_type=F32)
        fill(per_stage)
        for hd in range(n_heads):
            sl = slice(hd * HEAD, (hd + 1) * HEAD)
            o = outs[hd]
            o = o * lax.rsqrt(jnp.mean(o * o, axis=-1, keepdims=True) + EPS)
            mixed_ref[rows, d_pool + hd * HEAD:d_pool + (hd + 1) * HEAD] = (o * gate_ref[rows, sl]).astype(BF16)
        fill(per_stage)
    fill(len(fillers))

    @pl.when(j == last)
    def _():
        hstate_ref[...] = st_ref[...]


def even_mix_prompt(x, g, w_in, wpool_bd, pscale, lbl, gain, to_cast, *, layer, tb):
    b, t, d = x.shape
    n_in = w_in.shape[1]
    d_pool = wpool_bd.shape[0]
    d_hgrn = gain.shape[1]
    n_heads = d_hgrn // HEAD
    tps = t // tb
    n_tiles = b * tps
    cast_in, cast_out, cast_shapes = _cast_specs(to_cast, n_tiles)
    kern = functools.partial(_even_prompt_kernel, n_cast=len(to_cast), layer=layer, tb=tb, tiles_per_seq=tps)

    def next_tile(s):
        tile = jnp.minimum(s + 1, n_tiles - 1)
        return tile // tps, tile % tps

    assert d_pool % LANES == 0 and LANES % POOL_GROUP_DIM == 0 and tb % HGRN_BLOCK == 0
    return pl.pallas_call(
        kern,
        grid=(n_tiles,),
        in_specs=[pl.BlockSpec((None, tb, d), lambda s: (*next_tile(s), 0)),
                  pl.BlockSpec((None, tb, d), lambda s: (0, 0, 0)),
                  _resident((1, d)), _resident((d, n_in)), _resident((d_pool, d_pool)),
                  _resident((1, d_pool)), _resident(lbl.shape), _resident((1, d_hgrn))] + cast_in,
        out_specs=[pl.BlockSpec((None, tb, d), lambda s: (s // tps, s % tps, 0)),
                   pl.BlockSpec((None, POOL_BUF, d_pool), lambda s: (s // tps, 0, 0)),
                   pl.BlockSpec((None, n_heads, HEAD, HEAD), lambda s: (s // tps, 0, 0, 0))] + cast_out,
        out_shape=[jax.ShapeDtypeStruct((b, t, d), BF16),
                   jax.ShapeDtypeStruct((b, POOL_BUF, d_pool), F32),
                   jax.ShapeDtypeStruct((b, n_heads, HEAD, HEAD), F32)] + cast_shapes,
        scratch_shapes=[pltpu.VMEM((tb, d_pool), F32)]
                       + [pltpu.VMEM((tb, d_hgrn), F32)] * 4
                       + [pltpu.VMEM((tb, d), BF16),
                        pltpu.VMEM((POOL_PAD + POOL_HIST + tb, d_pool), F32),
                        pltpu.VMEM((2, POOL_PAD + POOL_HIST + tb, LANES), F32),
                        pltpu.VMEM((n_heads, HEAD, HEAD), F32),
                        pltpu.VMEM((tb, d_hgrn), F32),
                        pltpu.VMEM((tb, 3 * d_hgrn), BF16),
                        pltpu.VMEM((tb, d_hgrn), F32),
                        pltpu.VMEM((tb // HGRN_BLOCK, d_hgrn), F32),
                        pltpu.VMEM((tb, d_hgrn), F32)]
                       + [pltpu.VMEM((tb, d_hgrn), BF16)] * 8,
        compiler_params=pltpu.CompilerParams(dimension_semantics=("arbitrary",),
                                             vmem_limit_bytes=VMEM_LIMIT_V7X),
        name="even_mix_prompt",
    )(x, x, g, w_in, wpool_bd, pscale, lbl, gain, *[w for w, _ in to_cast])


def _even_sample_tail_kernel(proj_ref, oraw_ref, pool_ref, x_ref, wpool_ref, pscale_ref, gain_ref, wout_ref, g_ref,
                             w1_ref, w2_ref, gf_ref, o_ref, npool_ref, *, layer, final_norm, ff_chunk, pos0):
    d_pool = wpool_ref.shape[0]
    d_hgrn = gain_ref.shape[1]

    u = proj_ref[:, 0:d_pool]
    acc = u
    snaps = {}
    for s in range(1, POOL_WINDOWS[-1]):
        acc = acc + pool_ref[POOL_BUF - s]
        if s + 1 in POOL_WINDOWS:
            snaps[s + 1] = acc
    ssum, win = _pool_select(snaps, u.shape)
    cnt = jnp.minimum(win, pos0 + 1).astype(F32)
    diff = (ssum / cnt - u).astype(BF16)
    pool_out = jnp.dot(diff, wpool_ref[...], preferred_element_type=F32) * pscale_ref[...]
    npool_ref[0:POOL_BUF - 1] = pool_ref[1:POOL_BUF]
    npool_ref[POOL_BUF - 1] = u

    gate = gain_ref[...] * _sigmoid(proj_ref[:, d_pool + 3 * d_hgrn:d_pool + 4 * d_hgrn])
    mixed = [pool_out.astype(BF16)]
    for hd in range(d_hgrn // HEAD):
        sl = slice(hd * HEAD, (hd + 1) * HEAD)
        o = oraw_ref[:, sl]
        o = o * lax.rsqrt(jnp.mean(o * o, axis=-1, keepdims=True) + EPS)
        mixed.append((o * gate[:, sl]).astype(BF16))
    o_ref[...] = _mlp_block(jnp.concatenate(mixed, axis=1), x_ref[...], wout_ref, g_ref, w1_ref, w2_ref, gf_ref,
                            layer=layer, final_norm=final_norm, ff_chunk=ff_chunk)


def even_sample_tail(proj, o_raw, pool_rows, x_s, wpool_bd, pscale, gain, w_out, g_mlp, w1, w2, g_final,
                     *, layer, final_norm, pos0):
    b, n_in = proj.shape
    d = x_s.shape[1]
    d_pool = wpool_bd.shape[0]
    d_hgrn = gain.shape[1]
    d_ff = w1.shape[1]
    kern = functools.partial(_even_sample_tail_kernel, layer=layer, final_norm=final_norm, ff_chunk=1024, pos0=pos0)
    return pl.pallas_call(
        kern,
        grid=(1,),
        in_specs=[_resident((b, n_in)), _resident((b, d_hgrn)), _resident((POOL_BUF, b, d_pool)), _resident((b, d)),
                  _resident((d_pool, d_pool)), _resident((1, d_pool)), _resident((1, d_hgrn)),
                  _resident((d, d)), _resident(g_mlp.shape), _resident((d, d_ff)), _resident((d_ff, d)),
                  _resident((1, d))],
        out_specs=[pl.BlockSpec((b, d), lambda i: (0, 0)), pl.BlockSpec((POOL_BUF, b, d_pool), lambda i: (0, 0, 0))],
        out_shape=[jax.ShapeDtypeStruct((b, d), F32), jax.ShapeDtypeStruct((POOL_BUF, b, d_pool), F32)],
        compiler_params=pltpu.CompilerParams(dimension_semantics=("arbitrary",),
                                             vmem_limit_bytes=VMEM_LIMIT_V7X),
        name="even_sample_tail",
    )(proj, o_raw, pool_rows, x_s, wpool_bd, pscale, gain, w_out, g_mlp, w1, w2, g_final)


def _odd_kernel(x_ref, xs_ref, cst_ref, g_ref, win_ref, cw_ref, mixed_ref, cstate_ref, ms_ref, ncst_ref, zext_ref,
                *, tb):
    i, j = pl.program_id(0), pl.program_id(1)
    last = pl.num_programs(1) - 1
    dc = cw_ref.shape[1]

    def gated_taps(x):
        h = _rms(x, g_ref[...]).astype(BF16)
        cg = jnp.dot(h, win_ref[:, dc:2 * dc], preferred_element_type=F32)
        hv = jnp.dot(h, win_ref[:, 2 * dc:3 * dc], preferred_element_type=F32)
        z = cg * hv
        bg = jnp.dot(h, win_ref[:, 0:dc], preferred_element_type=F32)
        return bg, z, cw_ref[CONV_WIDTH - 1:CONV_WIDTH, :] * z

    @pl.when(j == 0)
    def _():
        zext_ref[0:CONV_HIST, :] = jnp.zeros((CONV_HIST, dc), F32)

    bg, z, conv = gated_taps(x_ref[...])
    zext_ref[CONV_HIST:CONV_HIST + tb, :] = z
    for s in range(1, CONV_WIDTH):
        conv = conv + cw_ref[CONV_WIDTH - 1 - s:CONV_WIDTH - s, :] * zext_ref[pl.ds(CONV_HIST - s, tb), :]
    mixed_ref[...] = (bg * conv).astype(BF16)

    @pl.when(j == last)
    def _():
        cstate_ref[...] = zext_ref[pl.ds(CONV_HIST + tb - CONV_BUF, CONV_BUF), :]

    zext_ref[0:CONV_HIST, :] = zext_ref[tb:tb + CONV_HIST, :]

    @pl.when((i == pl.num_programs(0) - 1) & (j == last))
    def _():
        bg_s, z_s, conv_s = gated_taps(xs_ref[...])
        for s in range(1, CONV_WIDTH):
            r0 = (CONV_BUF - s) * dc
            conv_s = conv_s + cw_ref[CONV_WIDTH - 1 - s:CONV_WIDTH - s, :] * cst_ref[:, r0:r0 + dc]
        ms_ref[...] = (bg_s * conv_s).astype(BF16)
        ncst_ref[:, 0:(CONV_BUF - 1) * dc] = cst_ref[:, dc:CONV_BUF * dc]
        ncst_ref[:, (CONV_BUF - 1) * dc:CONV_BUF * dc] = z_s


def odd_mix(x, x_s, cstate_s, g, w_in, conv_w, *, tb):
    b, t, d = x.shape
    ms = x_s.shape[0]
    dc = conv_w.shape[1]
    kern = functools.partial(_odd_kernel, tb=tb)
    return pl.pallas_call(
        kern,
        grid=(b, t // tb),
        in_specs=[pl.BlockSpec((None, tb, d), lambda i, j: (i, j, 0)),
                  _resident((ms, d)), _resident((ms, CONV_BUF * dc)),
                  _resident((1, d)), _resident((d, 3 * dc)), _resident((CONV_WIDTH, dc))],
        out_specs=[pl.BlockSpec((None, tb, dc), lambda i, j: (i, j, 0)),
                   pl.BlockSpec((None, CONV_BUF, dc), lambda i, j: (i, 0, 0)),
                   pl.BlockSpec((ms, dc), lambda i, j: (0, 0)),
                   pl.BlockSpec((ms, CONV_BUF * dc), lambda i, j: (0, 0))],
        out_shape=[jax.ShapeDtypeStruct((b, t, dc), BF16),
                   jax.ShapeDtypeStruct((b, CONV_BUF, dc), F32),
                   jax.ShapeDtypeStruct((ms, dc), BF16),
                   jax.ShapeDtypeStruct((ms, CONV_BUF * dc), F32)],
        scratch_shapes=[pltpu.VMEM((CONV_HIST + tb, dc), F32)],
        compiler_params=pltpu.CompilerParams(dimension_semantics=("arbitrary", "arbitrary"),
                                             vmem_limit_bytes=VMEM_LIMIT_V7X),
        name="odd_mix",
    )(x, x_s, cstate_s, g, w_in, conv_w)


def _block_diag(w):
    g, c, _ = w.shape
    rows = [jnp.pad(w[i], ((0, 0), (i * c, (g - 1 - i) * c))) for i in range(g)]
    return jnp.concatenate(rows, axis=0)


def kernel(x_prompt, x_sample, state_pool, state_hgrn, state_conv, norm_mix, norm_mlp, norm_final, even_w_in, pool_w, pool_scale, hgrn_lb_logits, hgrn_gain, even_w_out, odd_w_in, conv_w, odd_w_out, ff_w1, ff_w2):
    depth = norm_mix.shape[0]
    b, t, d = x_prompt.shape
    db, ds, _ = x_sample.shape
    assert ds == 1, "the sample group carries one token per sequence"
    xp = x_prompt.reshape(b * t, d)
    xs = x_sample.reshape(db, d)
    pool_p, hgrn_p, conv_p, pool_s, hgrn_s, conv_s = [], [], [], [], [], []

    def layer_weights(l):
        w_in, w_out = (even_w_in, even_w_out) if l % 2 == 0 else (odd_w_in, odd_w_out)
        return dict(w_in=(w_in[l // 2], None), w_out=(w_out[l // 2], None), w1=(ff_w1, l), w2=(ff_w2, l))

    def cast_now(w, layer):
        return (w if layer is None else w[layer]).astype(BF16)

    have = {}
    for l in range(depth):
        g_mix = norm_mix[l][None]
        wl = layer_weights(l)
        missing = [k for k in ("w_out", "w1", "w2") if k not in have]
        if l % 2 == 0:
            e = l // 2
            pool_bd = _block_diag(pool_w[e]).astype(BF16)
            shared = (pool_bd, pool_scale[e][None], hgrn_lb_logits, hgrn_gain[e][None])
            proj_s, have["w_in"] = norm_proj(xs, g_mix, have.get("w_in", wl["w_in"][0]), col_blocks=2)
            mixed_p, p_new, s_new, *cast = even_mix_prompt(xp.reshape(b, t, d), g_mix, have["w_in"], *shared,
                                                           [wl[k] for k in missing], layer=l, tb=256)
            have.update(zip(missing, cast))
            pool_p.append(p_new)
            hgrn_p.append(s_new)
            nxt = layer_weights(l + 1) if l + 1 < depth else {}
            xp, s_new, o_raw, *cast = out_mlp(
                mixed_p.reshape(b * t, d), xp, None, have["w_out"], norm_mlp, have["w1"], have["w2"],
                norm_final[None], list(nxt.values()),
                (proj_s, state_hgrn[e], hgrn_lb_logits, pool_bd.shape[0], l),
                layer=l, final_norm=l == depth - 1, bm=512)
            hgrn_s.append(s_new)
            xs, p_new = even_sample_tail(proj_s, o_raw.reshape(db, -1), jnp.swapaxes(state_pool[e], 0, 1), xs,
                                         pool_bd, pool_scale[e][None], hgrn_gain[e][None], have["w_out"], norm_mlp,
                                         have["w1"], have["w2"], norm_final[None],
                                         layer=l, final_norm=l == depth - 1, pos0=PAST_LEN)
            pool_s.append(jnp.swapaxes(p_new, 0, 1))
            have = dict(zip(nxt, cast))
            continue
        else:
            o = l // 2
            have.update({k: cast_now(*wl[k]) for k in ["w_in"] + missing if k not in have})
            mixed_p, c_new, mixed_s, cs_new = odd_mix(xp.reshape(b, t, d), xs, state_conv[o].reshape(db, -1), g_mix,
                                                      have["w_in"], conv_w[o], tb=512)
            conv_p.append(c_new)
            conv_s.append(cs_new.reshape(db, CONV_BUF, -1))
        nxt = layer_weights(l + 1) if l + 1 < depth else {}
        xp, xs, *cast = out_mlp(mixed_p.reshape(b * t, d), xp, (mixed_s, xs), have["w_out"], norm_mlp, have["w1"],
                                have["w2"], norm_final[None], list(nxt.values()),
                                layer=l, final_norm=l == depth - 1, bm=512)
        have = dict(zip(nxt, cast))
    return (xp.reshape(b, t, d), xs.reshape(db, ds, d), jnp.stack(pool_p), jnp.stack(hgrn_p), jnp.stack(conv_p),
            jnp.stack(pool_s), jnp.stack(hgrn_s), jnp.stack(conv_s))
```

```python
import functools

import jax
import jax.numpy as jnp
from jax import lax
from jax.experimental import pallas as pl
from jax.experimental.pallas import tpu as pltpu

F32 = jnp.float32
BF16 = jnp.bfloat16

EPS = 1e-6
PAST_LEN = 16384
POOL_WINDOWS = (2, 4, 8, 16)
POOL_GROUP_DIM = 64
POOL_BUF = max(POOL_WINDOWS) - 1
POOL_HIST = 16
POOL_PAD = 8
LANES = 128
HGRN_CHUNK = 32
HGRN_BLOCK = 128
HEAD = 128
CONV_WIDTH = 3
CONV_BUF = CONV_WIDTH - 1
CONV_HIST = 8

VMEM_LIMIT_V7X = 56 * 1024 * 1024
MXU_COLS_V7X = 256


def _resident(shape):
    nd = len(shape)
    return pl.BlockSpec(shape, lambda *_: (0,) * nd, pipeline_mode=pl.Buffered(1))


def _cast_specs(weights, n_steps):
    in_specs, out_specs, out_shapes = [], [], []
    for w, layer in weights:
        r, c = w.shape[-2:]
        assert r % (n_steps * 16) == 0, "row slabs must be whole bf16 sublane tiles"
        if layer is None:
            in_specs.append(pl.BlockSpec((r // n_steps, c), lambda s: (s, 0)))
        else:
            in_specs.append(pl.BlockSpec((None, r // n_steps, c), lambda s, layer=layer: (layer, s, 0)))
        out_specs.append(pl.BlockSpec((r // n_steps, c), lambda s: (s, 0)))
        out_shapes.append(jax.ShapeDtypeStruct((r, c), BF16))
    return in_specs, out_specs, out_shapes


def _rms(x, g):
    return x * lax.rsqrt(jnp.mean(x * x, axis=-1, keepdims=True) + EPS) * g


def _sigmoid(x):
    return 1.0 / (1.0 + jnp.exp(-x))


def _lower_bound(lbl, layer):
    e = jnp.exp(lbl - jnp.max(lbl, axis=0, keepdims=True))
    p = e / jnp.sum(e, axis=0, keepdims=True)
    return jnp.sum(p[0:layer + 1], axis=0, keepdims=True)


def _pool_select(snaps, u_shape):
    grp = lax.broadcasted_iota(jnp.int32, u_shape, 1) // POOL_GROUP_DIM
    s = snaps[POOL_WINDOWS[-1]]
    for g in range(len(POOL_WINDOWS) - 2, -1, -1):
        s = jnp.where(grp == g, snaps[POOL_WINDOWS[g]], s)
    win = jnp.left_shift(2, grp)
    return s, win


def _norm_proj_kernel(x_ref, g_ref, w_ref, o_ref, *wb_ref):
    h = _rms(x_ref[...], g_ref[...]).astype(BF16)
    w = w_ref[...].astype(BF16)
    for ref in wb_ref:
        ref[...] = w
    o_ref[...] = jnp.dot(h, w, preferred_element_type=F32)


def norm_proj(x, g, w, *, col_blocks):
    m, d = x.shape
    n = w.shape[1]
    bn = n // col_blocks
    assert n % col_blocks == 0 and bn % LANES == 0
    cast = w.dtype != BF16
    outs = pl.pallas_call(
        _norm_proj_kernel,
        grid=(col_blocks,),
        in_specs=[_resident((m, d)), _resident((1, d)), pl.BlockSpec((d, bn), lambda i: (0, i))],
        out_specs=[pl.BlockSpec((m, bn), lambda i: (0, i))] + [pl.BlockSpec((d, bn), lambda i: (0, i))] * cast,
        out_shape=[jax.ShapeDtypeStruct((m, n), F32)] + [jax.ShapeDtypeStruct((d, n), BF16)] * cast,
        compiler_params=pltpu.CompilerParams(dimension_semantics=("arbitrary",),
                                             vmem_limit_bytes=VMEM_LIMIT_V7X),
        name="norm_proj",
    )(x, g, w)
    return outs if cast else (outs[0], w)


def _mlp_block(mixed, x, wout_ref, g_ref, w1_ref, w2_ref, gf_ref, *, layer, final_norm, ff_chunk, between=None):
    d_ff = w1_ref.shape[1]
    y = x + jnp.dot(mixed, wout_ref[...], preferred_element_type=F32)
    h = _rms(y, g_ref[layer:layer + 1, :]).astype(BF16)
    acc = y
    for c in range(d_ff // ff_chunk):
        a = jnp.dot(h, w1_ref[:, c * ff_chunk:(c + 1) * ff_chunk], preferred_element_type=F32)
        a = jnp.square(jnp.maximum(a, 0.0)).astype(BF16)
        acc = acc + jnp.dot(a, w2_ref[c * ff_chunk:(c + 1) * ff_chunk, :], preferred_element_type=F32)
        if c == 0 and between is not None:
            between()
    return _rms(acc, gf_ref[...]) if final_norm else acc


def _hgrn_sample_step(pj_ref, hst_ref, lbl_ref, nhst_ref, oraw_ref, *, layer, d_pool):
    sb, n_heads = hst_ref.shape[0], hst_ref.shape[1]
    d_hgrn = n_heads * HEAD
    lb = _lower_bound(lbl_ref[...], layer)
    fz = pj_ref[:, d_pool + d_hgrn:d_pool + 2 * d_hgrn]
    iz = pj_ref[:, d_pool + 2 * d_hgrn:d_pool + 3 * d_hgrn]
    v = iz * _sigmoid(iz)
    fkq = (lb + (1.0 - lb) * _sigmoid(fz), (1.0 - lb) * _sigmoid(-fz), pj_ref[:, d_pool:d_pool + d_hgrn])

    eye = (lax.broadcasted_iota(jnp.int32, (HEAD, HEAD), 0) == lax.broadcasted_iota(jnp.int32, (HEAD, HEAD), 1))

    def to_column(row):
        return jnp.sum(jnp.where(eye, row, 0.0), axis=1, keepdims=True)

    for i in range(sb):
        for hd in range(n_heads):
            sl = slice(hd * HEAD, (hd + 1) * HEAD)
            f_col, k_col, q_col = [to_column(x[i:i + 1, sl]) for x in fkq]
            s_new = f_col * hst_ref[i, hd] + k_col * v[i:i + 1, sl]
            nhst_ref[i, hd] = s_new
            oraw_ref[i:i + 1, sl] = jnp.sum(q_col * s_new, axis=0, keepdims=True)


def _out_mlp_kernel(*refs, n_cast, layer, final_norm, ff_chunk, sample_rows, hgrn):
    refs = list(refs)
    m_ref, x_ref = refs[:2]
    del refs[:2]
    if sample_rows:
        ms_ref, xs_ref = refs[:2]
        del refs[:2]
    weights = refs[:5]
    del refs[:5]
    if hgrn is not None:
        pj_ref, hst_ref, lbl_ref = refs[:3]
        del refs[:3]
    cast_in = refs[:n_cast]
    del refs[:n_cast]
    o_ref = refs.pop(0)
    if sample_rows:
        os_ref = refs.pop(0)
    if hgrn is not None:
        nhst_ref, oraw_ref = refs[:2]
        del refs[:2]
    cast_out = refs
    for src_ref, dst_ref in zip(cast_in, cast_out):
        dst_ref[...] = src_ref[...].astype(BF16)

    between = None
    if hgrn is not None:
        between = functools.partial(_hgrn_sample_step, pj_ref, hst_ref, lbl_ref, nhst_ref, oraw_ref, **hgrn)
    o_ref[...] = _mlp_block(m_ref[...], x_ref[...], *weights, layer=layer, final_norm=final_norm,
                            ff_chunk=ff_chunk, between=between)

    if sample_rows:
        @pl.when(pl.program_id(0) == pl.num_programs(0) - 1)
        def _():
            os_ref[...] = _mlp_block(ms_ref[...], xs_ref[...], *weights, layer=layer, final_norm=final_norm,
                                     ff_chunk=ff_chunk)


def out_mlp(mixed, x, sample, w_out, g_mlp, w1, w2, g_final, to_cast, hgrn_sample=None, *, layer, final_norm, bm):
    m, d = x.shape
    d_ff = w1.shape[1]
    steps = m // bm
    cast_in, cast_out, cast_shapes = _cast_specs(to_cast, steps)
    args = [mixed, x]
    in_specs = [pl.BlockSpec((bm, d), lambda i: (i, 0)), pl.BlockSpec((bm, d), lambda i: (i, 0))]
    out_specs = [pl.BlockSpec((bm, d), lambda i: (i, 0))]
    out_shapes = [jax.ShapeDtypeStruct((m, d), F32)]
    if sample is not None:
        ms = sample[1].shape[0]
        args += list(sample)
        in_specs += [_resident((ms, d)), _resident((ms, d))]
        out_specs.append(pl.BlockSpec((ms, d), lambda i: (0, 0)))
        out_shapes.append(jax.ShapeDtypeStruct((ms, d), F32))
    args += [w_out, g_mlp, w1, w2, g_final]
    in_specs += [_resident((d, d)), _resident(g_mlp.shape), _resident((d, d_ff)), _resident((d_ff, d)),
                 _resident((1, d))]
    hgrn = None
    if hgrn_sample is not None:
        proj_s, state, lbl, d_pool, hgrn_layer = hgrn_sample
        nseq, n_in = proj_s.shape
        assert nseq % steps == 0
        sb = nseq // steps
        n_heads = state.shape[1]
        args += [proj_s.reshape(steps, sb, n_in), state, lbl]
        in_specs += [pl.BlockSpec((None, sb, n_in), lambda i: (i, 0, 0)),
                     pl.BlockSpec((sb, n_heads, HEAD, HEAD), lambda i: (i, 0, 0, 0)), _resident(lbl.shape)]
        out_specs += [pl.BlockSpec((sb, n_heads, HEAD, HEAD), lambda i: (i, 0, 0, 0)),
                      pl.BlockSpec((None, sb, n_heads * HEAD), lambda i: (i, 0, 0))]
        out_shapes += [jax.ShapeDtypeStruct(state.shape, F32),
                       jax.ShapeDtypeStruct((steps, sb, n_heads * HEAD), F32)]
        hgrn = dict(layer=hgrn_layer, d_pool=d_pool)
    kern = functools.partial(_out_mlp_kernel, n_cast=len(to_cast), layer=layer, final_norm=final_norm,
                             ff_chunk=1024, sample_rows=sample is not None, hgrn=hgrn)
    return pl.pallas_call(
        kern,
        grid=(steps,),
        in_specs=in_specs + cast_in,
        out_specs=out_specs + cast_out,
        out_shape=out_shapes + cast_shapes,
        compiler_params=pltpu.CompilerParams(dimension_semantics=("arbitrary",),
                                             vmem_limit_bytes=VMEM_LIMIT_V7X),
        name="out_mlp",
    )(*args, *[w for w, _ in to_cast])


def _even_prompt_kernel(*refs, n_cast, layer, tb, tiles_per_seq):
    xn_ref, x0_ref, g_ref, win_ref, wpool_ref, pscale_ref, lbl_ref, gain_ref = refs[:8]
    cast_in, (mixed_ref, pstate_ref, hstate_ref) = refs[8:8 + n_cast], refs[8 + n_cast:11 + n_cast]
    cast_out = refs[11 + n_cast:11 + 2 * n_cast]
    (pu_ref, pq_ref, pf_ref, pi_ref, pg_ref, hn_ref, ext_ref, lvl_ref, st_ref, k_ref, lsplit_ref,
     bcum_ref, dec_ref, gate_ref, v_ref, qd_ref, kd_ref, ke_ref, qb_ref, kb_ref, qr_ref, kc_ref) = refs[11 + 2 * n_cast:]
    for src_ref, dst_ref in zip(cast_in, cast_out):
        dst_ref[...] = src_ref[...].astype(BF16)

    step = pl.program_id(0)
    d_pool = wpool_ref.shape[0]
    n_heads = st_ref.shape[0]
    d_hgrn = n_heads * HEAD
    nc = tb // HGRN_CHUNK
    nblk = tb // HGRN_BLOCK
    j = step % tiles_per_seq
    o_q, o_f, o_i, o_g = d_pool, d_pool + d_hgrn, d_pool + 2 * d_hgrn, d_pool + 3 * d_hgrn
    sections = [(pu_ref, 0), (pq_ref, o_q), (pf_ref, o_f), (pi_ref, o_i), (pg_ref, o_g)]

    def chunks(dst_ref, col0):
        def make(c0, c1):
            def run():
                dst_ref[:, c0:c1] = jnp.dot(hn_ref[...], win_ref[:, col0 + c0:col0 + c1],
                                            preferred_element_type=F32)
            return run
        width = dst_ref.shape[1]
        return [make(c0, min(c0 + MXU_COLS_V7X, width)) for c0 in range(0, width, MXU_COLS_V7X)]

    r1_, r2_ = POOL_PAD + POOL_HIST, POOL_PAD + POOL_HIST + tb

    @pl.when(step == 0)
    def _():
        hn_ref[...] = _rms(x0_ref[...], g_ref[...]).astype(BF16)
        for dst_ref, col0 in sections:
            for run in chunks(dst_ref, col0):
                run()
        ext_ref[0:r1_, :] = jnp.zeros((r1_, d_pool), F32)
        lvl_ref[:, 0:POOL_PAD, :] = jnp.zeros((lvl_ref.shape[0], POOL_PAD, LANES), F32)
        st_ref[...] = jnp.zeros(st_ref.shape, F32)

    seq_start = j == 0
    hn_ref[...] = _rms(xn_ref[...], g_ref[...]).astype(BF16)

    u = pu_ref[...]
    ext_ref[r1_:r2_, :] = u
    ext_ref[0:r1_, :] = jnp.where(seq_start, 0.0, ext_ref[0:r1_, :])
    groups_per_tile = LANES // POOL_GROUP_DIM
    lane_grp = lax.broadcasted_iota(jnp.int32, (tb, LANES), 1) // POOL_GROUP_DIM
    pos = j * tb + lax.broadcasted_iota(jnp.int32, (tb, LANES), 0)
    means = []
    for lt in range(d_pool // LANES):
        wins = POOL_WINDOWS[lt * groups_per_tile:(lt + 1) * groups_per_tile]
        src, src_lanes, w, nbuf, got = ext_ref, slice(lt * LANES, (lt + 1) * LANES), 1, 0, {}
        while w < wins[-1]:
            new = src[POOL_PAD:r2_, src_lanes] + src[POOL_PAD - w:r2_ - w, src_lanes]
            w *= 2
            if w in wins:
                got[w] = new[POOL_HIST:]
            if w < wins[-1]:
                lvl_ref[nbuf, POOL_PAD:r2_, :] = new
                src, src_lanes, nbuf = lvl_ref.at[nbuf], slice(0, LANES), 1 - nbuf
        ssum, win = got[wins[-1]], jnp.full((tb, LANES), wins[-1], jnp.int32)
        for g_ in range(groups_per_tile - 2, -1, -1):
            ssum = jnp.where(lane_grp == g_, got[wins[g_]], ssum)
            win = jnp.where(lane_grp == g_, wins[g_], win)
        cnt = jnp.minimum(win, pos + 1).astype(F32)
        means.append(ssum / cnt)
    diff = (jnp.concatenate(means, axis=1) - u).astype(BF16)
    pool_out = jnp.dot(diff, wpool_ref[...], preferred_element_type=F32) * pscale_ref[...]
    mixed_ref[:, 0:d_pool] = pool_out.astype(BF16)
    pstate_ref[...] = ext_ref[pl.ds(r2_ - POOL_BUF, POOL_BUF), :]
    ext_ref[POOL_PAD:r1_, :] = ext_ref[POOL_PAD + tb:r1_ + tb, :]
    for run in chunks(pu_ref, 0):
        run()

    lb = _lower_bound(lbl_ref[...], layer)
    for c in range(nc):
        rows = slice(c * HGRN_CHUNK, (c + 1) * HGRN_CHUNK)
        fz = pf_ref[rows, :]
        a = jnp.exp(-jnp.abs(fz))
        r = 1.0 / (1.0 + a)
        ar = a * r
        sig = jnp.where(fz >= 0, r, ar)
        sig_neg = jnp.where(fz >= 0, ar, r)
        logf = jnp.log(lb + (1.0 - lb) * sig)
        k_ref[rows, :] = (1.0 - lb) * sig_neg
        l_hi = logf.astype(BF16)
        r1 = logf - l_hi.astype(F32)
        l_mid = r1.astype(BF16)
        lsplit_ref[rows, 0:d_hgrn] = l_hi
        lsplit_ref[rows, d_hgrn:2 * d_hgrn] = l_mid
        lsplit_ref[rows, 2 * d_hgrn:3 * d_hgrn] = (r1 - l_mid.astype(F32)).astype(BF16)
        iz = pi_ref[rows, :]
        v_ref[rows, :] = (iz * _sigmoid(iz)).astype(BF16)
        gate_ref[rows, :] = gain_ref[...] * _sigmoid(pg_ref[rows, :])

    ri = lax.broadcasted_iota(jnp.int32, (tb, tb), 0)
    ci = lax.broadcasted_iota(jnp.int32, (tb, tb), 1)
    tri = ((ri // HGRN_CHUNK == ci // HGRN_CHUNK) & (ci <= ri)).astype(BF16)
    cs = jnp.dot(tri, lsplit_ref[...], preferred_element_type=F32)
    bcum_ref[...] = cs[:, 0:d_hgrn] + cs[:, d_hgrn:2 * d_hgrn] + cs[:, 2 * d_hgrn:3 * d_hgrn]
    for run in chunks(pf_ref, o_f) + chunks(pi_ref, o_i):
        run()

    nsub = HGRN_BLOCK // HGRN_CHUNK
    half = nsub // 2
    for b in range(nblk):
        blast = [bcum_ref[(b * nsub + i + 1) * HGRN_CHUNK - 1:(b * nsub + i + 1) * HGRN_CHUNK, :]
                 for i in range(nsub)]
        pre = [jnp.zeros_like(blast[0])]
        for i in range(nsub):
            pre.append(pre[-1] + blast[i])
        dec_ref[b:b + 1, :] = jnp.exp(pre[nsub])
        for i in range(nsub):
            c = b * nsub + i
            rows = slice(c * HGRN_CHUNK, (c + 1) * HGRN_CHUNK)
            bc = bcum_ref[rows, :]
            qd = pq_ref[rows, :] * jnp.exp(bc)
            kd = k_ref[rows, :] * jnp.exp(-bc)
            ke = kd * jnp.exp(blast[i])
            qd_ref[rows, :] = qd.astype(BF16)
            kd_ref[rows, :] = kd.astype(BF16)
            ke_ref[rows, :] = ke.astype(BF16)
            qb_ref[rows, :] = (qd * jnp.exp(pre[i])).astype(BF16)
            kb_ref[rows, :] = (ke * jnp.exp(pre[nsub] - pre[i + 1])).astype(BF16)
            if i >= half:
                qr_ref[rows, :] = (qd * jnp.exp(pre[i] - pre[half])).astype(BF16)
                kc_ref[rows, :] = jnp.zeros((HGRN_CHUNK, d_hgrn), BF16)
            else:
                kc_ref[rows, :] = (ke * jnp.exp(pre[half] - pre[i + 1])).astype(BF16)

    rb = lax.broadcasted_iota(jnp.int32, (HGRN_BLOCK, HGRN_BLOCK), 0)
    cb = lax.broadcasted_iota(jnp.int32, (HGRN_BLOCK, HGRN_BLOCK), 1)
    rsub, csub = rb // HGRN_CHUNK, cb // HGRN_CHUNK
    m_diag = (rsub == csub) & (cb <= rb)
    m_adj = (csub == rsub - 1) & (rsub != half)
    eye = rb == cb
    hb = HGRN_BLOCK // 2
    nt_dims = (((1,), (1,)), ((), ()))
    tn_dims = (((0,), (0,)), ((), ()))
    fillers = chunks(pq_ref, o_q) + chunks(pg_ref, o_g)

    def fill(n):
        for _ in range(min(n, len(fillers))):
            fillers.pop(0)()

    per_stage = -(-len(fillers) // (3 * nblk))
    for b in range(nblk):
        rows = slice(b * HGRN_BLOCK, (b + 1) * HGRN_BLOCK)
        far = slice(b * HGRN_BLOCK + hb, (b + 1) * HGRN_BLOCK)
        g12s, g3s = [], []
        for hd in range(n_heads):
            sl = slice(hd * HEAD, (hd + 1) * HEAD)
            kk = jnp.concatenate([kd_ref[rows, sl], ke_ref[rows, sl]], axis=0)
            g12s.append(lax.dot_general(qd_ref[rows, sl], kk, nt_dims, preferred_element_type=F32))
            g3s.append(lax.dot_general(qr_ref[far, sl], kc_ref[rows, sl], nt_dims, preferred_element_type=F32))
        fill(per_stage)
        outs = []
        for hd in range(n_heads):
            sl = slice(hd * HEAD, (hd + 1) * HEAD)
            g12 = g12s[hd]
            scores = jnp.where(m_diag, g12[:, 0:HGRN_BLOCK], jnp.where(m_adj, g12[:, HGRN_BLOCK:], 0.0))
            scores = jnp.concatenate([scores[0:hb], scores[hb:] + g3s[hd]], axis=0).astype(BF16)
            v_blk = v_ref[rows, sl]
            st = st_ref[hd]
            if b == 0:
                st = jnp.where(seq_start, 0.0, st)
            lhs = jnp.concatenate([scores, qb_ref[rows, sl]], axis=1)
            rhs = jnp.concatenate([v_blk, st.astype(BF16)], axis=0)
            outs.append(jnp.dot(lhs, rhs, preferred_element_type=F32))
            d_col = jnp.sum(jnp.where(eye, dec_ref[b:b + 1, sl], 0.0), axis=1, keepdims=True)
            st_ref[hd] = st * d_col + lax.dot_general(kb_ref[rows, sl], v_blk, tn_dims,
                                                      preferred_element_type=F32)
        fill(per_stage)
        for hd in range(n_heads):
            sl = slice(hd * HEAD, (hd + 1) * HEAD)
            o = outs[hd]
            o = o * lax.rsqrt(jnp.mean(o * o, axis=-1, keepdims=True) + EPS)
            mixed_ref[rows, d_pool + hd * HEAD:d_pool + (hd + 1) * HEAD] = (o * gate_ref[rows, sl]).astype(BF16)
        fill(per_stage)
    fill(len(fillers))
    hstate_ref[...] = st_ref[...]


def even_mix_prompt(x, g, w_in, wpool_bd, pscale, lbl, gain, to_cast, *, layer, tb):
    b, t, d = x.shape
    n_in = w_in.shape[1]
    d_pool = wpool_bd.shape[0]
    d_hgrn = gain.shape[1]
    n_heads = d_hgrn // HEAD
    tps = t // tb
    n_tiles = b * tps
    cast_in, cast_out, cast_shapes = _cast_specs(to_cast, n_tiles)
    kern = functools.partial(_even_prompt_kernel, n_cast=len(to_cast), layer=layer, tb=tb, tiles_per_seq=tps)

    def next_tile(s):
        tile = jnp.minimum(s + 1, n_tiles - 1)
        return tile // tps, tile % tps

    assert d_pool % LANES == 0 and LANES % POOL_GROUP_DIM == 0 and tb % HGRN_BLOCK == 0
    return pl.pallas_call(
        kern,
        grid=(n_tiles,),
        in_specs=[pl.BlockSpec((None, tb, d), lambda s: (*next_tile(s), 0)),
                  pl.BlockSpec((None, tb, d), lambda s: (0, 0, 0)),
                  _resident((1, d)), _resident((d, n_in)), _resident((d_pool, d_pool)),
                  _resident((1, d_pool)), _resident(lbl.shape), _resident((1, d_hgrn))] + cast_in,
        out_specs=[pl.BlockSpec((None, tb, d), lambda s: (s // tps, s % tps, 0)),
                   pl.BlockSpec((None, POOL_BUF, d_pool), lambda s: (s // tps, 0, 0)),
                   pl.BlockSpec((None, n_heads, HEAD, HEAD), lambda s: (s // tps, 0, 0, 0))] + cast_out,
        out_shape=[jax.ShapeDtypeStruct((b, t, d), BF16),
                   jax.ShapeDtypeStruct((b, POOL_BUF, d_pool), F32),
                   jax.ShapeDtypeStruct((b, n_heads, HEAD, HEAD), F32)] + cast_shapes,
        scratch_shapes=[pltpu.VMEM((tb, d_pool), F32)]
                       + [pltpu.VMEM((tb, d_hgrn), F32)] * 4
                       + [pltpu.VMEM((tb, d), BF16),
                        pltpu.VMEM((POOL_PAD + POOL_HIST + tb, d_pool), F32),
                        pltpu.VMEM((2, POOL_PAD + POOL_HIST + tb, LANES), F32),
                        pltpu.VMEM((n_heads, HEAD, HEAD), F32),
                        pltpu.VMEM((tb, d_hgrn), F32),
                        pltpu.VMEM((tb, 3 * d_hgrn), BF16),
                        pltpu.VMEM((tb, d_hgrn), F32),
                        pltpu.VMEM((tb // HGRN_BLOCK, d_hgrn), F32),
                        pltpu.VMEM((tb, d_hgrn), F32)]
                       + [pltpu.VMEM((tb, d_hgrn), BF16)] * 8,
        compiler_params=pltpu.CompilerParams(dimension_semantics=("arbitrary",),
                                             vmem_limit_bytes=VMEM_LIMIT_V7X),
        name="even_mix_prompt",
    )(x, x, g, w_in, wpool_bd, pscale, lbl, gain, *[w for w, _ in to_cast])


def _even_sample_tail_kernel(proj_ref, oraw_ref, pool_ref, x_ref, wpool_ref, pscale_ref, gain_ref, wout_ref, g_ref,
                             w1_ref, w2_ref, gf_ref, o_ref, npool_ref, *, layer, final_norm, ff_chunk, pos0):
    d_pool = wpool_ref.shape[0]
    d_hgrn = gain_ref.shape[1]

    u = proj_ref[:, 0:d_pool]
    acc = u
    snaps = {}
    for s in range(1, POOL_WINDOWS[-1]):
        acc = acc + pool_ref[POOL_BUF - s]
        if s + 1 in POOL_WINDOWS:
            snaps[s + 1] = acc
    ssum, win = _pool_select(snaps, u.shape)
    cnt = jnp.minimum(win, pos0 + 1).astype(F32)
    diff = (ssum / cnt - u).astype(BF16)
    pool_out = jnp.dot(diff, wpool_ref[...], preferred_element_type=F32) * pscale_ref[...]
    npool_ref[0:POOL_BUF - 1] = pool_ref[1:POOL_BUF]
    npool_ref[POOL_BUF - 1] = u

    gate = gain_ref[...] * _sigmoid(proj_ref[:, d_pool + 3 * d_hgrn:d_pool + 4 * d_hgrn])
    mixed = [pool_out.astype(BF16)]
    for hd in range(d_hgrn // HEAD):
        sl = slice(hd * HEAD, (hd + 1) * HEAD)
        o = oraw_ref[:, sl]
        o = o * lax.rsqrt(jnp.mean(o * o, axis=-1, keepdims=True) + EPS)
        mixed.append((o * gate[:, sl]).astype(BF16))
    o_ref[...] = _mlp_block(jnp.concatenate(mixed, axis=1), x_ref[...], wout_ref, g_ref, w1_ref, w2_ref, gf_ref,
                            layer=layer, final_norm=final_norm, ff_chunk=ff_chunk)


def even_sample_tail(proj, o_raw, pool_rows, x_s, wpool_bd, pscale, gain, w_out, g_mlp, w1, w2, g_final,
                     *, layer, final_norm, pos0):
    b, n_in = proj.shape
    d = x_s.shape[1]
    d_pool = wpool_bd.shape[0]
    d_hgrn = gain.shape[1]
    d_ff = w1.shape[1]
    kern = functools.partial(_even_sample_tail_kernel, layer=layer, final_norm=final_norm, ff_chunk=1024, pos0=pos0)
    return pl.pallas_call(
        kern,
        grid=(1,),
        in_specs=[_resident((b, n_in)), _resident((b, d_hgrn)), _resident((POOL_BUF, b, d_pool)), _resident((b, d)),
                  _resident((d_pool, d_pool)), _resident((1, d_pool)), _resident((1, d_hgrn)),
                  _resident((d, d)), _resident(g_mlp.shape), _resident((d, d_ff)), _resident((d_ff, d)),
                  _resident((1, d))],
        out_specs=[pl.BlockSpec((b, d), lambda i: (0, 0)), pl.BlockSpec((POOL_BUF, b, d_pool), lambda i: (0, 0, 0))],
        out_shape=[jax.ShapeDtypeStruct((b, d), F32), jax.ShapeDtypeStruct((POOL_BUF, b, d_pool), F32)],
        compiler_params=pltpu.CompilerParams(dimension_semantics=("arbitrary",),
                                             vmem_limit_bytes=VMEM_LIMIT_V7X),
        name="even_sample_tail",
    )(proj, o_raw, pool_rows, x_s, wpool_bd, pscale, gain, w_out, g_mlp, w1, w2, g_final)


def _odd_kernel(x_ref, xs_ref, cst_ref, g_ref, win_ref, cw_ref, mixed_ref, cstate_ref, ms_ref, ncst_ref, zext_ref,
                *, tb):
    i, j = pl.program_id(0), pl.program_id(1)
    last = pl.num_programs(1) - 1
    dc = cw_ref.shape[1]

    def gated_taps(x):
        h = _rms(x, g_ref[...]).astype(BF16)
        cg = jnp.dot(h, win_ref[:, dc:2 * dc], preferred_element_type=F32)
        hv = jnp.dot(h, win_ref[:, 2 * dc:3 * dc], preferred_element_type=F32)
        z = cg * hv
        bg = jnp.dot(h, win_ref[:, 0:dc], preferred_element_type=F32)
        return bg, z, cw_ref[CONV_WIDTH - 1:CONV_WIDTH, :] * z

    @pl.when((i == 0) & (j == 0))
    def _():
        zext_ref[0:CONV_HIST, :] = jnp.zeros((CONV_HIST, dc), F32)

    bg, z, conv = gated_taps(x_ref[...])
    zext_ref[CONV_HIST:CONV_HIST + tb, :] = z
    zext_ref[0:CONV_HIST, :] = jnp.where(j == 0, 0.0, zext_ref[0:CONV_HIST, :])
    for s in range(1, CONV_WIDTH):
        conv = conv + cw_ref[CONV_WIDTH - 1 - s:CONV_WIDTH - s, :] * zext_ref[pl.ds(CONV_HIST - s, tb), :]
    mixed_ref[...] = (bg * conv).astype(BF16)
    cstate_ref[...] = zext_ref[pl.ds(CONV_HIST + tb - CONV_BUF, CONV_BUF), :]
    zext_ref[0:CONV_HIST, :] = zext_ref[tb:tb + CONV_HIST, :]

    @pl.when((i == pl.num_programs(0) - 1) & (j == last))
    def _():
        bg_s, z_s, conv_s = gated_taps(xs_ref[...])
        for s in range(1, CONV_WIDTH):
            r0 = (CONV_BUF - s) * dc
            conv_s = conv_s + cw_ref[CONV_WIDTH - 1 - s:CONV_WIDTH - s, :] * cst_ref[:, r0:r0 + dc]
        ms_ref[...] = (bg_s * conv_s).astype(BF16)
        ncst_ref[:, 0:(CONV_BUF - 1) * dc] = cst_ref[:, dc:CONV_BUF * dc]
        ncst_ref[:, (CONV_BUF - 1) * dc:CONV_BUF * dc] = z_s


def odd_mix(x, x_s, cstate_s, g, w_in, conv_w, *, tb):
    b, t, d = x.shape
    ms = x_s.shape[0]
    dc = conv_w.shape[1]
    kern = functools.partial(_odd_kernel, tb=tb)
    return pl.pallas_call(
        kern,
        grid=(b, t // tb),
        in_specs=[pl.BlockSpec((None, tb, d), lambda i, j: (i, j, 0)),
                  _resident((ms, d)), _resident((ms, CONV_BUF * dc)),
                  _resident((1, d)), _resident((d, 3 * dc)), _resident((CONV_WIDTH, dc))],
        out_specs=[pl.BlockSpec((None, tb, dc), lambda i, j: (i, j, 0)),
                   pl.BlockSpec((None, CONV_BUF, dc), lambda i, j: (i, 0, 0)),
                   pl.BlockSpec((ms, dc), lambda i, j: (0, 0)),
                   pl.BlockSpec((ms, CONV_BUF * dc), lambda i, j: (0, 0))],
        out_shape=[jax.ShapeDtypeStruct((b, t, dc), BF16),
                   jax.ShapeDtypeStruct((b, CONV_BUF, dc), F32),
                   jax.ShapeDtypeStruct((ms, dc), BF16),
                   jax.ShapeDtypeStruct((ms, CONV_BUF * dc), F32)],
        scratch_shapes=[pltpu.VMEM((CONV_HIST + tb, dc), F32)],
        compiler_params=pltpu.CompilerParams(dimension_semantics=("arbitrary", "arbitrary"),
                                             vmem_limit_bytes=VMEM_LIMIT_V7X),
        name="odd_mix",
    )(x, x_s, cstate_s, g, w_in, conv_w)


def _block_diag(w):
    g, c, _ = w.shape
    rows = [jnp.pad(w[i], ((0, 0), (i * c, (g - 1 - i) * c))) for i in range(g)]
    return jnp.concatenate(rows, axis=0)


def kernel(x_prompt, x_sample, state_pool, state_hgrn, state_conv, norm_mix, norm_mlp, norm_final, even_w_in, pool_w, pool_scale, hgrn_lb_logits, hgrn_gain, even_w_out, odd_w_in, conv_w, odd_w_out, ff_w1, ff_w2):
    depth = norm_mix.shape[0]
    b, t, d = x_prompt.shape
    db, ds, _ = x_sample.shape
    assert ds == 1, "the sample group carries one token per sequence"
    xp = x_prompt.reshape(b * t, d)
    xs = x_sample.reshape(db, d)
    pool_p, hgrn_p, conv_p, pool_s, hgrn_s, conv_s = [], [], [], [], [], []

    def layer_weights(l):
        w_in, w_out = (even_w_in, even_w_out) if l % 2 == 0 else (odd_w_in, odd_w_out)
        return dict(w_in=(w_in[l // 2], None), w_out=(w_out[l // 2], None), w1=(ff_w1, l), w2=(ff_w2, l))

    def cast_now(w, layer):
        return (w if layer is None else w[layer]).astype(BF16)

    have = {}
    for l in range(depth):
        g_mix = norm_mix[l][None]
        wl = layer_weights(l)
        missing = [k for k in ("w_out", "w1", "w2") if k not in have]
        if l % 2 == 0:
            e = l // 2
            pool_bd = _block_diag(pool_w[e]).astype(BF16)
            shared = (pool_bd, pool_scale[e][None], hgrn_lb_logits, hgrn_gain[e][None])
            proj_s, have["w_in"] = norm_proj(xs, g_mix, have.get("w_in", wl["w_in"][0]), col_blocks=2)
            mixed_p, p_new, s_new, *cast = even_mix_prompt(xp.reshape(b, t, d), g_mix, have["w_in"], *shared,
                                                           [wl[k] for k in missing], layer=l, tb=256)
            have.update(zip(missing, cast))
            pool_p.append(p_new)
            hgrn_p.append(s_new)
            nxt = layer_weights(l + 1) if l + 1 < depth else {}
            xp, s_new, o_raw, *cast = out_mlp(
                mixed_p.reshape(b * t, d), xp, None, have["w_out"], norm_mlp, have["w1"], have["w2"],
                norm_final[None], list(nxt.values()),
                (proj_s, state_hgrn[e], hgrn_lb_logits, pool_bd.shape[0], l),
                layer=l, final_norm=l == depth - 1, bm=512)
            hgrn_s.append(s_new)
            xs, p_new = even_sample_tail(proj_s, o_raw.reshape(db, -1), jnp.swapaxes(state_pool[e], 0, 1), xs,
                                         pool_bd, pool_scale[e][None], hgrn_gain[e][None], have["w_out"], norm_mlp,
                                         have["w1"], have["w2"], norm_final[None],
                                         layer=l, final_norm=l == depth - 1, pos0=PAST_LEN)
            pool_s.append(jnp.swapaxes(p_new, 0, 1))
            have = dict(zip(nxt, cast))
            continue
        else:
            o = l // 2
            have.update({k: cast_now(*wl[k]) for k in ["w_in"] + missing if k not in have})
            mixed_p, c_new, mixed_s, cs_new = odd_mix(xp.reshape(b, t, d), xs, state_conv[o].reshape(db, -1), g_mix,
                                                      have["w_in"], conv_w[o], tb=512)
            conv_p.append(c_new)
            conv_s.append(cs_new.reshape(db, CONV_BUF, -1))
        nxt = layer_weights(l + 1) if l + 1 < depth else {}
        xp, xs, *cast = out_mlp(mixed_p.reshape(b * t, d), xp, (mixed_s, xs), have["w_out"], norm_mlp, have["w1"],
                                have["w2"], norm_final[None], list(nxt.values()),
                                layer=l, final_norm=l == depth - 1, bm=512)
        have = dict(zip(nxt, cast))
    return (xp.reshape(b, t, d), xs.reshape(db, ds, d), jnp.stack(pool_p), jnp.stack(hgrn_p), jnp.stack(conv_p),
            jnp.stack(pool_s), jnp.stack(hgrn_s), jnp.stack(conv_s))
```

```python
import functools

import jax
import jax.numpy as jnp
from jax import lax
from jax.experimental import pallas as pl
from jax.experimental.pallas import tpu as pltpu

F32 = jnp.float32
BF16 = jnp.bfloat16

EPS = 1e-6
PAST_LEN = 16384
POOL_WINDOWS = (2, 4, 8, 16)
POOL_GROUP_DIM = 64
POOL_BUF = max(POOL_WINDOWS) - 1
POOL_HIST = 16
POOL_PAD = 8
LANES = 128
HGRN_CHUNK = 32
HGRN_BLOCK = 128
HEAD = 128
CONV_WIDTH = 3
CONV_BUF = CONV_WIDTH - 1
CONV_HIST = 8

VMEM_LIMIT_V7X = 56 * 1024 * 1024
MXU_COLS_V7X = 256


def _resident(shape):
    nd = len(shape)
    return pl.BlockSpec(shape, lambda *_: (0,) * nd, pipeline_mode=pl.Buffered(1))


def _layer_block(stacked, layer):
    nd = stacked.ndim
    return pl.BlockSpec((None,) + stacked.shape[1:], lambda *_: (layer,) + (0,) * (nd - 1),
                        pipeline_mode=pl.Buffered(1))


def _cast_specs(weights, n_steps):
    in_specs, out_specs, out_shapes = [], [], []
    for w, layer in weights:
        r, c = w.shape[-2:]
        assert r % (n_steps * 16) == 0, "row slabs must be whole bf16 sublane tiles"
        if layer is None:
            in_specs.append(pl.BlockSpec((r // n_steps, c), lambda s: (s, 0)))
        else:
            in_specs.append(pl.BlockSpec((None, r // n_steps, c), lambda s, layer=layer: (layer, s, 0)))
        out_specs.append(pl.BlockSpec((r // n_steps, c), lambda s: (s, 0)))
        out_shapes.append(jax.ShapeDtypeStruct((r, c), BF16))
    return in_specs, out_specs, out_shapes


def _rms(x, g):
    return x * lax.rsqrt(jnp.mean(x * x, axis=-1, keepdims=True) + EPS) * g


def _sigmoid(x):
    return 1.0 / (1.0 + jnp.exp(-x))


def _lower_bound(lbl, layer):
    e = jnp.exp(lbl - jnp.max(lbl, axis=0, keepdims=True))
    p = e / jnp.sum(e, axis=0, keepdims=True)
    return jnp.sum(p[0:layer + 1], axis=0, keepdims=True)


def _pool_select(snaps, u_shape):
    grp = lax.broadcasted_iota(jnp.int32, u_shape, 1) // POOL_GROUP_DIM
    s = snaps[POOL_WINDOWS[-1]]
    for g in range(len(POOL_WINDOWS) - 2, -1, -1):
        s = jnp.where(grp == g, snaps[POOL_WINDOWS[g]], s)
    win = jnp.left_shift(2, grp)
    return s, win


def _norm_proj_kernel(x_ref, g_ref, w_ref, o_ref, *wb_ref):
    h = _rms(x_ref[...], g_ref[...]).astype(BF16)
    w = w_ref[...].astype(BF16)
    for ref in wb_ref:
        ref[...] = w
    o_ref[...] = jnp.dot(h, w, preferred_element_type=F32)


def norm_proj(x, g, w, *, col_blocks):
    m, d = x.shape
    n = w.shape[1]
    bn = n // col_blocks
    assert n % col_blocks == 0 and bn % LANES == 0
    cast = w.dtype != BF16
    outs = pl.pallas_call(
        _norm_proj_kernel,
        grid=(col_blocks,),
        in_specs=[_resident((m, d)), _resident((1, d)), pl.BlockSpec((d, bn), lambda i: (0, i))],
        out_specs=[pl.BlockSpec((m, bn), lambda i: (0, i))] + [pl.BlockSpec((d, bn), lambda i: (0, i))] * cast,
        out_shape=[jax.ShapeDtypeStruct((m, n), F32)] + [jax.ShapeDtypeStruct((d, n), BF16)] * cast,
        compiler_params=pltpu.CompilerParams(dimension_semantics=("arbitrary",),
                                             vmem_limit_bytes=VMEM_LIMIT_V7X),
        name="norm_proj",
    )(x, g, w)
    return outs if cast else (outs[0], w)


def _mlp_block(mixed, x, wout_ref, g_ref, w1_ref, w2_ref, gf_ref, *, layer, final_norm, ff_chunk, between=None):
    d_ff = w1_ref.shape[1]
    y = x + jnp.dot(mixed, wout_ref[...], preferred_element_type=F32)
    h = _rms(y, g_ref[layer:layer + 1, :]).astype(BF16)
    acc = y
    for c in range(d_ff // ff_chunk):
        a = jnp.dot(h, w1_ref[:, c * ff_chunk:(c + 1) * ff_chunk], preferred_element_type=F32)
        a = jnp.square(jnp.maximum(a, 0.0)).astype(BF16)
        acc = acc + jnp.dot(a, w2_ref[c * ff_chunk:(c + 1) * ff_chunk, :], preferred_element_type=F32)
        if c == 0 and between is not None:
            between()
    return _rms(acc, gf_ref[...]) if final_norm else acc


def _hgrn_sample_step(pj_ref, hst_ref, lbl_ref, nhst_ref, oraw_ref, *, layer, d_pool):
    sb, n_heads = hst_ref.shape[0], hst_ref.shape[1]
    d_hgrn = n_heads * HEAD
    lb = _lower_bound(lbl_ref[...], layer)
    fz = pj_ref[:, d_pool + d_hgrn:d_pool + 2 * d_hgrn]
    iz = pj_ref[:, d_pool + 2 * d_hgrn:d_pool + 3 * d_hgrn]
    v = iz * _sigmoid(iz)
    fkq = (lb + (1.0 - lb) * _sigmoid(fz), (1.0 - lb) * _sigmoid(-fz), pj_ref[:, d_pool:d_pool + d_hgrn])

    eye = (lax.broadcasted_iota(jnp.int32, (HEAD, HEAD), 0) == lax.broadcasted_iota(jnp.int32, (HEAD, HEAD), 1))

    def to_column(row):
        return jnp.sum(jnp.where(eye, row, 0.0), axis=1, keepdims=True)

    for i in range(sb):
        for hd in range(n_heads):
            sl = slice(hd * HEAD, (hd + 1) * HEAD)
            f_col, k_col, q_col = [to_column(x[i:i + 1, sl]) for x in fkq]
            s_new = f_col * hst_ref[i, hd] + k_col * v[i:i + 1, sl]
            nhst_ref[i, hd] = s_new
            oraw_ref[i:i + 1, sl] = jnp.sum(q_col * s_new, axis=0, keepdims=True)


def _out_mlp_kernel(*refs, n_cast, layer, final_norm, ff_chunk, sample_rows, hgrn):
    refs = list(refs)
    m_ref, x_ref = refs[:2]
    del refs[:2]
    if sample_rows:
        ms_ref, xs_ref = refs[:2]
        del refs[:2]
    weights = refs[:5]
    del refs[:5]
    if hgrn is not None:
        pj_ref, hst_ref, lbl_ref = refs[:3]
        del refs[:3]
    cast_in = refs[:n_cast]
    del refs[:n_cast]
    o_ref = refs.pop(0)
    if sample_rows:
        os_ref = refs.pop(0)
    if hgrn is not None:
        nhst_ref, oraw_ref = refs[:2]
        del refs[:2]
    cast_out = refs
    for src_ref, dst_ref in zip(cast_in, cast_out):
        dst_ref[...] = src_ref[...].astype(BF16)

    between = None
    if hgrn is not None:
        between = functools.partial(_hgrn_sample_step, pj_ref, hst_ref, lbl_ref, nhst_ref, oraw_ref, **hgrn)
    o_ref[...] = _mlp_block(m_ref[...], x_ref[...], *weights, layer=layer, final_norm=final_norm,
                            ff_chunk=ff_chunk, between=between)

    if sample_rows:
        @pl.when(pl.program_id(0) == pl.num_programs(0) - 1)
        def _():
            os_ref[...] = _mlp_block(ms_ref[...], xs_ref[...], *weights, layer=layer, final_norm=final_norm,
                                     ff_chunk=ff_chunk)


def out_mlp(mixed, x, sample, w_out, g_mlp, w1, w2, g_final, to_cast, hgrn_sample=None, *, layer, final_norm, bm):
    m, d = x.shape
    d_ff = w1.shape[1]
    steps = m // bm
    cast_in, cast_out, cast_shapes = _cast_specs(to_cast, steps)
    args = [mixed, x]
    in_specs = [pl.BlockSpec((bm, d), lambda i: (i, 0)), pl.BlockSpec((bm, d), lambda i: (i, 0))]
    out_specs = [pl.BlockSpec((bm, d), lambda i: (i, 0))]
    out_shapes = [jax.ShapeDtypeStruct((m, d), F32)]
    if sample is not None:
        ms = sample[1].shape[0]
        args += list(sample)
        in_specs += [_resident((ms, d)), _resident((ms, d))]
        out_specs.append(pl.BlockSpec((ms, d), lambda i: (0, 0)))
        out_shapes.append(jax.ShapeDtypeStruct((ms, d), F32))
    args += [w_out, g_mlp, w1, w2, g_final]
    in_specs += [_resident((d, d)), _resident(g_mlp.shape), _resident((d, d_ff)), _resident((d_ff, d)),
                 _resident((1, d))]
    hgrn = None
    if hgrn_sample is not None:
        proj_s, state, lbl, d_pool, hgrn_layer = hgrn_sample
        nseq, n_in = proj_s.shape
        assert nseq % steps == 0
        sb = nseq // steps
        n_heads = state.shape[1]
        args += [proj_s.reshape(steps, sb, n_in), state, lbl]
        in_specs += [pl.BlockSpec((None, sb, n_in), lambda i: (i, 0, 0)),
                     pl.BlockSpec((sb, n_heads, HEAD, HEAD), lambda i: (i, 0, 0, 0)), _resident(lbl.shape)]
        out_specs += [pl.BlockSpec((sb, n_heads, HEAD, HEAD), lambda i: (i, 0, 0, 0)),
                      pl.BlockSpec((None, sb, n_heads * HEAD), lambda i: (i, 0, 0))]
        out_shapes += [jax.ShapeDtypeStruct(state.shape, F32),
                       jax.ShapeDtypeStruct((steps, sb, n_heads * HEAD), F32)]
        hgrn = dict(layer=hgrn_layer, d_pool=d_pool)
    kern = functools.partial(_out_mlp_kernel, n_cast=len(to_cast), layer=layer, final_norm=final_norm,
                             ff_chunk=1024, sample_rows=sample is not None, hgrn=hgrn)
    return pl.pallas_call(
        kern,
        grid=(steps,),
        in_specs=in_specs + cast_in,
        out_specs=out_specs + cast_out,
        out_shape=out_shapes + cast_shapes,
        compiler_params=pltpu.CompilerParams(dimension_semantics=("arbitrary",),
                                             vmem_limit_bytes=VMEM_LIMIT_V7X),
        name="out_mlp",
    )(*args, *[w for w, _ in to_cast])


def _even_prompt_kernel(*refs, n_cast, layer, tb, tiles_per_seq):
    xn_ref, x0_ref, g_ref, win_ref, wpool_ref, pscale_ref, lbl_ref, gain_ref = refs[:8]
    cast_in, (mixed_ref, pstate_ref, hstate_ref) = refs[8:8 + n_cast], refs[8 + n_cast:11 + n_cast]
    cast_out = refs[11 + n_cast:11 + 2 * n_cast]
    (pu_ref, pq_ref, pf_ref, pi_ref, pg_ref, hn_ref, ext_ref, lvl_ref, st_ref, k_ref, lsplit_ref,
     bcum_ref, dec_ref, gate_ref, v_ref, qd_ref, kd_ref, ke_ref, qb_ref, kb_ref, qr_ref, kc_ref, qt_ref,
     kt_ref) = refs[11 + 2 * n_cast:]
    for src_ref, dst_ref in zip(cast_in, cast_out):
        dst_ref[...] = src_ref[...].astype(BF16)

    step = pl.program_id(0)
    d_pool = wpool_ref.shape[0]
    n_heads = st_ref.shape[0]
    d_hgrn = n_heads * HEAD
    nc = tb // HGRN_CHUNK
    nblk = tb // HGRN_BLOCK
    j = step % tiles_per_seq
    last = tiles_per_seq - 1
    o_q, o_f, o_i, o_g = d_pool, d_pool + d_hgrn, d_pool + 2 * d_hgrn, d_pool + 3 * d_hgrn
    sections = [(pu_ref, 0), (pq_ref, o_q), (pf_ref, o_f), (pi_ref, o_i), (pg_ref, o_g)]

    def chunks(dst_ref, col0):
        def make(c0, c1):
            def run():
                dst_ref[:, c0:c1] = jnp.dot(hn_ref[...], win_ref[:, col0 + c0:col0 + c1],
                                            preferred_element_type=F32)
            return run
        width = dst_ref.shape[1]
        return [make(c0, min(c0 + MXU_COLS_V7X, width)) for c0 in range(0, width, MXU_COLS_V7X)]

    @pl.when(step == 0)
    def _():
        hn_ref[...] = _rms(x0_ref[...], g_ref[...]).astype(BF16)
        for dst_ref, col0 in sections:
            for run in chunks(dst_ref, col0):
                run()

    r1_, r2_ = POOL_PAD + POOL_HIST, POOL_PAD + POOL_HIST + tb

    @pl.when(j == 0)
    def _():
        ext_ref[0:r1_, :] = jnp.zeros((r1_, d_pool), F32)
        lvl_ref[:, 0:POOL_PAD, :] = jnp.zeros((lvl_ref.shape[0], POOL_PAD, LANES), F32)
        st_ref[...] = jnp.zeros(st_ref.shape, F32)

    hn_ref[...] = _rms(xn_ref[...], g_ref[...]).astype(BF16)

    u = pu_ref[...]
    ext_ref[r1_:r2_, :] = u
    groups_per_tile = LANES // POOL_GROUP_DIM
    lane_grp = lax.broadcasted_iota(jnp.int32, (tb, LANES), 1) // POOL_GROUP_DIM
    pos = j * tb + lax.broadcasted_iota(jnp.int32, (tb, LANES), 0)
    means = []
    for lt in range(d_pool // LANES):
        wins = POOL_WINDOWS[lt * groups_per_tile:(lt + 1) * groups_per_tile]
        src, src_lanes, w, nbuf, got = ext_ref, slice(lt * LANES, (lt + 1) * LANES), 1, 0, {}
        while w < wins[-1]:
            new = src[POOL_PAD:r2_, src_lanes] + src[POOL_PAD - w:r2_ - w, src_lanes]
            w *= 2
            if w in wins:
                got[w] = new[POOL_HIST:]
            if w < wins[-1]:
                lvl_ref[nbuf, POOL_PAD:r2_, :] = new
                src, src_lanes, nbuf = lvl_ref.at[nbuf], slice(0, LANES), 1 - nbuf
        ssum, win = got[wins[-1]], jnp.full((tb, LANES), wins[-1], jnp.int32)
        for g_ in range(groups_per_tile - 2, -1, -1):
            ssum = jnp.where(lane_grp == g_, got[wins[g_]], ssum)
            win = jnp.where(lane_grp == g_, wins[g_], win)
        cnt = jnp.minimum(win, pos + 1).astype(F32)
        means.append(ssum / cnt)
    diff = (jnp.concatenate(means, axis=1) - u).astype(BF16)
    pool_out = jnp.dot(diff, wpool_ref[...], preferred_element_type=F32) * pscale_ref[...]
    mixed_ref[:, 0:d_pool] = pool_out.astype(BF16)

    @pl.when(j == last)
    def _():
        pstate_ref[...] = ext_ref[pl.ds(r2_ - POOL_BUF, POOL_BUF), :]

    ext_ref[POOL_PAD:r1_, :] = ext_ref[POOL_PAD + tb:r1_ + tb, :]
    for run in chunks(pu_ref, 0):
        run()

    lb = _lower_bound(lbl_ref[...], layer)
    for c in range(nc):
        rows = slice(c * HGRN_CHUNK, (c + 1) * HGRN_CHUNK)
        fz = pf_ref[rows, :]
        a = jnp.exp(-jnp.abs(fz))
        r = 1.0 / (1.0 + a)
        ar = a * r
        sig = jnp.where(fz >= 0, r, ar)
        sig_neg = jnp.where(fz >= 0, ar, r)
        logf = jnp.log(lb + (1.0 - lb) * sig)
        k_ref[rows, :] = (1.0 - lb) * sig_neg
        l_hi = logf.astype(BF16)
        r1 = logf - l_hi.astype(F32)
        l_mid = r1.astype(BF16)
        lsplit_ref[rows, 0:d_hgrn] = l_hi
        lsplit_ref[rows, d_hgrn:2 * d_hgrn] = l_mid
        lsplit_ref[rows, 2 * d_hgrn:3 * d_hgrn] = (r1 - l_mid.astype(F32)).astype(BF16)
        iz = pi_ref[rows, :]
        v_ref[rows, :] = (iz * _sigmoid(iz)).astype(BF16)
        gate_ref[rows, :] = gain_ref[...] * _sigmoid(pg_ref[rows, :])

    ri = lax.broadcasted_iota(jnp.int32, (tb, tb), 0)
    ci = lax.broadcasted_iota(jnp.int32, (tb, tb), 1)
    tri = ((ri // HGRN_CHUNK == ci // HGRN_CHUNK) & (ci <= ri)).astype(BF16)
    cs = jnp.dot(tri, lsplit_ref[...], preferred_element_type=F32)
    bcum_ref[...] = cs[:, 0:d_hgrn] + cs[:, d_hgrn:2 * d_hgrn] + cs[:, 2 * d_hgrn:3 * d_hgrn]
    for run in chunks(pf_ref, o_f) + chunks(pi_ref, o_i):
        run()

    nsub = HGRN_BLOCK // HGRN_CHUNK
    half = nsub // 2
    decays = []
    for b in range(nblk):
        blast = [bcum_ref[(b * nsub + i + 1) * HGRN_CHUNK - 1:(b * nsub + i + 1) * HGRN_CHUNK, :]
                 for i in range(nsub)]
        pre = [jnp.zeros_like(blast[0])]
        for i in range(nsub):
            pre.append(pre[-1] + blast[i])
        decays.append((blast, pre))
    zero = jnp.zeros_like(decays[0][1][0])
    before = [sum((decays[c][1][nsub] for c in range(b)), zero) for b in range(nblk)]
    after = [sum((decays[c][1][nsub] for c in range(b + 1, nblk)), zero) for b in range(nblk)]
    dec_ref[0:1, :] = jnp.exp(before[nblk - 1] + decays[nblk - 1][1][nsub])
    for b in range(nblk):
        blast, pre = decays[b]
        for i in range(nsub):
            c = b * nsub + i
            rows = slice(c * HGRN_CHUNK, (c + 1) * HGRN_CHUNK)
            bc = bcum_ref[rows, :]
            qd = pq_ref[rows, :] * jnp.exp(bc)
            kd = k_ref[rows, :] * jnp.exp(-bc)
            ke = kd * jnp.exp(blast[i])
            qd_ref[rows, :] = qd.astype(BF16)
            kd_ref[rows, :] = kd.astype(BF16)
            ke_ref[rows, :] = ke.astype(BF16)
            qb_ref[rows, :] = (qd * jnp.exp(pre[i])).astype(BF16)
            kb_ref[rows, :] = (ke * jnp.exp(pre[nsub] - pre[i + 1])).astype(BF16)
            qt_ref[rows, :] = (qd * jnp.exp(pre[i] + before[b])).astype(BF16)
            kt_ref[rows, :] = (ke * jnp.exp(pre[nsub] - pre[i + 1] + after[b])).astype(BF16)
            if i >= half:
                qr_ref[rows, :] = (qd * jnp.exp(pre[i] - pre[half])).astype(BF16)
                kc_ref[rows, :] = jnp.zeros((HGRN_CHUNK, d_hgrn), BF16)
            else:
                kc_ref[rows, :] = (ke * jnp.exp(pre[half] - pre[i + 1])).astype(BF16)

    rb = lax.broadcasted_iota(jnp.int32, (HGRN_BLOCK, HGRN_BLOCK), 0)
    cb = lax.broadcasted_iota(jnp.int32, (HGRN_BLOCK, HGRN_BLOCK), 1)
    rsub, csub = rb // HGRN_CHUNK, cb // HGRN_CHUNK
    m_diag = (rsub == csub) & (cb <= rb)
    m_adj = (csub == rsub - 1) & (rsub != half)
    eye = rb == cb
    hb = HGRN_BLOCK // 2
    nt_dims = (((1,), (1,)), ((), ()))
    tn_dims = (((0,), (0,)), ((), ()))
    fillers = chunks(pq_ref, o_q) + chunks(pg_ref, o_g)

    def fill(n):
        for _ in range(min(n, len(fillers))):
            fillers.pop(0)()

    assert nblk <= 2, "cross-block scores are built between adjacent blocks only"
    per_stage = -(-len(fillers) // (nblk + 2))
    blk_rows = [slice(b * HGRN_BLOCK, (b + 1) * HGRN_BLOCK) for b in range(nblk)]
    g12s, g3s, cross = {}, {}, {}
    for b in range(nblk):
        rows = blk_rows[b]
        far = slice(b * HGRN_BLOCK + hb, (b + 1) * HGRN_BLOCK)
        for hd in range(n_heads):
            sl = slice(hd * HEAD, (hd + 1) * HEAD)
            kk = jnp.concatenate([kd_ref[rows, sl], ke_ref[rows, sl]], axis=0)
            g12s[b, hd] = lax.dot_general(qd_ref[rows, sl], kk, nt_dims, preferred_element_type=F32)
            g3s[b, hd] = lax.dot_general(qr_ref[far, sl], kc_ref[rows, sl], nt_dims, preferred_element_type=F32)
            if b > 0:
                cross[b, hd] = lax.dot_general(qb_ref[rows, sl], kb_ref[blk_rows[b - 1], sl], nt_dims,
                                               preferred_element_type=F32)
        fill(per_stage)
    outs = {}
    for hd in range(n_heads):
        sl = slice(hd * HEAD, (hd + 1) * HEAD)
        st = st_ref[hd]
        st_bf = st.astype(BF16)
        for b in range(nblk):
            rows = blk_rows[b]
            g12 = g12s[b, hd]
            scores = jnp.where(m_diag, g12[:, 0:HGRN_BLOCK], jnp.where(m_adj, g12[:, HGRN_BLOCK:], 0.0))
            scores = jnp.concatenate([scores[0:hb], scores[hb:] + g3s[b, hd]], axis=0).astype(BF16)
            lhs, rhs = [scores], [v_ref[rows, sl]]
            if b > 0:
                lhs.append(cross[b, hd].astype(BF16))
                rhs.append(v_ref[blk_rows[b - 1], sl])
            lhs.append(qt_ref[rows, sl])
            rhs.append(st_bf)
            outs[b, hd] = jnp.dot(jnp.concatenate(lhs, axis=1), jnp.concatenate(rhs, axis=0),
                                  preferred_element_type=F32)
        d_col = jnp.sum(jnp.where(eye, dec_ref[0:1, sl], 0.0), axis=1, keepdims=True)
        st_ref[hd] = st * d_col + lax.dot_general(kt_ref[:, sl], v_ref[:, sl], tn_dims, preferred_element_type=F32)
    fill(per_stage)
    for b in range(nblk):
        rows = blk_rows[b]
        for hd in range(n_heads):
            sl = slice(hd * HEAD, (hd + 1) * HEAD)
            o = outs[b, hd]
            o = o * lax.rsqrt(jnp.mean(o * o, axis=-1, keepdims=True) + EPS)
            mixed_ref[rows, d_pool + hd * HEAD:d_pool + (hd + 1) * HEAD] = (o * gate_ref[rows, sl]).astype(BF16)
        fill(per_stage)
    fill(len(fillers))

    @pl.when(j == last)
    def _():
        hstate_ref[...] = st_ref[...]


def even_mix_prompt(x, g, w_in, wpool_bd, pscale, lbl, gain, to_cast, *, layer, tb):
    b, t, d = x.shape
    n_in = w_in.shape[1]
    d_pool = wpool_bd.shape[0]
    d_hgrn = gain.shape[1]
    n_heads = d_hgrn // HEAD
    tps = t // tb
    n_tiles = b * tps
    cast_in, cast_out, cast_shapes = _cast_specs(to_cast, n_tiles)
    kern = functools.partial(_even_prompt_kernel, n_cast=len(to_cast), layer=layer, tb=tb, tiles_per_seq=tps)

    def next_tile(s):
        tile = jnp.minimum(s + 1, n_tiles - 1)
        return tile // tps, tile % tps

    assert d_pool % LANES == 0 and LANES % POOL_GROUP_DIM == 0 and tb % HGRN_BLOCK == 0
    return pl.pallas_call(
        kern,
        grid=(n_tiles,),
        in_specs=[pl.BlockSpec((None, tb, d), lambda s: (*next_tile(s), 0)),
                  pl.BlockSpec((None, tb, d), lambda s: (0, 0, 0)),
                  _resident((1, d)), _resident((d, n_in)), _resident((d_pool, d_pool)),
                  _resident((1, d_pool)), _resident(lbl.shape), _resident((1, d_hgrn))] + cast_in,
        out_specs=[pl.BlockSpec((None, tb, d), lambda s: (s // tps, s % tps, 0)),
                   pl.BlockSpec((None, POOL_BUF, d_pool), lambda s: (s // tps, 0, 0)),
                   pl.BlockSpec((None, n_heads, HEAD, HEAD), lambda s: (s // tps, 0, 0, 0))] + cast_out,
        out_shape=[jax.ShapeDtypeStruct((b, t, d), BF16),
                   jax.ShapeDtypeStruct((b, POOL_BUF, d_pool), F32),
                   jax.ShapeDtypeStruct((b, n_heads, HEAD, HEAD), F32)] + cast_shapes,
        scratch_shapes=[pltpu.VMEM((tb, d_pool), F32)]
                       + [pltpu.VMEM((tb, d_hgrn), F32)] * 4
                       + [pltpu.VMEM((tb, d), BF16),
                        pltpu.VMEM((POOL_PAD + POOL_HIST + tb, d_pool), F32),
                        pltpu.VMEM((2, POOL_PAD + POOL_HIST + tb, LANES), F32),
                        pltpu.VMEM((n_heads, HEAD, HEAD), F32),
                        pltpu.VMEM((tb, d_hgrn), F32),
                        pltpu.VMEM((tb, 3 * d_hgrn), BF16),
                        pltpu.VMEM((tb, d_hgrn), F32),
                        pltpu.VMEM((tb // HGRN_BLOCK, d_hgrn), F32),
                        pltpu.VMEM((tb, d_hgrn), F32)]
                       + [pltpu.VMEM((tb, d_hgrn), BF16)] * 10,
        compiler_params=pltpu.CompilerParams(dimension_semantics=("arbitrary",),
                                             vmem_limit_bytes=VMEM_LIMIT_V7X),
        name="even_mix_prompt",
    )(x, x, g, w_in, wpool_bd, pscale, lbl, gain, *[w for w, _ in to_cast])


def _even_sample_tail_kernel(proj_ref, oraw_ref, pool_ref, x_ref, wpool_ref, pscale_ref, gain_ref, wout_ref, g_ref,
                             w1_ref, w2_ref, gf_ref, o_ref, npool_ref, *, layer, final_norm, ff_chunk, pos0):
    d_pool = wpool_ref.shape[0]
    d_hgrn = gain_ref.shape[1]

    u = proj_ref[:, 0:d_pool]
    acc = u
    snaps = {}
    for s in range(1, POOL_WINDOWS[-1]):
        acc = acc + pool_ref[POOL_BUF - s]
        if s + 1 in POOL_WINDOWS:
            snaps[s + 1] = acc
    ssum, win = _pool_select(snaps, u.shape)
    cnt = jnp.minimum(win, pos0 + 1).astype(F32)
    diff = (ssum / cnt - u).astype(BF16)
    pool_out = jnp.dot(diff, wpool_ref[...], preferred_element_type=F32) * pscale_ref[...]
    npool_ref[0:POOL_BUF - 1] = pool_ref[1:POOL_BUF]
    npool_ref[POOL_BUF - 1] = u

    gate = gain_ref[...] * _sigmoid(proj_ref[:, d_pool + 3 * d_hgrn:d_pool + 4 * d_hgrn])
    mixed = [pool_out.astype(BF16)]
    for hd in range(d_hgrn // HEAD):
        sl = slice(hd * HEAD, (hd + 1) * HEAD)
        o = oraw_ref[:, sl]
        o = o * lax.rsqrt(jnp.mean(o * o, axis=-1, keepdims=True) + EPS)
        mixed.append((o * gate[:, sl]).astype(BF16))
    o_ref[...] = _mlp_block(jnp.concatenate(mixed, axis=1), x_ref[...], wout_ref, g_ref, w1_ref, w2_ref, gf_ref,
                            layer=layer, final_norm=final_norm, ff_chunk=ff_chunk)


def even_sample_tail(proj, o_raw, pool_rows, x_s, wpool_bd, pscale, gain, w_out, g_mlp, w1, w2, g_final,
                     *, layer, final_norm, pos0):
    b, n_in = proj.shape
    d = x_s.shape[1]
    d_pool = wpool_bd.shape[0]
    d_hgrn = gain.shape[1]
    d_ff = w1.shape[1]
    kern = functools.partial(_even_sample_tail_kernel, layer=layer, final_norm=final_norm, ff_chunk=1024, pos0=pos0)
    return pl.pallas_call(
        kern,
        grid=(1,),
        in_specs=[_resident((b, n_in)), _resident((b, d_hgrn)), _resident((POOL_BUF, b, d_pool)), _resident((b, d)),
                  _resident((d_pool, d_pool)), _resident((1, d_pool)), _resident((1, d_hgrn)),
                  _resident((d, d)), _resident(g_mlp.shape), _resident((d, d_ff)), _resident((d_ff, d)),
                  _resident((1, d))],
        out_specs=[pl.BlockSpec((b, d), lambda i: (0, 0)), pl.BlockSpec((POOL_BUF, b, d_pool), lambda i: (0, 0, 0))],
        out_shape=[jax.ShapeDtypeStruct((b, d), F32), jax.ShapeDtypeStruct((POOL_BUF, b, d_pool), F32)],
        compiler_params=pltpu.CompilerParams(dimension_semantics=("arbitrary",),
                                             vmem_limit_bytes=VMEM_LIMIT_V7X),
        name="even_sample_tail",
    )(proj, o_raw, pool_rows, x_s, wpool_bd, pscale, gain, w_out, g_mlp, w1, w2, g_final)


def _odd_kernel(x_ref, xs_ref, cst_ref, g_ref, win_ref, cw_ref, mixed_ref, cstate_ref, ms_ref, ncst_ref, zext_ref,
                *, tb):
    i, j = pl.program_id(0), pl.program_id(1)
    last = pl.num_programs(1) - 1
    dc = cw_ref.shape[1]

    def gated_taps(x):
        h = _rms(x, g_ref[...]).astype(BF16)
        cg = jnp.dot(h, win_ref[:, dc:2 * dc], preferred_element_type=F32)
        hv = jnp.dot(h, win_ref[:, 2 * dc:3 * dc], preferred_element_type=F32)
        z = cg * hv
        bg = jnp.dot(h, win_ref[:, 0:dc], preferred_element_type=F32)
        return bg, z, cw_ref[CONV_WIDTH - 1:CONV_WIDTH, :] * z

    @pl.when(j == 0)
    def _():
        zext_ref[0:CONV_HIST, :] = jnp.zeros((CONV_HIST, dc), F32)

    bg, z, conv = gated_taps(x_ref[...])
    zext_ref[CONV_HIST:CONV_HIST + tb, :] = z
    for s in range(1, CONV_WIDTH):
        conv = conv + cw_ref[CONV_WIDTH - 1 - s:CONV_WIDTH - s, :] * zext_ref[pl.ds(CONV_HIST - s, tb), :]
    mixed_ref[...] = (bg * conv).astype(BF16)

    @pl.when(j == last)
    def _():
        cstate_ref[...] = zext_ref[pl.ds(CONV_HIST + tb - CONV_BUF, CONV_BUF), :]

    zext_ref[0:CONV_HIST, :] = zext_ref[tb:tb + CONV_HIST, :]

    @pl.when((i == pl.num_programs(0) - 1) & (j == last))
    def _():
        bg_s, z_s, conv_s = gated_taps(xs_ref[...])
        for s in range(1, CONV_WIDTH):
            r0 = (CONV_BUF - s) * dc
            conv_s = conv_s + cw_ref[CONV_WIDTH - 1 - s:CONV_WIDTH - s, :] * cst_ref[:, r0:r0 + dc]
        ms_ref[...] = (bg_s * conv_s).astype(BF16)
        ncst_ref[:, 0:(CONV_BUF - 1) * dc] = cst_ref[:, dc:CONV_BUF * dc]
        ncst_ref[:, (CONV_BUF - 1) * dc:CONV_BUF * dc] = z_s


def odd_mix(x, x_s, cstate_s, g, w_in, conv_w, *, tb):
    b, t, d = x.shape
    ms = x_s.shape[0]
    dc = conv_w.shape[1]
    kern = functools.partial(_odd_kernel, tb=tb)
    return pl.pallas_call(
        kern,
        grid=(b, t // tb),
        in_specs=[pl.BlockSpec((None, tb, d), lambda i, j: (i, j, 0)),
                  _resident((ms, d)), _resident((ms, CONV_BUF * dc)),
                  _resident((1, d)), _resident((d, 3 * dc)), _resident((CONV_WIDTH, dc))],
        out_specs=[pl.BlockSpec((None, tb, dc), lambda i, j: (i, j, 0)),
                   pl.BlockSpec((None, CONV_BUF, dc), lambda i, j: (i, 0, 0)),
                   pl.BlockSpec((ms, dc), lambda i, j: (0, 0)),
                   pl.BlockSpec((ms, CONV_BUF * dc), lambda i, j: (0, 0))],
        out_shape=[jax.ShapeDtypeStruct((b, t, dc), BF16),
                   jax.ShapeDtypeStruct((b, CONV_BUF, dc), F32),
                   jax.ShapeDtypeStruct((ms, dc), BF16),
                   jax.ShapeDtypeStruct((ms, CONV_BUF * dc), F32)],
        scratch_shapes=[pltpu.VMEM((CONV_HIST + tb, dc), F32)],
        compiler_params=pltpu.CompilerParams(dimension_semantics=("arbitrary", "arbitrary"),
                                             vmem_limit_bytes=VMEM_LIMIT_V7X),
        name="odd_mix",
    )(x, x_s, cstate_s, g, w_in, conv_w)


def _block_diag(w):
    g, c, _ = w.shape
    rows = [jnp.pad(w[i], ((0, 0), (i * c, (g - 1 - i) * c))) for i in range(g)]
    return jnp.concatenate(rows, axis=0)


def kernel(x_prompt, x_sample, state_pool, state_hgrn, state_conv, norm_mix, norm_mlp, norm_final, even_w_in, pool_w, pool_scale, hgrn_lb_logits, hgrn_gain, even_w_out, odd_w_in, conv_w, odd_w_out, ff_w1, ff_w2):
    depth = norm_mix.shape[0]
    b, t, d = x_prompt.shape
    db, ds, _ = x_sample.shape
    assert ds == 1, "the sample group carries one token per sequence"
    xp = x_prompt.reshape(b * t, d)
    xs = x_sample.reshape(db, d)
    pool_p, hgrn_p, conv_p, pool_s, hgrn_s, conv_s = [], [], [], [], [], []

    def layer_weights(l):
        w_in, w_out = (even_w_in, even_w_out) if l % 2 == 0 else (odd_w_in, odd_w_out)
        return dict(w_in=(w_in[l // 2], None), w_out=(w_out[l // 2], None), w1=(ff_w1, l), w2=(ff_w2, l))

    def cast_now(w, layer):
        return (w if layer is None else w[layer]).astype(BF16)

    have = {}
    for l in range(depth):
        g_mix = norm_mix[l][None]
        wl = layer_weights(l)
        missing = [k for k in ("w_out", "w1", "w2") if k not in have]
        if l % 2 == 0:
            e = l // 2
            pool_bd = _block_diag(pool_w[e]).astype(BF16)
            shared = (pool_bd, pool_scale[e][None], hgrn_lb_logits, hgrn_gain[e][None])
            proj_s, have["w_in"] = norm_proj(xs, g_mix, have.get("w_in", wl["w_in"][0]), col_blocks=2)
            mixed_p, p_new, s_new, *cast = even_mix_prompt(xp.reshape(b, t, d), g_mix, have["w_in"], *shared,
                                                           [wl[k] for k in missing], layer=l, tb=256)
            have.update(zip(missing, cast))
            pool_p.append(p_new)
            hgrn_p.append(s_new)
            nxt = layer_weights(l + 1) if l + 1 < depth else {}
            xp, s_new, o_raw, *cast = out_mlp(
                mixed_p.reshape(b * t, d), xp, None, have["w_out"], norm_mlp, have["w1"], have["w2"],
                norm_final[None], list(nxt.values()),
                (proj_s, state_hgrn[e], hgrn_lb_logits, pool_bd.shape[0], l),
                layer=l, final_norm=l == depth - 1, bm=512)
            hgrn_s.append(s_new)
            xs, p_new = even_sample_tail(proj_s, o_raw.reshape(db, -1), jnp.swapaxes(state_pool[e], 0, 1), xs,
                                         pool_bd, pool_scale[e][None], hgrn_gain[e][None], have["w_out"], norm_mlp,
                                         have["w1"], have["w2"], norm_final[None],
                                         layer=l, final_norm=l == depth - 1, pos0=PAST_LEN)
            pool_s.append(jnp.swapaxes(p_new, 0, 1))
            have = dict(zip(nxt, cast))
            continue
        else:
            o = l // 2
            have.update({k: cast_now(*wl[k]) for k in ["w_in"] + missing if k not in have})
            mixed_p, c_new, mixed_s, cs_new = odd_mix(xp.reshape(b, t, d), xs, state_conv[o].reshape(db, -1), g_mix,
                                                      have["w_in"], conv_w[o], tb=512)
            conv_p.append(c_new)
            conv_s.append(cs_new.reshape(db, CONV_BUF, -1))
        nxt = layer_weights(l + 1) if l + 1 < depth else {}
        xp, xs, *cast = out_mlp(mixed_p.reshape(b * t, d), xp, (mixed_s, xs), have["w_out"], norm_mlp, have["w1"],
                                have["w2"], norm_final[None], list(nxt.values()),
                                layer=l, final_norm=l == depth - 1, bm=512)
        have = dict(zip(nxt, cast))
    return (xp.reshape(b, t, d), xs.reshape(db, ds, d), jnp.stack(pool_p), jnp.stack(hgrn_p), jnp.stack(conv_p),
            jnp.stack(pool_s), jnp.stack(hgrn_s), jnp.stack(conv_s))
```

```python
import functools

import jax
import jax.numpy as jnp
from jax import lax
from jax.experimental import pallas as pl
from jax.experimental.pallas import tpu as pltpu

F32 = jnp.float32
BF16 = jnp.bfloat16

EPS = 1e-6
PAST_LEN = 16384
POOL_WINDOWS = (2, 4, 8, 16)
POOL_GROUP_DIM = 64
POOL_BUF = max(POOL_WINDOWS) - 1
POOL_HIST = 16
POOL_PAD = 8
LANES = 128
HGRN_CHUNK = 32
HGRN_BLOCK = 128
HEAD = 128
CONV_WIDTH = 3
CONV_BUF = CONV_WIDTH - 1
CONV_HIST = 8

VMEM_LIMIT_V7X = 56 * 1024 * 1024
MXU_COLS_V7X = 256


def _resident(shape):
    nd = len(shape)
    return pl.BlockSpec(shape, lambda *_: (0,) * nd, pipeline_mode=pl.Buffered(1))


def _layer_block(stacked, layer):
    nd = stacked.ndim
    return pl.BlockSpec((None,) + stacked.shape[1:], lambda *_: (layer,) + (0,) * (nd - 1),
                        pipeline_mode=pl.Buffered(1))


def _cast_specs(weights, n_steps):
    in_specs, out_specs, out_shapes = [], [], []
    for w, layer in weights:
        r, c = w.shape[-2:]
        assert r % (n_steps * 16) == 0, "row slabs must be whole bf16 sublane tiles"
        if layer is None:
            in_specs.append(pl.BlockSpec((r // n_steps, c), lambda s: (s, 0)))
        else:
            in_specs.append(pl.BlockSpec((None, r // n_steps, c), lambda s, layer=layer: (layer, s, 0)))
        out_specs.append(pl.BlockSpec((r // n_steps, c), lambda s: (s, 0)))
        out_shapes.append(jax.ShapeDtypeStruct((r, c), BF16))
    return in_specs, out_specs, out_shapes


def _rms(x, g):
    return x * lax.rsqrt(jnp.mean(x * x, axis=-1, keepdims=True) + EPS) * g


def _sigmoid(x):
    return 1.0 / (1.0 + jnp.exp(-x))


def _lower_bound(lbl, layer):
    e = jnp.exp(lbl - jnp.max(lbl, axis=0, keepdims=True))
    p = e / jnp.sum(e, axis=0, keepdims=True)
    return jnp.sum(p[0:layer + 1], axis=0, keepdims=True)


def _pool_select(snaps, u_shape):
    grp = lax.broadcasted_iota(jnp.int32, u_shape, 1) // POOL_GROUP_DIM
    s = snaps[POOL_WINDOWS[-1]]
    for g in range(len(POOL_WINDOWS) - 2, -1, -1):
        s = jnp.where(grp == g, snaps[POOL_WINDOWS[g]], s)
    win = jnp.left_shift(2, grp)
    return s, win


def _norm_proj_kernel(x_ref, g_ref, w_ref, o_ref, *wb_ref):
    h = _rms(x_ref[...], g_ref[...]).astype(BF16)
    w = w_ref[...].astype(BF16)
    for ref in wb_ref:
        ref[...] = w
    o_ref[...] = jnp.dot(h, w, preferred_element_type=F32)


def norm_proj(x, g, w, *, col_blocks):
    m, d = x.shape
    n = w.shape[1]
    bn = n // col_blocks
    assert n % col_blocks == 0 and bn % LANES == 0
    cast = w.dtype != BF16
    outs = pl.pallas_call(
        _norm_proj_kernel,
        grid=(col_blocks,),
        in_specs=[_resident((m, d)), _resident((1, d)), pl.BlockSpec((d, bn), lambda i: (0, i))],
        out_specs=[pl.BlockSpec((m, bn), lambda i: (0, i))] + [pl.BlockSpec((d, bn), lambda i: (0, i))] * cast,
        out_shape=[jax.ShapeDtypeStruct((m, n), F32)] + [jax.ShapeDtypeStruct((d, n), BF16)] * cast,
        compiler_params=pltpu.CompilerParams(dimension_semantics=("arbitrary",),
                                             vmem_limit_bytes=VMEM_LIMIT_V7X),
        name="norm_proj",
    )(x, g, w)
    return outs if cast else (outs[0], w)


def _mlp_block(mixed, x, wout_ref, g_ref, w1_ref, w2_ref, gf_ref, *, layer, final_norm, ff_chunk, between=None):
    d_ff = w1_ref.shape[1]
    y = x + jnp.dot(mixed, wout_ref[...], preferred_element_type=F32)
    h = _rms(y, g_ref[layer:layer + 1, :]).astype(BF16)
    acc = y
    for c in range(d_ff // ff_chunk):
        a = jnp.dot(h, w1_ref[:, c * ff_chunk:(c + 1) * ff_chunk], preferred_element_type=F32)
        a = jnp.square(jnp.maximum(a, 0.0)).astype(BF16)
        acc = acc + jnp.dot(a, w2_ref[c * ff_chunk:(c + 1) * ff_chunk, :], preferred_element_type=F32)
        if c == 0 and between is not None:
            between()
    return _rms(acc, gf_ref[...]) if final_norm else acc


def _hgrn_sample_step(pj_ref, hst_ref, lbl_ref, nhst_ref, oraw_ref, *, layer, d_pool):
    sb, n_heads = hst_ref.shape[0], hst_ref.shape[1]
    d_hgrn = n_heads * HEAD
    lb = _lower_bound(lbl_ref[...], layer)
    fz = pj_ref[:, d_pool + d_hgrn:d_pool + 2 * d_hgrn]
    iz = pj_ref[:, d_pool + 2 * d_hgrn:d_pool + 3 * d_hgrn]
    v = iz * _sigmoid(iz)
    fkq = (lb + (1.0 - lb) * _sigmoid(fz), (1.0 - lb) * _sigmoid(-fz), pj_ref[:, d_pool:d_pool + d_hgrn])

    eye = (lax.broadcasted_iota(jnp.int32, (HEAD, HEAD), 0) == lax.broadcasted_iota(jnp.int32, (HEAD, HEAD), 1))

    def to_column(row):
        return jnp.sum(jnp.where(eye, row, 0.0), axis=1, keepdims=True)

    for i in range(sb):
        for hd in range(n_heads):
            sl = slice(hd * HEAD, (hd + 1) * HEAD)
            f_col, k_col, q_col = [to_column(x[i:i + 1, sl]) for x in fkq]
            s_new = f_col * hst_ref[i, hd] + k_col * v[i:i + 1, sl]
            nhst_ref[i, hd] = s_new
            oraw_ref[i:i + 1, sl] = jnp.sum(q_col * s_new, axis=0, keepdims=True)


def _out_mlp_kernel(*refs, n_cast, layer, final_norm, ff_chunk, sample_rows, hgrn):
    refs = list(refs)
    m_ref, x_ref = refs[:2]
    del refs[:2]
    if sample_rows:
        ms_ref, xs_ref = refs[:2]
        del refs[:2]
    weights = refs[:5]
    del refs[:5]
    if hgrn is not None:
        pj_ref, hst_ref, lbl_ref = refs[:3]
        del refs[:3]
    cast_in = refs[:n_cast]
    del refs[:n_cast]
    o_ref = refs.pop(0)
    if sample_rows:
        os_ref = refs.pop(0)
    if hgrn is not None:
        nhst_ref, oraw_ref = refs[:2]
        del refs[:2]
    cast_out = refs
    for src_ref, dst_ref in zip(cast_in, cast_out):
        dst_ref[...] = src_ref[...].astype(BF16)

    between = None
    if hgrn is not None:
        between = functools.partial(_hgrn_sample_step, pj_ref, hst_ref, lbl_ref, nhst_ref, oraw_ref, **hgrn)
    o_ref[...] = _mlp_block(m_ref[...], x_ref[...], *weights, layer=layer, final_norm=final_norm,
                            ff_chunk=ff_chunk, between=between)

    if sample_rows:
        @pl.when(pl.program_id(0) == pl.num_programs(0) - 1)
        def _():
            os_ref[...] = _mlp_block(ms_ref[...], xs_ref[...], *weights, layer=layer, final_norm=final_norm,
                                     ff_chunk=ff_chunk)


def out_mlp(mixed, x, sample, w_out, g_mlp, w1, w2, g_final, to_cast, hgrn_sample=None, *, layer, final_norm, bm):
    m, d = x.shape
    d_ff = w1.shape[1]
    steps = m // bm
    cast_in, cast_out, cast_shapes = _cast_specs(to_cast, steps)
    args = [mixed, x]
    in_specs = [pl.BlockSpec((bm, d), lambda i: (i, 0)), pl.BlockSpec((bm, d), lambda i: (i, 0))]
    out_specs = [pl.BlockSpec((bm, d), lambda i: (i, 0))]
    out_shapes = [jax.ShapeDtypeStruct((m, d), F32)]
    if sample is not None:
        ms = sample[1].shape[0]
        args += list(sample)
        in_specs += [_resident((ms, d)), _resident((ms, d))]
        out_specs.append(pl.BlockSpec((ms, d), lambda i: (0, 0)))
        out_shapes.append(jax.ShapeDtypeStruct((ms, d), F32))
    args += [w_out, g_mlp, w1, w2, g_final]
    in_specs += [_resident((d, d)), _resident(g_mlp.shape), _resident((d, d_ff)), _resident((d_ff, d)),
                 _resident((1, d))]
    hgrn = None
    if hgrn_sample is not None:
        proj_s, state, lbl, d_pool, hgrn_layer = hgrn_sample
        nseq, n_in = proj_s.shape
        assert nseq % steps == 0
        sb = nseq // steps
        n_heads = state.shape[1]
        args += [proj_s.reshape(steps, sb, n_in), state, lbl]
        in_specs += [pl.BlockSpec((None, sb, n_in), lambda i: (i, 0, 0)),
                     pl.BlockSpec((sb, n_heads, HEAD, HEAD), lambda i: (i, 0, 0, 0)), _resident(lbl.shape)]
        out_specs += [pl.BlockSpec((sb, n_heads, HEAD, HEAD), lambda i: (i, 0, 0, 0)),
                      pl.BlockSpec((None, sb, n_heads * HEAD), lambda i: (i, 0, 0))]
        out_shapes += [jax.ShapeDtypeStruct(state.shape, F32),
                       jax.ShapeDtypeStruct((steps, sb, n_heads * HEAD), F32)]
        hgrn = dict(layer=hgrn_layer, d_pool=d_pool)
    kern = functools.partial(_out_mlp_kernel, n_cast=len(to_cast), layer=layer, final_norm=final_norm,
                             ff_chunk=1024, sample_rows=sample is not None, hgrn=hgrn)
    return pl.pallas_call(
        kern,
        grid=(steps,),
        in_specs=in_specs + cast_in,
        out_specs=out_specs + cast_out,
        out_shape=out_shapes + cast_shapes,
        compiler_params=pltpu.CompilerParams(dimension_semantics=("arbitrary",),
                                             vmem_limit_bytes=VMEM_LIMIT_V7X),
        name="out_mlp",
    )(*args, *[w for w, _ in to_cast])


def _even_prompt_kernel(*refs, n_cast, layer, tb, tiles_per_seq):
    xn_ref, x0_ref, g_ref, win_ref, wpool_ref, pscale_ref, lbl_ref, gain_ref = refs[:8]
    cast_in, (mixed_ref, pstate_ref, hstate_ref) = refs[8:8 + n_cast], refs[8 + n_cast:11 + n_cast]
    cast_out = refs[11 + n_cast:11 + 2 * n_cast]
    (pu_ref, pq_ref, pf_ref, pi_ref, pg_ref, hn_ref, ext_ref, st_ref, k_ref,
     bcum_ref, dec_ref, gate_ref, v_ref, qd_ref, kd_ref, ke_ref, qb_ref, kb_ref, qr_ref, kc_ref) = refs[11 + 2 * n_cast:]
    for src_ref, dst_ref in zip(cast_in, cast_out):
        dst_ref[...] = src_ref[...].astype(BF16)

    step = pl.program_id(0)
    d_pool = wpool_ref.shape[0]
    n_heads = st_ref.shape[0]
    d_hgrn = n_heads * HEAD
    nc = tb // HGRN_CHUNK
    nblk = tb // HGRN_BLOCK
    j = step % tiles_per_seq
    last = tiles_per_seq - 1
    o_q, o_f, o_i, o_g = d_pool, d_pool + d_hgrn, d_pool + 2 * d_hgrn, d_pool + 3 * d_hgrn
    sections = [(pu_ref, 0), (pq_ref, o_q), (pf_ref, o_f), (pi_ref, o_i), (pg_ref, o_g)]

    def chunks(dst_ref, col0):
        def make(c0, c1):
            def run():
                dst_ref[:, c0:c1] = jnp.dot(hn_ref[...], win_ref[:, col0 + c0:col0 + c1],
                                            preferred_element_type=F32)
            return run
        width = dst_ref.shape[1]
        return [make(c0, min(c0 + MXU_COLS_V7X, width)) for c0 in range(0, width, MXU_COLS_V7X)]

    @pl.when(step == 0)
    def _():
        hn_ref[...] = _rms(x0_ref[...], g_ref[...]).astype(BF16)
        for dst_ref, col0 in sections:
            for run in chunks(dst_ref, col0):
                run()

    r1_, r2_ = POOL_PAD + POOL_HIST, POOL_PAD + POOL_HIST + tb

    @pl.when(j == 0)
    def _():
        ext_ref[0:r1_, :] = jnp.zeros((r1_, d_pool), F32)
        st_ref[...] = jnp.zeros(st_ref.shape, F32)

    hn_ref[...] = _rms(xn_ref[...], g_ref[...]).astype(BF16)

    u = pu_ref[...]
    ext_ref[r1_:r2_, :] = u
    groups_per_tile = LANES // POOL_GROUP_DIM
    lane_grp = lax.broadcasted_iota(jnp.int32, (tb, LANES), 1) // POOL_GROUP_DIM
    pos = j * tb + lax.broadcasted_iota(jnp.int32, (tb, LANES), 0)
    means = []
    for lt in range(d_pool // LANES):
        wins = POOL_WINDOWS[lt * groups_per_tile:(lt + 1) * groups_per_tile]
        cur, w, got = ext_ref[0:r2_, lt * LANES:(lt + 1) * LANES], 1, {}
        while w < wins[-1]:
            cur = cur + pltpu.roll(cur, w, 0)
            w *= 2
            if w in wins:
                got[w] = cur[r1_:]
        ssum, win = got[wins[-1]], jnp.full((tb, LANES), wins[-1], jnp.int32)
        for g_ in range(groups_per_tile - 2, -1, -1):
            ssum = jnp.where(lane_grp == g_, got[wins[g_]], ssum)
            win = jnp.where(lane_grp == g_, wins[g_], win)
        cnt = jnp.minimum(win, pos + 1).astype(F32)
        means.append(ssum / cnt)
    diff = (jnp.concatenate(means, axis=1) - u).astype(BF16)
    pool_out = jnp.dot(diff, wpool_ref[...], preferred_element_type=F32) * pscale_ref[...]
    mixed_ref[:, 0:d_pool] = pool_out.astype(BF16)

    @pl.when(j == last)
    def _():
        pstate_ref[...] = ext_ref[pl.ds(r2_ - POOL_BUF, POOL_BUF), :]

    ext_ref[POOL_PAD:r1_, :] = ext_ref[POOL_PAD + tb:r1_ + tb, :]
    for run in chunks(pu_ref, 0):
        run()

    lb = _lower_bound(lbl_ref[...], layer)
    for c in range(nc):
        rows = slice(c * HGRN_CHUNK, (c + 1) * HGRN_CHUNK)
        fz = pf_ref[rows, :]
        a = jnp.exp(-jnp.abs(fz))
        r = 1.0 / (1.0 + a)
        ar = a * r
        sig = jnp.where(fz >= 0, r, ar)
        sig_neg = jnp.where(fz >= 0, ar, r)
        k_ref[rows, :] = (1.0 - lb) * sig_neg
        dcy = lb + (1.0 - lb) * sig
        row = lax.broadcasted_iota(jnp.int32, dcy.shape, 0)
        shift = 1
        while shift < HGRN_CHUNK:
            dcy = dcy * jnp.where(row >= shift, pltpu.roll(dcy, shift, 0), 1.0)
            shift *= 2
        bcum_ref[rows, :] = dcy
        iz = pi_ref[rows, :]
        v_ref[rows, :] = (iz * _sigmoid(iz)).astype(BF16)
        gate_ref[rows, :] = gain_ref[...] * _sigmoid(pg_ref[rows, :])
    for run in chunks(pf_ref, o_f) + chunks(pi_ref, o_i):
        run()

    def prod(rows_):
        out = None
        for r_ in rows_:
            out = r_ if out is None else out * r_
        return out

    nsub = HGRN_BLOCK // HGRN_CHUNK
    half = nsub // 2
    for b in range(nblk):
        whole = [bcum_ref[(b * nsub + i + 1) * HGRN_CHUNK - 1:(b * nsub + i + 1) * HGRN_CHUNK, :]
                 for i in range(nsub)]
        dec_ref[b:b + 1, :] = prod(whole)
        for i in range(nsub):
            c = b * nsub + i
            rows = slice(c * HGRN_CHUNK, (c + 1) * HGRN_CHUNK)
            dcy = bcum_ref[rows, :]
            qd = pq_ref[rows, :] * dcy
            kd = k_ref[rows, :] / dcy
            ke = kd * whole[i]
            qd_ref[rows, :] = qd.astype(BF16)
            kd_ref[rows, :] = kd.astype(BF16)
            ke_ref[rows, :] = ke.astype(BF16)
            from_start, to_end = prod(whole[:i]), prod(whole[i + 1:])
            qb_ref[rows, :] = (qd if from_start is None else qd * from_start).astype(BF16)
            kb_ref[rows, :] = (ke if to_end is None else ke * to_end).astype(BF16)
            if i >= half:
                from_mid = prod(whole[half:i])
                qr_ref[rows, :] = (qd if from_mid is None else qd * from_mid).astype(BF16)
                kc_ref[rows, :] = jnp.zeros((HGRN_CHUNK, d_hgrn), BF16)
            else:
                to_mid = prod(whole[i + 1:half])
                kc_ref[rows, :] = (ke if to_mid is None else ke * to_mid).astype(BF16)

    rb = lax.broadcasted_iota(jnp.int32, (HGRN_BLOCK, HGRN_BLOCK), 0)
    cb = lax.broadcasted_iota(jnp.int32, (HGRN_BLOCK, HGRN_BLOCK), 1)
    rsub, csub = rb // HGRN_CHUNK, cb // HGRN_CHUNK
    m_diag = (rsub == csub) & (cb <= rb)
    m_adj = (csub == rsub - 1) & (rsub != half)
    eye = rb == cb
    hb = HGRN_BLOCK // 2
    nt_dims = (((1,), (1,)), ((), ()))
    tn_dims = (((0,), (0,)), ((), ()))
    fillers = chunks(pq_ref, o_q) + chunks(pg_ref, o_g)

    def fill(n):
        for _ in range(min(n, len(fillers))):
            fillers.pop(0)()

    per_stage = -(-len(fillers) // (3 * nblk))
    for b in range(nblk):
        rows = slice(b * HGRN_BLOCK, (b + 1) * HGRN_BLOCK)
        far = slice(b * HGRN_BLOCK + hb, (b + 1) * HGRN_BLOCK)
        g12s, g3s = [], []
        for hd in range(n_heads):
            sl = slice(hd * HEAD, (hd + 1) * HEAD)
            kk = jnp.concatenate([kd_ref[rows, sl], ke_ref[rows, sl]], axis=0)
            g12s.append(lax.dot_general(qd_ref[rows, sl], kk, nt_dims, preferred_element_type=F32))
            g3s.append(lax.dot_general(qr_ref[far, sl], kc_ref[rows, sl], nt_dims, preferred_element_type=F32))
        fill(per_stage)
        outs = []
        for hd in range(n_heads):
            sl = slice(hd * HEAD, (hd + 1) * HEAD)
            g12 = g12s[hd]
            scores = jnp.where(m_diag, g12[:, 0:HGRN_BLOCK], jnp.where(m_adj, g12[:, HGRN_BLOCK:], 0.0))
            scores = jnp.concatenate([scores[0:hb], scores[hb:] + g3s[hd]], axis=0).astype(BF16)
            v_blk = v_ref[rows, sl]
            st = st_ref[hd]
            lhs = jnp.concatenate([scores, qb_ref[rows, sl]], axis=1)
            rhs = jnp.concatenate([v_blk, st.astype(BF16)], axis=0)
            outs.append(jnp.dot(lhs, rhs, preferred_element_type=F32))
            d_col = jnp.sum(jnp.where(eye, dec_ref[b:b + 1, sl], 0.0), axis=1, keepdims=True)
            st_ref[hd] = st * d_col + lax.dot_general(kb_ref[rows, sl], v_blk, tn_dims,
                                                      preferred_element_type=F32)
        fill(per_stage)
        for hd in range(n_heads):
            sl = slice(hd * HEAD, (hd + 1) * HEAD)
            o = outs[hd]
            o = o * lax.rsqrt(jnp.mean(o * o, axis=-1, keepdims=True) + EPS)
            mixed_ref[rows, d_pool + hd * HEAD:d_pool + (hd + 1) * HEAD] = (o * gate_ref[rows, sl]).astype(BF16)
        fill(per_stage)
    fill(len(fillers))

    @pl.when(j == last)
    def _():
        hstate_ref[...] = st_ref[...]


def even_mix_prompt(x, g, w_in, wpool_bd, pscale, lbl, gain, to_cast, *, layer, tb):
    b, t, d = x.shape
    n_in = w_in.shape[1]
    d_pool = wpool_bd.shape[0]
    d_hgrn = gain.shape[1]
    n_heads = d_hgrn // HEAD
    tps = t // tb
    n_tiles = b * tps
    cast_in, cast_out, cast_shapes = _cast_specs(to_cast, n_tiles)
    kern = functools.partial(_even_prompt_kernel, n_cast=len(to_cast), layer=layer, tb=tb, tiles_per_seq=tps)

    def next_tile(s):
        tile = jnp.minimum(s + 1, n_tiles - 1)
        return tile // tps, tile % tps

    assert d_pool % LANES == 0 and LANES % POOL_GROUP_DIM == 0 and tb % HGRN_BLOCK == 0
    return pl.pallas_call(
        kern,
        grid=(n_tiles,),
        in_specs=[pl.BlockSpec((None, tb, d), lambda s: (*next_tile(s), 0)),
                  pl.BlockSpec((None, tb, d), lambda s: (0, 0, 0)),
                  _resident((1, d)), _resident((d, n_in)), _resident((d_pool, d_pool)),
                  _resident((1, d_pool)), _resident(lbl.shape), _resident((1, d_hgrn))] + cast_in,
        out_specs=[pl.BlockSpec((None, tb, d), lambda s: (s // tps, s % tps, 0)),
                   pl.BlockSpec((None, POOL_BUF, d_pool), lambda s: (s // tps, 0, 0)),
                   pl.BlockSpec((None, n_heads, HEAD, HEAD), lambda s: (s // tps, 0, 0, 0))] + cast_out,
        out_shape=[jax.ShapeDtypeStruct((b, t, d), BF16),
                   jax.ShapeDtypeStruct((b, POOL_BUF, d_pool), F32),
                   jax.ShapeDtypeStruct((b, n_heads, HEAD, HEAD), F32)] + cast_shapes,
        scratch_shapes=[pltpu.VMEM((tb, d_pool), F32)]
                       + [pltpu.VMEM((tb, d_hgrn), F32)] * 4
                       + [pltpu.VMEM((tb, d), BF16),
                        pltpu.VMEM((POOL_PAD + POOL_HIST + tb, d_pool), F32),
                        pltpu.VMEM((n_heads, HEAD, HEAD), F32),
                        pltpu.VMEM((tb, d_hgrn), F32),
                        pltpu.VMEM((tb, d_hgrn), F32),
                        pltpu.VMEM((tb // HGRN_BLOCK, d_hgrn), F32),
                        pltpu.VMEM((tb, d_hgrn), F32)]
                       + [pltpu.VMEM((tb, d_hgrn), BF16)] * 8,
        compiler_params=pltpu.CompilerParams(dimension_semantics=("arbitrary",),
                                             vmem_limit_bytes=VMEM_LIMIT_V7X),
        name="even_mix_prompt",
    )(x, x, g, w_in, wpool_bd, pscale, lbl, gain, *[w for w, _ in to_cast])


def _even_sample_tail_kernel(proj_ref, oraw_ref, pool_ref, x_ref, wpool_ref, pscale_ref, gain_ref, wout_ref, g_ref,
                             w1_ref, w2_ref, gf_ref, o_ref, npool_ref, *, layer, final_norm, ff_chunk, pos0):
    d_pool = wpool_ref.shape[0]
    d_hgrn = gain_ref.shape[1]

    u = proj_ref[:, 0:d_pool]
    acc = u
    snaps = {}
    for s in range(1, POOL_WINDOWS[-1]):
        acc = acc + pool_ref[POOL_BUF - s]
        if s + 1 in POOL_WINDOWS:
            snaps[s + 1] = acc
    ssum, win = _pool_select(snaps, u.shape)
    cnt = jnp.minimum(win, pos0 + 1).astype(F32)
    diff = (ssum / cnt - u).astype(BF16)
    pool_out = jnp.dot(diff, wpool_ref[...], preferred_element_type=F32) * pscale_ref[...]
    npool_ref[0:POOL_BUF - 1] = pool_ref[1:POOL_BUF]
    npool_ref[POOL_BUF - 1] = u

    gate = gain_ref[...] * _sigmoid(proj_ref[:, d_pool + 3 * d_hgrn:d_pool + 4 * d_hgrn])
    mixed = [pool_out.astype(BF16)]
    for hd in range(d_hgrn // HEAD):
        sl = slice(hd * HEAD, (hd + 1) * HEAD)
        o = oraw_ref[:, sl]
        o = o * lax.rsqrt(jnp.mean(o * o, axis=-1, keepdims=True) + EPS)
        mixed.append((o * gate[:, sl]).astype(BF16))
    o_ref[...] = _mlp_block(jnp.concatenate(mixed, axis=1), x_ref[...], wout_ref, g_ref, w1_ref, w2_ref, gf_ref,
                            layer=layer, final_norm=final_norm, ff_chunk=ff_chunk)


def even_sample_tail(proj, o_raw, pool_rows, x_s, wpool_bd, pscale, gain, w_out, g_mlp, w1, w2, g_final,
                     *, layer, final_norm, pos0):
    b, n_in = proj.shape
    d = x_s.shape[1]
    d_pool = wpool_bd.shape[0]
    d_hgrn = gain.shape[1]
    d_ff = w1.shape[1]
    kern = functools.partial(_even_sample_tail_kernel, layer=layer, final_norm=final_norm, ff_chunk=1024, pos0=pos0)
    return pl.pallas_call(
        kern,
        grid=(1,),
        in_specs=[_resident((b, n_in)), _resident((b, d_hgrn)), _resident((POOL_BUF, b, d_pool)), _resident((b, d)),
                  _resident((d_pool, d_pool)), _resident((1, d_pool)), _resident((1, d_hgrn)),
                  _resident((d, d)), _resident(g_mlp.shape), _resident((d, d_ff)), _resident((d_ff, d)),
                  _resident((1, d))],
        out_specs=[pl.BlockSpec((b, d), lambda i: (0, 0)), pl.BlockSpec((POOL_BUF, b, d_pool), lambda i: (0, 0, 0))],
        out_shape=[jax.ShapeDtypeStruct((b, d), F32), jax.ShapeDtypeStruct((POOL_BUF, b, d_pool), F32)],
        compiler_params=pltpu.CompilerParams(dimension_semantics=("arbitrary",),
                                             vmem_limit_bytes=VMEM_LIMIT_V7X),
        name="even_sample_tail",
    )(proj, o_raw, pool_rows, x_s, wpool_bd, pscale, gain, w_out, g_mlp, w1, w2, g_final)


def _odd_kernel(x_ref, xs_ref, cst_ref, g_ref, win_ref, cw_ref, mixed_ref, cstate_ref, ms_ref, ncst_ref, zext_ref,
                *, tb):
    i, j = pl.program_id(0), pl.program_id(1)
    last = pl.num_programs(1) - 1
    dc = cw_ref.shape[1]

    def gated_taps(x):
        h = _rms(x, g_ref[...]).astype(BF16)
        cg = jnp.dot(h, win_ref[:, dc:2 * dc], preferred_element_type=F32)
        hv = jnp.dot(h, win_ref[:, 2 * dc:3 * dc], preferred_element_type=F32)
        z = cg * hv
        bg = jnp.dot(h, win_ref[:, 0:dc], preferred_element_type=F32)
        return bg, z, cw_ref[CONV_WIDTH - 1:CONV_WIDTH, :] * z

    @pl.when(j == 0)
    def _():
        zext_ref[0:CONV_HIST, :] = jnp.zeros((CONV_HIST, dc), F32)

    bg, z, conv = gated_taps(x_ref[...])
    zext_ref[CONV_HIST:CONV_HIST + tb, :] = z
    for s in range(1, CONV_WIDTH):
        conv = conv + cw_ref[CONV_WIDTH - 1 - s:CONV_WIDTH - s, :] * zext_ref[pl.ds(CONV_HIST - s, tb), :]
    mixed_ref[...] = (bg * conv).astype(BF16)

    @pl.when(j == last)
    def _():
        cstate_ref[...] = zext_ref[pl.ds(CONV_HIST + tb - CONV_BUF, CONV_BUF), :]

    zext_ref[0:CONV_HIST, :] = zext_ref[tb:tb + CONV_HIST, :]

    @pl.when((i == pl.num_programs(0) - 1) & (j == last))
    def _():
        bg_s, z_s, conv_s = gated_taps(xs_ref[...])
        for s in range(1, CONV_WIDTH):
            r0 = (CONV_BUF - s) * dc
            conv_s = conv_s + cw_ref[CONV_WIDTH - 1 - s:CONV_WIDTH - s, :] * cst_ref[:, r0:r0 + dc]
        ms_ref[...] = (bg_s * conv_s).astype(BF16)
        ncst_ref[:, 0:(CONV_BUF - 1) * dc] = cst_ref[:, dc:CONV_BUF * dc]
        ncst_ref[:, (CONV_BUF - 1) * dc:CONV_BUF * dc] = z_s


def odd_mix(x, x_s, cstate_s, g, w_in, conv_w, *, tb):
    b, t, d = x.shape
    ms = x_s.shape[0]
    dc = conv_w.shape[1]
    kern = functools.partial(_odd_kernel, tb=tb)
    return pl.pallas_call(
        kern,
        grid=(b, t // tb),
        in_specs=[pl.BlockSpec((None, tb, d), lambda i, j: (i, j, 0)),
                  _resident((ms, d)), _resident((ms, CONV_BUF * dc)),
                  _resident((1, d)), _resident((d, 3 * dc)), _resident((CONV_WIDTH, dc))],
        out_specs=[pl.BlockSpec((None, tb, dc), lambda i, j: (i, j, 0)),
                   pl.BlockSpec((None, CONV_BUF, dc), lambda i, j: (i, 0, 0)),
                   pl.BlockSpec((ms, dc), lambda i, j: (0, 0)),
                   pl.BlockSpec((ms, CONV_BUF * dc), lambda i, j: (0, 0))],
        out_shape=[jax.ShapeDtypeStruct((b, t, dc), BF16),
                   jax.ShapeDtypeStruct((b, CONV_BUF, dc), F32),
                   jax.ShapeDtypeStruct((ms, dc), BF16),
                   jax.ShapeDtypeStruct((ms, CONV_BUF * dc), F32)],
        scratch_shapes=[pltpu.VMEM((CONV_HIST + tb, dc), F32)],
        compiler_params=pltpu.CompilerParams(dimension_semantics=("arbitrary", "arbitrary"),
                                             vmem_limit_bytes=VMEM_LIMIT_V7X),
        name="odd_mix",
    )(x, x_s, cstate_s, g, w_in, conv_w)


def _block_diag(w):
    g, c, _ = w.shape
    rows = [jnp.pad(w[i], ((0, 0), (i * c, (g - 1 - i) * c))) for i in range(g)]
    return jnp.concatenate(rows, axis=0)


def kernel(x_prompt, x_sample, state_pool, state_hgrn, state_conv, norm_mix, norm_mlp, norm_final, even_w_in, pool_w, pool_scale, hgrn_lb_logits, hgrn_gain, even_w_out, odd_w_in, conv_w, odd_w_out, ff_w1, ff_w2):
    depth = norm_mix.shape[0]
    b, t, d = x_prompt.shape
    db, ds, _ = x_sample.shape
    assert ds == 1, "the sample group carries one token per sequence"
    xp = x_prompt.reshape(b * t, d)
    xs = x_sample.reshape(db, d)
    pool_p, hgrn_p, conv_p, pool_s, hgrn_s, conv_s = [], [], [], [], [], []

    def layer_weights(l):
        w_in, w_out = (even_w_in, even_w_out) if l % 2 == 0 else (odd_w_in, odd_w_out)
        return dict(w_in=(w_in[l // 2], None), w_out=(w_out[l // 2], None), w1=(ff_w1, l), w2=(ff_w2, l))

    def cast_now(w, layer):
        return (w if layer is None else w[layer]).astype(BF16)

    have = {}
    for l in range(depth):
        g_mix = norm_mix[l][None]
        wl = layer_weights(l)
        missing = [k for k in ("w_out", "w1", "w2") if k not in have]
        if l % 2 == 0:
            e = l // 2
            pool_bd = _block_diag(pool_w[e]).astype(BF16)
            shared = (pool_bd, pool_scale[e][None], hgrn_lb_logits, hgrn_gain[e][None])
            proj_s, have["w_in"] = norm_proj(xs, g_mix, have.get("w_in", wl["w_in"][0]), col_blocks=2)
            mixed_p, p_new, s_new, *cast = even_mix_prompt(xp.reshape(b, t, d), g_mix, have["w_in"], *shared,
                                                           [wl[k] for k in missing], layer=l, tb=256)
            have.update(zip(missing, cast))
            pool_p.append(p_new)
            hgrn_p.append(s_new)
            nxt = layer_weights(l + 1) if l + 1 < depth else {}
            xp, s_new, o_raw, *cast = out_mlp(
                mixed_p.reshape(b * t, d), xp, None, have["w_out"], norm_mlp, have["w1"], have["w2"],
                norm_final[None], list(nxt.values()),
                (proj_s, state_hgrn[e], hgrn_lb_logits, pool_bd.shape[0], l),
                layer=l, final_norm=l == depth - 1, bm=512)
            hgrn_s.append(s_new)
            xs, p_new = even_sample_tail(proj_s, o_raw.reshape(db, -1), jnp.swapaxes(state_pool[e], 0, 1), xs,
                                         pool_bd, pool_scale[e][None], hgrn_gain[e][None], have["w_out"], norm_mlp,
                                         have["w1"], have["w2"], norm_final[None],
                                         layer=l, final_norm=l == depth - 1, pos0=PAST_LEN)
            pool_s.append(jnp.swapaxes(p_new, 0, 1))
            have = dict(zip(nxt, cast))
            continue
        else:
            o = l // 2
            have.update({k: cast_now(*wl[k]) for k in ["w_in"] + missing if k not in have})
            mixed_p, c_new, mixed_s, cs_new = odd_mix(xp.reshape(b, t, d), xs, state_conv[o].reshape(db, -1), g_mix,
                                                      have["w_in"], conv_w[o], tb=512)
            conv_p.append(c_new)
            conv_s.append(cs_new.reshape(db, CONV_BUF, -1))
        nxt = layer_weights(l + 1) if l + 1 < depth else {}
        xp, xs, *cast = out_mlp(mixed_p.reshape(b * t, d), xp, (mixed_s, xs), have["w_out"], norm_mlp, have["w1"],
                                have["w2"], norm_final[None], list(nxt.values()),
                                layer=l, final_norm=l == depth - 1, bm=512)
        have = dict(zip(nxt, cast))
    return (xp.reshape(b, t, d), xs.reshape(db, ds, d), jnp.stack(pool_p), jnp.stack(hgrn_p), jnp.stack(conv_p),
            jnp.stack(pool_s), jnp.stack(hgrn_s), jnp.stack(conv_s))
```

```python
import functools

import jax
import jax.numpy as jnp
from jax import lax
from jax.experimental import pallas as pl
from jax.experimental.pallas import tpu as pltpu

F32 = jnp.float32
BF16 = jnp.bfloat16

EPS = 1e-6
PAST_LEN = 16384
POOL_WINDOWS = (2, 4, 8, 16)
POOL_GROUP_DIM = 64
POOL_BUF = max(POOL_WINDOWS) - 1
POOL_HIST = 16
POOL_PAD = 8
LANES = 128
HGRN_CHUNK = 32
HGRN_BLOCK = 128
HEAD = 128
CONV_WIDTH = 3
CONV_BUF = CONV_WIDTH - 1
CONV_HIST = 8

VMEM_LIMIT_V7X = 56 * 1024 * 1024
MXU_COLS_V7X = 256


def _resident(shape):
    nd = len(shape)
    return pl.BlockSpec(shape, lambda *_: (0,) * nd, pipeline_mode=pl.Buffered(1))


def _cast_specs(weights, n_steps):
    in_specs, out_specs, out_shapes = [], [], []
    for w, layer in weights:
        r, c = w.shape[-2:]
        assert r % (n_steps * 16) == 0, "row slabs must be whole bf16 sublane tiles"
        if layer is None:
            in_specs.append(pl.BlockSpec((r // n_steps, c), lambda s: (s, 0)))
        else:
            in_specs.append(pl.BlockSpec((None, r // n_steps, c), lambda s, layer=layer: (layer, s, 0)))
        out_specs.append(pl.BlockSpec((r // n_steps, c), lambda s: (s, 0)))
        out_shapes.append(jax.ShapeDtypeStruct((r, c), BF16))
    return in_specs, out_specs, out_shapes


def _rms(x, g):
    return x * lax.rsqrt(jnp.mean(x * x, axis=-1, keepdims=True) + EPS) * g


def _sigmoid(x):
    return 1.0 / (1.0 + jnp.exp(-x))


def _lower_bound(lbl, layer):
    e = jnp.exp(lbl - jnp.max(lbl, axis=0, keepdims=True))
    p = e / jnp.sum(e, axis=0, keepdims=True)
    return jnp.sum(p[0:layer + 1], axis=0, keepdims=True)


def _pool_select(snaps, u_shape):
    grp = lax.broadcasted_iota(jnp.int32, u_shape, 1) // POOL_GROUP_DIM
    s = snaps[POOL_WINDOWS[-1]]
    for g in range(len(POOL_WINDOWS) - 2, -1, -1):
        s = jnp.where(grp == g, snaps[POOL_WINDOWS[g]], s)
    win = jnp.left_shift(2, grp)
    return s, win


def _norm_proj_kernel(x_ref, g_ref, w_ref, o_ref, *wb_ref):
    h = _rms(x_ref[...], g_ref[...]).astype(BF16)
    w = w_ref[...].astype(BF16)
    for ref in wb_ref:
        ref[...] = w
    o_ref[...] = jnp.dot(h, w, preferred_element_type=F32)


def norm_proj(x, g, w, *, col_blocks):
    m, d = x.shape
    n = w.shape[1]
    bn = n // col_blocks
    assert n % col_blocks == 0 and bn % LANES == 0
    cast = w.dtype != BF16
    outs = pl.pallas_call(
        _norm_proj_kernel,
        grid=(col_blocks,),
        in_specs=[_resident((m, d)), _resident((1, d)), pl.BlockSpec((d, bn), lambda i: (0, i))],
        out_specs=[pl.BlockSpec((m, bn), lambda i: (0, i))] + [pl.BlockSpec((d, bn), lambda i: (0, i))] * cast,
        out_shape=[jax.ShapeDtypeStruct((m, n), F32)] + [jax.ShapeDtypeStruct((d, n), BF16)] * cast,
        compiler_params=pltpu.CompilerParams(dimension_semantics=("arbitrary",),
                                             vmem_limit_bytes=VMEM_LIMIT_V7X),
        name="norm_proj",
    )(x, g, w)
    return outs if cast else (outs[0], w)


def _mlp_block(mixed, x, wout_ref, g_ref, w1_ref, w2_ref, gf_ref, *, layer, final_norm, ff_chunk, between=None):
    d_ff = w1_ref.shape[1]
    y = x + jnp.dot(mixed, wout_ref[...], preferred_element_type=F32)
    h = _rms(y, g_ref[layer:layer + 1, :]).astype(BF16)
    acc = y
    for c in range(d_ff // ff_chunk):
        a = jnp.dot(h, w1_ref[:, c * ff_chunk:(c + 1) * ff_chunk], preferred_element_type=F32)
        a = jnp.square(jnp.maximum(a, 0.0)).astype(BF16)
        acc = acc + jnp.dot(a, w2_ref[c * ff_chunk:(c + 1) * ff_chunk, :], preferred_element_type=F32)
        if c == 0 and between is not None:
            between()
    return _rms(acc, gf_ref[...]) if final_norm else acc


def _hgrn_sample_step(pj_ref, hst_ref, lbl_ref, nhst_ref, oraw_ref, *, layer, d_pool):
    sb, n_heads = hst_ref.shape[0], hst_ref.shape[1]
    d_hgrn = n_heads * HEAD
    lb = _lower_bound(lbl_ref[...], layer)
    fz = pj_ref[:, d_pool + d_hgrn:d_pool + 2 * d_hgrn]
    iz = pj_ref[:, d_pool + 2 * d_hgrn:d_pool + 3 * d_hgrn]
    v = iz * _sigmoid(iz)
    fkq = (lb + (1.0 - lb) * _sigmoid(fz), (1.0 - lb) * _sigmoid(-fz), pj_ref[:, d_pool:d_pool + d_hgrn])

    eye = (lax.broadcasted_iota(jnp.int32, (HEAD, HEAD), 0) == lax.broadcasted_iota(jnp.int32, (HEAD, HEAD), 1))

    def to_column(row):
        return jnp.sum(jnp.where(eye, row, 0.0), axis=1, keepdims=True)

    for i in range(sb):
        for hd in range(n_heads):
            sl = slice(hd * HEAD, (hd + 1) * HEAD)
            f_col, k_col, q_col = [to_column(x[i:i + 1, sl]) for x in fkq]
            s_new = f_col * hst_ref[i, hd] + k_col * v[i:i + 1, sl]
            nhst_ref[i, hd] = s_new
            oraw_ref[i:i + 1, sl] = jnp.sum(q_col * s_new, axis=0, keepdims=True)


def _out_mlp_kernel(*refs, n_cast, layer, final_norm, ff_chunk, sample_rows, hgrn):
    refs = list(refs)
    m_ref, x_ref = refs[:2]
    del refs[:2]
    if sample_rows:
        ms_ref, xs_ref = refs[:2]
        del refs[:2]
    weights = refs[:5]
    del refs[:5]
    if hgrn is not None:
        pj_ref, hst_ref, lbl_ref = refs[:3]
        del refs[:3]
    cast_in = refs[:n_cast]
    del refs[:n_cast]
    o_ref = refs.pop(0)
    if sample_rows:
        os_ref = refs.pop(0)
    if hgrn is not None:
        nhst_ref, oraw_ref = refs[:2]
        del refs[:2]
    cast_out = refs
    for src_ref, dst_ref in zip(cast_in, cast_out):
        dst_ref[...] = src_ref[...].astype(BF16)

    between = None
    if hgrn is not None:
        between = functools.partial(_hgrn_sample_step, pj_ref, hst_ref, lbl_ref, nhst_ref, oraw_ref, **hgrn)
    o_ref[...] = _mlp_block(m_ref[...], x_ref[...], *weights, layer=layer, final_norm=final_norm,
                            ff_chunk=ff_chunk, between=between)

    if sample_rows:
        @pl.when(pl.program_id(0) == pl.num_programs(0) - 1)
        def _():
            os_ref[...] = _mlp_block(ms_ref[...], xs_ref[...], *weights, layer=layer, final_norm=final_norm,
                                     ff_chunk=ff_chunk)


def out_mlp(mixed, x, sample, w_out, g_mlp, w1, w2, g_final, to_cast, hgrn_sample=None, *, layer, final_norm, bm):
    m, d = x.shape
    d_ff = w1.shape[1]
    steps = m // bm
    cast_in, cast_out, cast_shapes = _cast_specs(to_cast, steps)
    args = [mixed, x]
    in_specs = [pl.BlockSpec((bm, d), lambda i: (i, 0)), pl.BlockSpec((bm, d), lambda i: (i, 0))]
    out_specs = [pl.BlockSpec((bm, d), lambda i: (i, 0))]
    out_shapes = [jax.ShapeDtypeStruct((m, d), F32)]
    if sample is not None:
        ms = sample[1].shape[0]
        args += list(sample)
        in_specs += [_resident((ms, d)), _resident((ms, d))]
        out_specs.append(pl.BlockSpec((ms, d), lambda i: (0, 0)))
        out_shapes.append(jax.ShapeDtypeStruct((ms, d), F32))
    args += [w_out, g_mlp, w1, w2, g_final]
    in_specs += [_resident((d, d)), _resident(g_mlp.shape), _resident((d, d_ff)), _resident((d_ff, d)),
                 _resident((1, d))]
    hgrn = None
    if hgrn_sample is not None:
        proj_s, state, lbl, d_pool, hgrn_layer = hgrn_sample
        nseq, n_in = proj_s.shape
        assert nseq % steps == 0
        sb = nseq // steps
        n_heads = state.shape[1]
        args += [proj_s.reshape(steps, sb, n_in), state, lbl]
        in_specs += [pl.BlockSpec((None, sb, n_in), lambda i: (i, 0, 0)),
                     pl.BlockSpec((sb, n_heads, HEAD, HEAD), lambda i: (i, 0, 0, 0)), _resident(lbl.shape)]
        out_specs += [pl.BlockSpec((sb, n_heads, HEAD, HEAD), lambda i: (i, 0, 0, 0)),
                      pl.BlockSpec((None, sb, n_heads * HEAD), lambda i: (i, 0, 0))]
        out_shapes += [jax.ShapeDtypeStruct(state.shape, F32),
                       jax.ShapeDtypeStruct((steps, sb, n_heads * HEAD), F32)]
        hgrn = dict(layer=hgrn_layer, d_pool=d_pool)
    kern = functools.partial(_out_mlp_kernel, n_cast=len(to_cast), layer=layer, final_norm=final_norm,
                             ff_chunk=1024, sample_rows=sample is not None, hgrn=hgrn)
    return pl.pallas_call(
        kern,
        grid=(steps,),
        in_specs=in_specs + cast_in,
        out_specs=out_specs + cast_out,
        out_shape=out_shapes + cast_shapes,
        compiler_params=pltpu.CompilerParams(dimension_semantics=("arbitrary",),
                                             vmem_limit_bytes=VMEM_LIMIT_V7X),
        name="out_mlp",
    )(*args, *[w for w, _ in to_cast])


def _even_prompt_kernel(*refs, n_cast, layer, tb, tiles_per_seq):
    xn_ref, x0_ref, g_ref, win_ref, wpool_ref, pscale_ref, lbl_ref, gain_ref = refs[:8]
    cast_in, (mixed_ref, pstate_ref, hstate_ref) = refs[8:8 + n_cast], refs[8 + n_cast:11 + n_cast]
    cast_out = refs[11 + n_cast:11 + 2 * n_cast]
    (pu_ref, pq_ref, pf_ref, pi_ref, pg_ref, hn_ref, ext_ref, st_ref, k_ref,
     bcum_ref, dec_ref, gate_ref, v_ref, qd_ref, kd_ref, ke_ref, qb_ref, kb_ref, qr_ref, kc_ref) = refs[11 + 2 * n_cast:]
    for src_ref, dst_ref in zip(cast_in, cast_out):
        dst_ref[...] = src_ref[...].astype(BF16)

    step = pl.program_id(0)
    d_pool = wpool_ref.shape[0]
    n_heads = st_ref.shape[0]
    d_hgrn = n_heads * HEAD
    nc = tb // HGRN_CHUNK
    nblk = tb // HGRN_BLOCK
    j = step % tiles_per_seq
    last = tiles_per_seq - 1
    o_q, o_f, o_i, o_g = d_pool, d_pool + d_hgrn, d_pool + 2 * d_hgrn, d_pool + 3 * d_hgrn
    sections = [(pu_ref, 0), (pq_ref, o_q), (pf_ref, o_f), (pi_ref, o_i), (pg_ref, o_g)]

    def chunks(dst_ref, col0):
        def make(c0, c1):
            def run():
                dst_ref[:, c0:c1] = jnp.dot(hn_ref[...], win_ref[:, col0 + c0:col0 + c1],
                                            preferred_element_type=F32)
            return run
        width = dst_ref.shape[1]
        return [make(c0, min(c0 + MXU_COLS_V7X, width)) for c0 in range(0, width, MXU_COLS_V7X)]

    @pl.when(step == 0)
    def _():
        hn_ref[...] = _rms(x0_ref[...], g_ref[...]).astype(BF16)
        for dst_ref, col0 in sections:
            for run in chunks(dst_ref, col0):
                run()

    r1_, r2_ = POOL_PAD + POOL_HIST, POOL_PAD + POOL_HIST + tb

    @pl.when(j == 0)
    def _():
        ext_ref[0:r1_, :] = jnp.zeros((r1_, d_pool), F32)
        st_ref[...] = jnp.zeros(st_ref.shape, F32)

    hn_ref[...] = _rms(xn_ref[...], g_ref[...]).astype(BF16)

    u = pu_ref[...]
    ext_ref[r1_:r2_, :] = u
    groups_per_tile = LANES // POOL_GROUP_DIM
    lane_grp = lax.broadcasted_iota(jnp.int32, (tb, LANES), 1) // POOL_GROUP_DIM
    pos = j * tb + lax.broadcasted_iota(jnp.int32, (tb, LANES), 0)
    means = []
    for lt in range(d_pool // LANES):
        wins = POOL_WINDOWS[lt * groups_per_tile:(lt + 1) * groups_per_tile]
        cur, w, got = ext_ref[0:r2_, lt * LANES:(lt + 1) * LANES], 1, {}
        while w < wins[-1]:
            cur = cur + pltpu.roll(cur, w, 0)
            w *= 2
            if w in wins:
                got[w] = cur[r1_:]
        ssum, win = got[wins[-1]], jnp.full((tb, LANES), wins[-1], jnp.int32)
        for g_ in range(groups_per_tile - 2, -1, -1):
            ssum = jnp.where(lane_grp == g_, got[wins[g_]], ssum)
            win = jnp.where(lane_grp == g_, wins[g_], win)
        cnt = jnp.minimum(win, pos + 1).astype(F32)
        means.append(ssum / cnt)
    diff = (jnp.concatenate(means, axis=1) - u).astype(BF16)
    pool_out = jnp.dot(diff, wpool_ref[...], preferred_element_type=F32) * pscale_ref[...]
    mixed_ref[:, 0:d_pool] = pool_out.astype(BF16)

    @pl.when(j == last)
    def _():
        pstate_ref[...] = ext_ref[pl.ds(r2_ - POOL_BUF, POOL_BUF), :]

    ext_ref[POOL_PAD:r1_, :] = ext_ref[POOL_PAD + tb:r1_ + tb, :]
    for run in chunks(pu_ref, 0):
        run()

    lb = _lower_bound(lbl_ref[...], layer)
    for c in range(nc):
        rows = slice(c * HGRN_CHUNK, (c + 1) * HGRN_CHUNK)
        fz = pf_ref[rows, :]
        a = jnp.exp(-jnp.abs(fz))
        r = 1.0 / (1.0 + a)
        ar = a * r
        sig = jnp.where(fz >= 0, r, ar)
        sig_neg = jnp.where(fz >= 0, ar, r)
        k_ref[rows, :] = (1.0 - lb) * sig_neg
        dcy = lb + (1.0 - lb) * sig
        row = lax.broadcasted_iota(jnp.int32, dcy.shape, 0)
        shift = 1
        while shift < HGRN_CHUNK:
            dcy = dcy * jnp.where(row >= shift, pltpu.roll(dcy, shift, 0), 1.0)
            shift *= 2
        bcum_ref[rows, :] = dcy
        iz = pi_ref[rows, :]
        v_ref[rows, :] = (iz * _sigmoid(iz)).astype(BF16)
        gate_ref[rows, :] = gain_ref[...] * _sigmoid(pg_ref[rows, :])
    for run in chunks(pf_ref, o_f) + chunks(pi_ref, o_i):
        run()

    def prod(rows_):
        out = None
        for r_ in rows_:
            out = r_ if out is None else out * r_
        return out

    nsub = HGRN_BLOCK // HGRN_CHUNK
    half = nsub // 2
    for b in range(nblk):
        whole = [bcum_ref[(b * nsub + i + 1) * HGRN_CHUNK - 1:(b * nsub + i + 1) * HGRN_CHUNK, :]
                 for i in range(nsub)]
        dec_ref[b:b + 1, :] = prod(whole)
        for i in range(nsub):
            c = b * nsub + i
            rows = slice(c * HGRN_CHUNK, (c + 1) * HGRN_CHUNK)
            dcy = bcum_ref[rows, :]
            qd = pq_ref[rows, :] * dcy
            kd = k_ref[rows, :] / dcy
            ke = kd * whole[i]
            qd_ref[rows, :] = qd.astype(BF16)
            kd_ref[rows, :] = kd.astype(BF16)
            ke_ref[rows, :] = ke.astype(BF16)
            from_start, to_end = prod(whole[:i]), prod(whole[i + 1:])
            qb_ref[rows, :] = (qd if from_start is None else qd * from_start).astype(BF16)
            kb_ref[rows, :] = (ke if to_end is None else ke * to_end).astype(BF16)
            if i >= half:
                from_mid = prod(whole[half:i])
                qr_ref[rows, :] = (qd if from_mid is None else qd * from_mid).astype(BF16)
                kc_ref[rows, :] = jnp.zeros((HGRN_CHUNK, d_hgrn), BF16)
            else:
                to_mid = prod(whole[i + 1:half])
                kc_ref[rows, :] = (ke if to_mid is None else ke * to_mid).astype(BF16)

    rb = lax.broadcasted_iota(jnp.int32, (HGRN_BLOCK, HGRN_BLOCK), 0)
    cb = lax.broadcasted_iota(jnp.int32, (HGRN_BLOCK, HGRN_BLOCK), 1)
    rsub, csub = rb // HGRN_CHUNK, cb // HGRN_CHUNK
    m_diag = (rsub == csub) & (cb <= rb)
    m_adj = (csub == rsub - 1) & (rsub != half)
    eye = rb == cb
    hb = HGRN_BLOCK // 2
    nt_dims = (((1,), (1,)), ((), ()))
    tn_dims = (((0,), (0,)), ((), ()))
    fillers = chunks(pq_ref, o_q) + chunks(pg_ref, o_g)

    def fill(n):
        for _ in range(min(n, len(fillers))):
            fillers.pop(0)()

    per_stage = -(-len(fillers) // (3 * nblk))
    for b in range(nblk):
        rows = slice(b * HGRN_BLOCK, (b + 1) * HGRN_BLOCK)
        far = slice(b * HGRN_BLOCK + hb, (b + 1) * HGRN_BLOCK)
        g12s, g3s = [], []
        for hd in range(n_heads):
            sl = slice(hd * HEAD, (hd + 1) * HEAD)
            kk = jnp.concatenate([kd_ref[rows, sl], ke_ref[rows, sl]], axis=0)
            g12s.append(lax.dot_general(qd_ref[rows, sl], kk, nt_dims, preferred_element_type=F32))
            g3s.append(lax.dot_general(qr_ref[far, sl], kc_ref[rows, sl], nt_dims, preferred_element_type=F32))
        fill(per_stage)
        outs = []
        for hd in range(n_heads):
            sl = slice(hd * HEAD, (hd + 1) * HEAD)
            g12 = g12s[hd]
            scores = jnp.where(m_diag, g12[:, 0:HGRN_BLOCK], jnp.where(m_adj, g12[:, HGRN_BLOCK:], 0.0))
            scores = jnp.concatenate([scores[0:hb], scores[hb:] + g3s[hd]], axis=0).astype(BF16)
            v_blk = v_ref[rows, sl]
            st = st_ref[hd]
            lhs = jnp.concatenate([scores, qb_ref[rows, sl]], axis=1)
            rhs = jnp.concatenate([v_blk, st.astype(BF16)], axis=0)
            outs.append(jnp.dot(lhs, rhs, preferred_element_type=F32))
            d_col = jnp.sum(jnp.where(eye, dec_ref[b:b + 1, sl], 0.0), axis=1, keepdims=True)
            st_ref[hd] = st * d_col + lax.dot_general(kb_ref[rows, sl], v_blk, tn_dims,
                                                      preferred_element_type=F32)
        fill(per_stage)
        for hd in range(n_heads):
            sl = slice(hd * HEAD, (hd + 1) * HEAD)
            o = outs[hd]
            o = o * lax.rsqrt(jnp.mean(o * o, axis=-1, keepdims=True) + EPS)
            mixed_ref[rows, d_pool + hd * HEAD:d_pool + (hd + 1) * HEAD] = (o * gate_ref[rows, sl]).astype(BF16)
        fill(per_stage)
    fill(len(fillers))

    @pl.when(j == last)
    def _():
        hstate_ref[...] = st_ref[...]


def even_mix_prompt(x, g, w_in, wpool_bd, pscale, lbl, gain, to_cast, *, layer, tb):
    b, t, d = x.shape
    n_in = w_in.shape[1]
    d_pool = wpool_bd.shape[0]
    d_hgrn = gain.shape[1]
    n_heads = d_hgrn // HEAD
    tps = t // tb
    n_tiles = b * tps
    cast_in, cast_out, cast_shapes = _cast_specs(to_cast, n_tiles)
    kern = functools.partial(_even_prompt_kernel, n_cast=len(to_cast), layer=layer, tb=tb, tiles_per_seq=tps)

    def next_tile(s):
        tile = jnp.minimum(s + 1, n_tiles - 1)
        return tile // tps, tile % tps

    assert d_pool % LANES == 0 and LANES % POOL_GROUP_DIM == 0 and tb % HGRN_BLOCK == 0
    return pl.pallas_call(
        kern,
        grid=(n_tiles,),
        in_specs=[pl.BlockSpec((None, tb, d), lambda s: (*next_tile(s), 0)),
                  pl.BlockSpec((None, tb, d), lambda s: (0, 0, 0)),
                  _resident((1, d)), _resident((d, n_in)), _resident((d_pool, d_pool)),
                  _resident((1, d_pool)), _resident(lbl.shape), _resident((1, d_hgrn))] + cast_in,
        out_specs=[pl.BlockSpec((None, tb, d), lambda s: (s // tps, s % tps, 0)),
                   pl.BlockSpec((None, POOL_BUF, d_pool), lambda s: (s // tps, 0, 0)),
                   pl.BlockSpec((None, n_heads, HEAD, HEAD), lambda s: (s // tps, 0, 0, 0))] + cast_out,
        out_shape=[jax.ShapeDtypeStruct((b, t, d), BF16),
                   jax.ShapeDtypeStruct((b, POOL_BUF, d_pool), F32),
                   jax.ShapeDtypeStruct((b, n_heads, HEAD, HEAD), F32)] + cast_shapes,
        scratch_shapes=[pltpu.VMEM((tb, d_pool), F32)]
                       + [pltpu.VMEM((tb, d_hgrn), F32)] * 4
                       + [pltpu.VMEM((tb, d), BF16),
                        pltpu.VMEM((POOL_PAD + POOL_HIST + tb, d_pool), F32),
                        pltpu.VMEM((n_heads, HEAD, HEAD), F32),
                        pltpu.VMEM((tb, d_hgrn), F32),
                        pltpu.VMEM((tb, d_hgrn), F32),
                        pltpu.VMEM((tb // HGRN_BLOCK, d_hgrn), F32),
                        pltpu.VMEM((tb, d_hgrn), F32)]
                       + [pltpu.VMEM((tb, d_hgrn), BF16)] * 8,
        compiler_params=pltpu.CompilerParams(dimension_semantics=("arbitrary",),
                                             vmem_limit_bytes=VMEM_LIMIT_V7X),
        name="even_mix_prompt",
    )(x, x, g, w_in, wpool_bd, pscale, lbl, gain, *[w for w, _ in to_cast])


def _even_sample_tail_kernel(proj_ref, oraw_ref, pool_ref, x_ref, wpool_ref, pscale_ref, gain_ref, wout_ref, g_ref,
                             w1_ref, w2_ref, gf_ref, o_ref, npool_ref, *, layer, final_norm, ff_chunk, pos0):
    d_pool = wpool_ref.shape[0]
    d_hgrn = gain_ref.shape[1]

    u = proj_ref[:, 0:d_pool]
    acc = u
    snaps = {}
    for s in range(1, POOL_WINDOWS[-1]):
        acc = acc + pool_ref[POOL_BUF - s]
        if s + 1 in POOL_WINDOWS:
            snaps[s + 1] = acc
    ssum, win = _pool_select(snaps, u.shape)
    cnt = jnp.minimum(win, pos0 + 1).astype(F32)
    diff = (ssum / cnt - u).astype(BF16)
    pool_out = jnp.dot(diff, wpool_ref[...], preferred_element_type=F32) * pscale_ref[...]
    npool_ref[0:POOL_BUF - 1] = pool_ref[1:POOL_BUF]
    npool_ref[POOL_BUF - 1] = u

    gate = gain_ref[...] * _sigmoid(proj_ref[:, d_pool + 3 * d_hgrn:d_pool + 4 * d_hgrn])
    mixed = [pool_out.astype(BF16)]
    for hd in range(d_hgrn // HEAD):
        sl = slice(hd * HEAD, (hd + 1) * HEAD)
        o = oraw_ref[:, sl]
        o = o * lax.rsqrt(jnp.mean(o * o, axis=-1, keepdims=True) + EPS)
        mixed.append((o * gate[:, sl]).astype(BF16))
    o_ref[...] = _mlp_block(jnp.concatenate(mixed, axis=1), x_ref[...], wout_ref, g_ref, w1_ref, w2_ref, gf_ref,
                            layer=layer, final_norm=final_norm, ff_chunk=ff_chunk)


def even_sample_tail(proj, o_raw, pool_rows, x_s, wpool_bd, pscale, gain, w_out, g_mlp, w1, w2, g_final,
                     *, layer, final_norm, pos0):
    b, n_in = proj.shape
    d = x_s.shape[1]
    d_pool = wpool_bd.shape[0]
    d_hgrn = gain.shape[1]
    d_ff = w1.shape[1]
    kern = functools.partial(_even_sample_tail_kernel, layer=layer, final_norm=final_norm, ff_chunk=1024, pos0=pos0)
    return pl.pallas_call(
        kern,
        grid=(1,),
        in_specs=[_resident((b, n_in)), _resident((b, d_hgrn)), _resident((POOL_BUF, b, d_pool)), _resident((b, d)),
                  _resident((d_pool, d_pool)), _resident((1, d_pool)), _resident((1, d_hgrn)),
                  _resident((d, d)), _resident(g_mlp.shape), _resident((d, d_ff)), _resident((d_ff, d)),
                  _resident((1, d))],
        out_specs=[pl.BlockSpec((b, d), lambda i: (0, 0)), pl.BlockSpec((POOL_BUF, b, d_pool), lambda i: (0, 0, 0))],
        out_shape=[jax.ShapeDtypeStruct((b, d), F32), jax.ShapeDtypeStruct((POOL_BUF, b, d_pool), F32)],
        compiler_params=pltpu.CompilerParams(dimension_semantics=("arbitrary",),
                                             vmem_limit_bytes=VMEM_LIMIT_V7X),
        name="even_sample_tail",
    )(proj, o_raw, pool_rows, x_s, wpool_bd, pscale, gain, w_out, g_mlp, w1, w2, g_final)


def _odd_kernel(x_ref, xs_ref, cst_ref, g_ref, win_ref, cw_ref, mixed_ref, cstate_ref, ms_ref, ncst_ref, zext_ref,
                *, tb):
    i, j = pl.program_id(0), pl.program_id(1)
    last = pl.num_programs(1) - 1
    dc = cw_ref.shape[1]

    def gated_taps(x):
        h = _rms(x, g_ref[...]).astype(BF16)
        cg = jnp.dot(h, win_ref[:, dc:2 * dc], preferred_element_type=F32)
        hv = jnp.dot(h, win_ref[:, 2 * dc:3 * dc], preferred_element_type=F32)
        z = cg * hv
        bg = jnp.dot(h, win_ref[:, 0:dc], preferred_element_type=F32)
        return bg, z, cw_ref[CONV_WIDTH - 1:CONV_WIDTH, :] * z

    @pl.when(j == 0)
    def _():
        zext_ref[0:CONV_HIST, :] = jnp.zeros((CONV_HIST, dc), F32)

    bg, z, conv = gated_taps(x_ref[...])
    zext_ref[CONV_HIST:CONV_HIST + tb, :] = z
    for s in range(1, CONV_WIDTH):
        conv = conv + cw_ref[CONV_WIDTH - 1 - s:CONV_WIDTH - s, :] * zext_ref[pl.ds(CONV_HIST - s, tb), :]
    mixed_ref[...] = (bg * conv).astype(BF16)

    @pl.when(j == last)
    def _():
        cstate_ref[...] = zext_ref[pl.ds(CONV_HIST + tb - CONV_BUF, CONV_BUF), :]

    zext_ref[0:CONV_HIST, :] = zext_ref[tb:tb + CONV_HIST, :]

    @pl.when((i == pl.num_programs(0) - 1) & (j == last))
    def _():
        bg_s, z_s, conv_s = gated_taps(xs_ref[...])
        for s in range(1, CONV_WIDTH):
            conv_s = conv_s + cw_ref[CONV_WIDTH - 1 - s:CONV_WIDTH - s, :] * cst_ref[:, CONV_BUF - s, :]
        ms_ref[...] = (bg_s * conv_s).astype(BF16)
        for r in range(CONV_BUF - 1):
            ncst_ref[:, r, :] = cst_ref[:, r + 1, :]
        ncst_ref[:, CONV_BUF - 1, :] = z_s


def odd_mix(x, x_s, cstate_s, g, w_in, conv_w, *, tb):
    b, t, d = x.shape
    ms = x_s.shape[0]
    dc = conv_w.shape[1]
    kern = functools.partial(_odd_kernel, tb=tb)
    return pl.pallas_call(
        kern,
        grid=(b, t // tb),
        in_specs=[pl.BlockSpec((None, tb, d), lambda i, j: (i, j, 0)),
                  _resident((ms, d)), _resident((ms, CONV_BUF, dc)),
                  _resident((1, d)), _resident((d, 3 * dc)), _resident((CONV_WIDTH, dc))],
        out_specs=[pl.BlockSpec((None, tb, dc), lambda i, j: (i, j, 0)),
                   pl.BlockSpec((None, CONV_BUF, dc), lambda i, j: (i, 0, 0)),
                   pl.BlockSpec((ms, dc), lambda i, j: (0, 0)),
                   pl.BlockSpec((ms, CONV_BUF, dc), lambda i, j: (0, 0, 0))],
        out_shape=[jax.ShapeDtypeStruct((b, t, dc), BF16),
                   jax.ShapeDtypeStruct((b, CONV_BUF, dc), F32),
                   jax.ShapeDtypeStruct((ms, dc), BF16),
                   jax.ShapeDtypeStruct((ms, CONV_BUF, dc), F32)],
        scratch_shapes=[pltpu.VMEM((CONV_HIST + tb, dc), F32)],
        compiler_params=pltpu.CompilerParams(dimension_semantics=("arbitrary", "arbitrary"),
                                             vmem_limit_bytes=VMEM_LIMIT_V7X),
        name="odd_mix",
    )(x, x_s, cstate_s, g, w_in, conv_w)


def _block_diag(w):
    g, c, _ = w.shape
    rows = [jnp.pad(w[i], ((0, 0), (i * c, (g - 1 - i) * c))) for i in range(g)]
    return jnp.concatenate(rows, axis=0)


def kernel(x_prompt, x_sample, state_pool, state_hgrn, state_conv, norm_mix, norm_mlp, norm_final, even_w_in, pool_w, pool_scale, hgrn_lb_logits, hgrn_gain, even_w_out, odd_w_in, conv_w, odd_w_out, ff_w1, ff_w2):
    depth = norm_mix.shape[0]
    b, t, d = x_prompt.shape
    db, ds, _ = x_sample.shape
    assert ds == 1, "the sample group carries one token per sequence"
    xp = x_prompt.reshape(b * t, d)
    xs = x_sample.reshape(db, d)
    pool_p, hgrn_p, conv_p, pool_s, hgrn_s, conv_s = [], [], [], [], [], []

    def layer_weights(l):
        w_in, w_out = (even_w_in, even_w_out) if l % 2 == 0 else (odd_w_in, odd_w_out)
        return dict(w_in=(w_in[l // 2], None), w_out=(w_out[l // 2], None), w1=(ff_w1, l), w2=(ff_w2, l))

    def cast_now(w, layer):
        return (w if layer is None else w[layer]).astype(BF16)

    have = {}
    for l in range(depth):
        g_mix = norm_mix[l][None]
        wl = layer_weights(l)
        missing = [k for k in ("w_out", "w1", "w2") if k not in have]
        if l % 2 == 0:
            e = l // 2
            pool_bd = _block_diag(pool_w[e]).astype(BF16)
            shared = (pool_bd, pool_scale[e][None], hgrn_lb_logits, hgrn_gain[e][None])
            proj_s, have["w_in"] = norm_proj(xs, g_mix, have.get("w_in", wl["w_in"][0]), col_blocks=2)
            mixed_p, p_new, s_new, *cast = even_mix_prompt(xp.reshape(b, t, d), g_mix, have["w_in"], *shared,
                                                           [wl[k] for k in missing], layer=l, tb=256)
            have.update(zip(missing, cast))
            pool_p.append(p_new)
            hgrn_p.append(s_new)
            nxt = layer_weights(l + 1) if l + 1 < depth else {}
            xp, s_new, o_raw, *cast = out_mlp(
                mixed_p.reshape(b * t, d), xp, None, have["w_out"], norm_mlp, have["w1"], have["w2"],
                norm_final[None], list(nxt.values()),
                (proj_s, state_hgrn[e], hgrn_lb_logits, pool_bd.shape[0], l),
                layer=l, final_norm=l == depth - 1, bm=512)
            hgrn_s.append(s_new)
            xs, p_new = even_sample_tail(proj_s, o_raw.reshape(db, -1), jnp.swapaxes(state_pool[e], 0, 1), xs,
                                         pool_bd, pool_scale[e][None], hgrn_gain[e][None], have["w_out"], norm_mlp,
                                         have["w1"], have["w2"], norm_final[None],
                                         layer=l, final_norm=l == depth - 1, pos0=PAST_LEN)
            pool_s.append(jnp.swapaxes(p_new, 0, 1))
            have = dict(zip(nxt, cast))
            continue
        else:
            o = l // 2
            have.update({k: cast_now(*wl[k]) for k in ["w_in"] + missing if k not in have})
            mixed_p, c_new, mixed_s, cs_new = odd_mix(xp.reshape(b, t, d), xs, state_conv[o], g_mix,
                                                      have["w_in"], conv_w[o], tb=512)
            conv_p.append(c_new)
            conv_s.append(cs_new)
        nxt = layer_weights(l + 1) if l + 1 < depth else {}
        xp, xs, *cast = out_mlp(mixed_p.reshape(b * t, d), xp, (mixed_s, xs), have["w_out"], norm_mlp, have["w1"],
                                have["w2"], norm_final[None], list(nxt.values()),
                                layer=l, final_norm=l == depth - 1, bm=512)
        have = dict(zip(nxt, cast))
    return (xp.reshape(b, t, d), xs.reshape(db, ds, d), jnp.stack(pool_p), jnp.stack(hgrn_p), jnp.stack(conv_p),
            jnp.stack(pool_s), jnp.stack(hgrn_s), jnp.stack(conv_s))
```
